```python
import math
import jax, jax.numpy as jnp
from jax import lax
import numpy as np

D_MODEL = 1024
BATCH = 8
SEQ = 4096
DEPTH = 2

MEM_LEN = 256
NORM_EPS = 1e-6
N_BRANCH = 4
BRANCH_WIDTH = 256

MLA_HEADS = 4
MLA_Q_RANK = 256
MLA_KV_RANK = 128
MLA_NOPE = 64
MLA_ROPE = 32
MLA_V = 64
ROPE_THETA = 10000.0
ATTN_BLOCK = 128

SG_GROUPS = 4
SG_WIDTH = 256
SG_CHUNK = 128

GLA_HEADS = 4
GLA_DK = 32
GLA_DV = 64
GLA_GATE_RANK = 16
GLA_GATE_TEMP = 16.0
GLA_CHUNK = 64

DN_HEADS = 4
DN_DK = 64
DN_DV = 64
DN_CONV = 4
DN_CHUNK = 64

XATTN_HEADS = 4
XATTN_DIM = D_MODEL // XATTN_HEADS
MLP_HIDDEN = 4 * D_MODEL

IN_SIZES = (
    MLA_Q_RANK, MLA_KV_RANK, MLA_ROPE,
    2 * SG_WIDTH,
    GLA_HEADS * GLA_DK, GLA_HEADS * GLA_DK, GLA_HEADS * GLA_DV,
    GLA_GATE_RANK, GLA_HEADS * GLA_DV,
    DN_HEADS * DN_DK, DN_HEADS * DN_DK, DN_HEADS * DN_DV,
    DN_HEADS, DN_HEADS, DN_HEADS * DN_DV,
    N_BRANCH * D_MODEL,
)
IN_WIDTH = sum(IN_SIZES)

kernel_name = 'hybrid_parallel_mixer_decoder'


def rmsnorm(x, g, eps=NORM_EPS):
    xf = x.astype(jnp.float32)
    y = xf * lax.rsqrt(jnp.mean(xf * xf, axis=-1, keepdims=True) + eps)
    return (y * g).astype(x.dtype)


def layernorm(x, g, b, eps=NORM_EPS):
    xf = x.astype(jnp.float32)
    mu = jnp.mean(xf, axis=-1, keepdims=True)
    xc = xf - mu
    y = xc * lax.rsqrt(jnp.mean(xc * xc, axis=-1, keepdims=True) + eps)
    return (y * g + b).astype(x.dtype)


def l2norm(x, eps=NORM_EPS):
    return x * lax.rsqrt(jnp.sum(x * x, axis=-1, keepdims=True) + eps)


def rope_angles(positions, dim):
    inv_freq = ROPE_THETA ** (-jnp.arange(0, dim, 2, dtype=jnp.float32) / dim)
    ang = positions.astype(jnp.float32)[..., None] * inv_freq
    return jnp.cos(ang), jnp.sin(ang)


def apply_rope(x, cos, sin):
    half = x.shape[-1] // 2
    x1 = x[..., :half].astype(jnp.float32)
    x2 = x[..., half:].astype(jnp.float32)
    return jnp.concatenate([x1 * cos - x2 * sin, x2 * cos + x1 * sin], axis=-1).astype(x.dtype)


def to_chunks(t, chunk):
    b, s, h = t.shape[:3]
    t = t.reshape((b, s // chunk, chunk, h) + t.shape[3:])
    return t.transpose((1, 0, 3, 2) + tuple(range(4, t.ndim)))


def from_chunks(t):
    n, b, h, c = t.shape[:4]
    t = t.transpose((1, 0, 3, 2) + tuple(range(4, t.ndim)))
    return t.reshape((b, n * c, h) + t.shape[4:])


def causal_block_attention(q, k, v, scale):
    s_len = q.shape[2]
    outs = []
    for start in range(0, s_len, ATTN_BLOCK):
        end = start + ATTN_BLOCK
        s = jnp.einsum('bhqd,bhkd->bhqk', q[:, :, start:end], k[:, :, :end]).astype(jnp.float32) * scale
        qi = start + jnp.arange(ATTN_BLOCK)
        ki = jnp.arange(end)
        s = jnp.where(qi[:, None] >= ki[None, :], s, -jnp.inf)
        p = jax.nn.softmax(s, axis=-1).astype(v.dtype)
        outs.append(jnp.einsum('bhqk,bhkd->bhqd', p, v[:, :, :end]))
    return jnp.concatenate(outs, axis=2)


def mla_branch(cq, ckv, kpe, positions, norm_q, norm_kv, w_uq, w_ukv):
    b, s, _ = cq.shape
    h = MLA_HEADS
    q = (rmsnorm(cq, norm_q) @ w_uq).reshape(b, s, h, MLA_NOPE + MLA_ROPE)
    kv = (rmsnorm(ckv, norm_kv) @ w_ukv).reshape(b, s, h, MLA_NOPE + MLA_V)
    q_nope, q_pe = q[..., :MLA_NOPE], q[..., MLA_NOPE:]
    k_nope, v = kv[..., :MLA_NOPE], kv[..., MLA_NOPE:]
    cos, sin = rope_angles(positions, MLA_ROPE)
    q_pe = apply_rope(q_pe, cos[:, :, None, :], sin[:, :, None, :])
    k_pe = apply_rope(kpe, cos, sin)
    q = jnp.concatenate([q_nope, q_pe], axis=-1).transpose(0, 2, 1, 3)
    k = jnp.concatenate([k_nope, jnp.broadcast_to(k_pe[:, :, None, :], (b, s, h, MLA_ROPE))],
                        axis=-1).transpose(0, 2, 1, 3)
    v = v.transpose(0, 2, 1, 3)
    o = causal_block_attention(q, k, v, (MLA_NOPE + MLA_ROPE) ** -0.5)
    return o.transpose(0, 2, 1, 3).reshape(b, s, h * MLA_V)


def spatial_gating_branch(uv, ln_g, ln_b, w_s, b_s):
    b, s, _ = uv.shape
    uv = jax.nn.gelu(uv)
    u, v = uv[..., :SG_WIDTH], uv[..., SG_WIDTH:]
    v = layernorm(v, ln_g, ln_b)
    n = s // SG_CHUNK
    v = v.reshape(b, n, SG_CHUNK, SG_GROUPS, SG_WIDTH // SG_GROUPS)
    mask = jnp.tril(jnp.ones((SG_CHUNK, SG_CHUNK), dtype=w_s.dtype))
    mixed = jnp.einsum('gij,bnjgc->bnigc', w_s * mask, v) + b_s.T[:, :, None]
    return u * mixed.reshape(b, s, SG_WIDTH)


def gla_branch(q, k, v, gate_lr, out_gate, w_gate, b_gate, norm_o):
    b, s, _ = q.shape
    h, c = GLA_HEADS, GLA_CHUNK
    f32 = jnp.float32
    log_a = jax.nn.log_sigmoid((gate_lr @ w_gate + b_gate).astype(f32)) / GLA_GATE_TEMP
    q = q.astype(f32).reshape(b, s, h, GLA_DK) * (GLA_DK ** -0.5)
    k = k.astype(f32).reshape(b, s, h, GLA_DK)
    v = v.astype(f32).reshape(b, s, h, GLA_DV)
    log_a = log_a.reshape(b, s, h, GLA_DK)
    qc, kc, vc, ac = (to_chunks(t, c) for t in (q, k, v, log_a))
    idx = jnp.arange(c)
    incl = (idx[:, None] >= idx[None, :])[:, :, None]

    def step(state, inp):
        qi, ki, vi, ai = inp
        cum = jnp.cumsum(ai, axis=2)
        diff = cum[:, :, :, None, :] - cum[:, :, None, :, :]
        decay = jnp.where(incl, jnp.exp(jnp.where(incl, diff, 0.0)), 0.0)
        scores = jnp.einsum('bhik,bhjk,bhijk->bhij', qi, ki, decay)
        o = (jnp.einsum('bhik,bhkv->bhiv', qi * jnp.exp(cum), state)
             + jnp.einsum('bhij,bhjv->bhiv', scores, vi))
        last = cum[:, :, -1:, :]
        state = (jnp.exp(last[:, :, 0, :])[..., None] * state
                 + jnp.einsum('bhjk,bhjv->bhkv', ki * jnp.exp(last - cum), vi))
        return state, o

    state0 = jnp.zeros((b, h, GLA_DK, GLA_DV), f32)
    _, o = lax.scan(step, state0, (qc, kc, vc, ac))
    o = from_chunks(o)
    o = rmsnorm(o, norm_o) * jax.nn.silu(out_gate.astype(f32).reshape(b, s, h, GLA_DV))
    return o.reshape(b, s, h * GLA_DV)


def causal_depthwise_conv(x, w):
    return lax.conv_general_dilated(x, w[:, None, :], window_strides=(1,),
                                    padding=[(DN_CONV - 1, 0)],
                                    dimension_numbers=('NWC', 'WIO', 'NWC'),
                                    feature_group_count=x.shape[-1])


def deltanet_branch(q, k, v, a, bt, z, conv_w, a_log, dt_bias, norm_o):
    b, s, _ = q.shape
    h, c = DN_HEADS, DN_CHUNK
    f32 = jnp.float32
    qkv = jax.nn.silu(causal_depthwise_conv(jnp.concatenate([q, k, v], axis=-1), conv_w))
    q, k, v = jnp.split(qkv, [h * DN_DK, 2 * h * DN_DK], axis=-1)
    q = l2norm(q.astype(f32).reshape(b, s, h, DN_DK)) * (DN_DK ** -0.5)
    k = l2norm(k.astype(f32).reshape(b, s, h, DN_DK))
    v = v.astype(f32).reshape(b, s, h, DN_DV)
    beta = jax.nn.sigmoid(bt.astype(f32))
    g = -jnp.exp(a_log.astype(f32)) * jax.nn.softplus(a.astype(f32) + dt_bias.astype(f32))
    qc, kc, vc = (to_chunks(t, c) for t in (q, k, v))
    beta_c, g_c = to_chunks(beta, c), to_chunks(g, c)
    gam = jnp.cumsum(g_c, axis=-1)
    idx = jnp.arange(c)
    incl = idx[:, None] >= idx[None, :]
    strict = idx[:, None] > idx[None, :]
    diff = gam[..., :, None] - gam[..., None, :]
    dec_incl = jnp.where(incl, jnp.exp(jnp.where(incl, diff, 0.0)), 0.0)
    dec_strict = jnp.where(strict, dec_incl, 0.0)
    kk = jnp.einsum('nbhid,nbhjd->nbhij', kc, kc)
    lower = jnp.eye(c, dtype=f32) + beta_c[..., :, None] * kk * dec_strict
    rhs = jnp.concatenate([beta_c[..., None] * vc, (beta_c * jnp.exp(gam))[..., None] * kc], axis=-1)
    sol = lax.linalg.triangular_solve(lower, rhs, left_side=True, lower=True, unit_diagonal=True)
    w_c, x_c = sol[..., :DN_DV], sol[..., DN_DV:]
    attn = jnp.einsum('nbhid,nbhjd->nbhij', qc, kc) * dec_incl
    q_dec = qc * jnp.exp(gam)[..., None]
    k_dec = kc * jnp.exp(gam[..., -1:] - gam)[..., None]
    g_last = jnp.exp(gam[..., -1])

    def step(state, inp):
        w_i, x_i, attn_i, qd_i, kd_i, gl_i = inp
        u = w_i - jnp.einsum('bhck,bhkv->bhcv', x_i, state)
        o = jnp.einsum('bhck,bhkv->bhcv', qd_i, state) + jnp.einsum('bhij,bhjv->bhiv', attn_i, u)
        state = gl_i[..., None, None] * state + jnp.einsum('bhjk,bhjv->bhkv', kd_i, u)
        return state, o

    state0 = jnp.zeros((b, h, DN_DK, DN_DV), f32)
    _, o = lax.scan(step, state0, (w_c, x_c, attn, q_dec, k_dec, g_last))
    o = from_chunks(o)
    o = rmsnorm(o, norm_o) * jax.nn.silu(z.astype(f32).reshape(b, s, h, DN_DV))
    return o.reshape(b, s, h * DN_DV)


def mixer_block(h, positions, w_in, mla_norm_q, mla_norm_kv, mla_w_uq, mla_w_ukv,
                sg_ln_g, sg_ln_b, sg_w, sg_b, gla_w_gate, gla_b_gate, gla_norm,
                dn_conv, dn_a_log, dn_dt_bias, dn_norm, w_branch, w_out):
    b, s, _ = h.shape
    zin = h @ w_in
    splits = [int(i) for i in np.cumsum(IN_SIZES)[:-1]]
    (cq, ckv, kpe, sg_uv, gq, gk, gv, g_lr, g_og,
     dq, dk, dv, da, db, dz, gates) = jnp.split(zin, splits, axis=-1)
    o_a = mla_branch(cq, ckv, kpe, positions, mla_norm_q, mla_norm_kv, mla_w_uq, mla_w_ukv)
    o_b = spatial_gating_branch(sg_uv, sg_ln_g, sg_ln_b, sg_w, sg_b)
    o_c = gla_branch(gq, gk, gv, g_lr, g_og, gla_w_gate, gla_b_gate, gla_norm)
    o_d = deltanet_branch(dq, dk, dv, da, db, dz, dn_conv, dn_a_log, dn_dt_bias, dn_norm)
    branches = jnp.stack([o.astype(h.dtype) for o in (o_a, o_b, o_c, o_d)], axis=2)
    proj = jnp.einsum('bsnw,nwd->bsnd', branches, w_branch)
    gate = jax.nn.sigmoid(gates).reshape(b, s, N_BRANCH, D_MODEL)
    merged = jnp.sum(gate * proj, axis=2)
    return merged @ w_out


def cross_attention(h, mem_n, wq, wk, wv, wo):
    b, s, _ = h.shape
    m = mem_n.shape[1]
    q = (h @ wq).reshape(b, s, XATTN_HEADS, XATTN_DIM)
    k = (mem_n @ wk).reshape(b, m, XATTN_HEADS, XATTN_DIM)
    v = (mem_n @ wv).reshape(b, m, XATTN_HEADS, XATTN_DIM)
    sc = jnp.einsum('bshd,bmhd->bhsm', q, k).astype(jnp.float32) * (XATTN_DIM ** -0.5)
    p = jax.nn.softmax(sc, axis=-1).astype(v.dtype)
    o = jnp.einsum('bhsm,bmhd->bshd', p, v).reshape(b, s, XATTN_HEADS * XATTN_DIM)
    return o @ wo


def squared_relu_mlp(h, w_up, w_down):
    a = jax.nn.relu(h @ w_up)
    return (a * a) @ w_down


def setup_inputs(seed: int = 0) -> dict:
    key = jax.random.key(seed)
    ks = iter(jax.random.split(key, 48))
    f32 = jnp.float32
    L = DEPTH

    def nrm(shape, scale):
        return jax.random.normal(next(ks), shape, f32) * scale

    def gain(shape):
        return 1.0 + 0.1 * jax.random.normal(next(ks), shape, f32)

    x = nrm((BATCH, SEQ, D_MODEL), 1.0)
    mem = nrm((BATCH, MEM_LEN, D_MODEL), 1.0)
    offsets = jax.random.randint(next(ks), (BATCH, 1), 0, 1024, dtype=jnp.int32)
    positions = (offsets + jnp.arange(SEQ, dtype=jnp.int32)[None, :]).astype(jnp.int32)
    dn_ch = 2 * DN_HEADS * DN_DK + DN_HEADS * DN_DV
    a_init = jax.random.uniform(next(ks), (L, DN_HEADS), f32, 1.0, 16.0)
    dt = jnp.exp(jax.random.uniform(next(ks), (L, DN_HEADS), f32, math.log(1e-3), math.log(1e-1)))
    return {
        'x': x,
        'mem': mem,
        'positions': positions,
        'norm_mix': gain((L, D_MODEL)),
        'w_in': nrm((L, D_MODEL, IN_WIDTH), D_MODEL ** -0.5),
        'mla_norm_q': gain((L, MLA_Q_RANK)),
        'mla_norm_kv': gain((L, MLA_KV_RANK)),
        'mla_w_uq': nrm((L, MLA_Q_RANK, MLA_HEADS * (MLA_NOPE + MLA_ROPE)), MLA_Q_RANK ** -0.5),
        'mla_w_ukv': nrm((L, MLA_KV_RANK, MLA_HEADS * (MLA_NOPE + MLA_V)), MLA_KV_RANK ** -0.5),
        'sg_ln_g': gain((L, SG_WIDTH)),
        'sg_ln_b': nrm((L, SG_WIDTH), 0.02),
        'sg_w': nrm((L, SG_GROUPS, SG_CHUNK, SG_CHUNK), SG_CHUNK ** -0.5),
        'sg_b': 1.0 + nrm((L, SG_GROUPS, SG_CHUNK), 0.1),
        'gla_w_gate': nrm((L, GLA_GATE_RANK, GLA_HEADS * GLA_DK), GLA_GATE_RANK ** -0.5),
        'gla_b_gate': nrm((L, GLA_HEADS * GLA_DK), 0.1),
        'gla_norm': gain((L, GLA_DV)),
        'dn_conv': nrm((L, DN_CONV, dn_ch), DN_CONV ** -0.5),
        'dn_a_log': jnp.log(a_init),
        'dn_dt_bias': dt + jnp.log(-jnp.expm1(-dt)),
        'dn_norm': gain((L, DN_DV)),
        'w_branch': nrm((L, N_BRANCH, BRANCH_WIDTH, D_MODEL), BRANCH_WIDTH ** -0.5),
        'w_out': nrm((L, D_MODEL, D_MODEL), D_MODEL ** -0.5),
        'norm_xattn': gain((L, D_MODEL)),
        'norm_mem': gain((L, D_MODEL)),
        'xattn_wq': nrm((L, D_MODEL, XATTN_HEADS * XATTN_DIM), D_MODEL ** -0.5),
        'xattn_wk': nrm((L, D_MODEL, XATTN_HEADS * XATTN_DIM), D_MODEL ** -0.5),
        'xattn_wv': nrm((L, D_MODEL, XATTN_HEADS * XATTN_DIM), D_MODEL ** -0.5),
        'xattn_wo': nrm((L, XATTN_HEADS * XATTN_DIM, D_MODEL), D_MODEL ** -0.5),
        'norm_mlp': gain((L, D_MODEL)),
        'w_up': nrm((L, D_MODEL, MLP_HIDDEN), D_MODEL ** -0.5),
        'w_down': nrm((L, MLP_HIDDEN, D_MODEL), MLP_HIDDEN ** -0.5),
        'norm_final': gain((D_MODEL,)),
    }


def reference(x, mem, positions, norm_mix, w_in, mla_norm_q, mla_norm_kv, mla_w_uq, mla_w_ukv,
              sg_ln_g, sg_ln_b, sg_w, sg_b, gla_w_gate, gla_b_gate, gla_norm,
              dn_conv, dn_a_log, dn_dt_bias, dn_norm, w_branch, w_out,
              norm_xattn, norm_mem, xattn_wq, xattn_wk, xattn_wv, xattn_wo,
              norm_mlp, w_up, w_down, norm_final):
    for l in range(DEPTH):
        h = rmsnorm(x, norm_mix[l])
        x = x + mixer_block(h, positions, w_in[l], mla_norm_q[l], mla_norm_kv[l], mla_w_uq[l], mla_w_ukv[l],
                            sg_ln_g[l], sg_ln_b[l], sg_w[l], sg_b[l], gla_w_gate[l], gla_b_gate[l], gla_norm[l],
                            dn_conv[l], dn_a_log[l], dn_dt_bias[l], dn_norm[l], w_branch[l], w_out[l])
        x = x + cross_attention(rmsnorm(x, norm_xattn[l]), rmsnorm(mem, norm_mem[l]),
                                xattn_wq[l], xattn_wk[l], xattn_wv[l], xattn_wo[l])
        x = x + squared_relu_mlp(rmsnorm(x, norm_mlp[l]), w_up[l], w_down[l])
    return rmsnorm(x, norm_final)
```

```python
import functools

import jax
import jax.numpy as jnp
import numpy as np
from jax import lax
from jax.experimental import pallas as pl
from jax.experimental.pallas import tpu as pltpu

f32 = jnp.float32
bf16 = jnp.bfloat16
HIGHEST = lax.Precision.HIGHEST

D_MODEL = 1024
NORM_EPS = 1e-6
N_BRANCH = 4
BRANCH_WIDTH = 256

MLA_HEADS = 4
MLA_Q_RANK = 256
MLA_KV_RANK = 128
MLA_NOPE = 64
MLA_ROPE = 32
MLA_V = 64
ROPE_THETA = 10000.0
MLA_HEAD_PAD = 128

SG_GROUPS = 4
SG_WIDTH = 256
SG_CHUNK = 128

GLA_HEADS = 4
GLA_DK = 32
GLA_DV = 64
GLA_GATE_RANK = 16
GLA_GATE_TEMP = 16.0
GLA_CHUNK = 64

DN_HEADS = 4
DN_DK = 64
DN_DV = 64
DN_CONV = 4
DN_CHUNK = 64
DN_INV_BLOCK = 16

XATTN_HEADS = 4
XATTN_DIM = D_MODEL // XATTN_HEADS
MLP_HIDDEN = 4 * D_MODEL

IN_SIZES = (
    MLA_Q_RANK, MLA_KV_RANK, MLA_ROPE,
    2 * SG_WIDTH,
    GLA_HEADS * GLA_DK, GLA_HEADS * GLA_DK, GLA_HEADS * GLA_DV,
    GLA_GATE_RANK, GLA_HEADS * GLA_DV,
    DN_HEADS * DN_DK, DN_HEADS * DN_DK, DN_HEADS * DN_DV,
    DN_HEADS, DN_HEADS, DN_HEADS * DN_DV,
    N_BRANCH * D_MODEL,
)
IN_OFFS = tuple(int(v) for v in np.cumsum((0,) + IN_SIZES))

W_MLA = 512
W_SG = 512
W_GLA = 896
W_DN = 1152
W_SMALL = W_MLA + W_SG + W_GLA + W_DN

TOKEN_TILE = 512
ATTN_TILE = 256
SCAN_TILE = 512
MLP_CHUNK = 1024
VMEM_LIMIT = 56 * 1024 * 1024


def _cparams(n_axes):
    return pltpu.CompilerParams(dimension_semantics=("arbitrary",) * n_axes,
                                vmem_limit_bytes=VMEM_LIMIT)


def _const_spec(shape):
    nd = len(shape)
    return pl.BlockSpec(shape, lambda *_: (0,) * nd, pipeline_mode=pl.Buffered(1))


def _rms(x, g):
    return x * lax.rsqrt(jnp.mean(x * x, axis=-1, keepdims=True) + NORM_EPS) * g


def _sigmoid(x):
    return 1.0 / (1.0 + jnp.exp(-x))


def _softplus(x):
    return jnp.maximum(x, 0.0) + jnp.log1p(jnp.exp(-jnp.abs(x)))


def _dot(a, b):
    return jnp.dot(a, b, preferred_element_type=f32)


def _dot_hi(a, b):
    return jnp.dot(a, b, preferred_element_type=f32, precision=HIGHEST)


def _dot_nt(a, b):
    return lax.dot_general(a, b, (((1,), (1,)), ((), ())), preferred_element_type=f32)


def _dot_tn(a, b):
    return lax.dot_general(a, b, (((0,), (0,)), ((), ())), preferred_element_type=f32)


def _iota(shape, dim):
    return lax.broadcasted_iota(jnp.int32, shape, dim)


def _group_mean_matrix(width, group):
    r = _iota((width, width), 0) // group
    c = _iota((width, width), 1) // group
    return jnp.where(r == c, 1.0 / group, 0.0).astype(f32)


def _inproj_kernel(x_ref, g_ref, w_ref, mla_ref, sg_ref, gla_ref, dn_ref):
    h = _rms(x_ref[...], g_ref[...]).astype(bf16)
    lo = 0
    for ref, width in ((mla_ref, W_MLA), (sg_ref, W_SG), (gla_ref, W_GLA), (dn_ref, W_DN)):
        ref[...] = _dot(h, w_ref[:, lo:lo + width])
        lo += width


def _inproj(x2, g, w_small):
    t = x2.shape[0]
    tm = TOKEN_TILE
    outs = tuple(jax.ShapeDtypeStruct((t, w), f32) for w in (W_MLA, W_SG, W_GLA, W_DN))
    return pl.pallas_call(
        _inproj_kernel,
        out_shape=outs,
        grid=(t // tm,),
        in_specs=[pl.BlockSpec((tm, D_MODEL), lambda i: (i, 0)),
                  _const_spec((1, D_MODEL)),
                  _const_spec((D_MODEL, W_SMALL))],
        out_specs=tuple(pl.BlockSpec((tm, w), lambda i: (i, 0)) for w in (W_MLA, W_SG, W_GLA, W_DN)),
        compiler_params=_cparams(1),
        name="inproj",
    )(x2, g, w_small)


def _mla_pre_kernel(z_ref, pos_ref, invf_ref, nq_ref, nkv_ref, wuq_ref, wuk_ref, wuv_ref,
                    q_ref, k_ref, v_ref):
    z = z_ref[...]
    tm = z.shape[0]
    cq = z[:, :MLA_Q_RANK]
    ckv = z[:, MLA_Q_RANK:MLA_Q_RANK + MLA_KV_RANK]
    kpe = z[:, MLA_Q_RANK + MLA_KV_RANK:]

    ang = pos_ref[...].astype(f32) * invf_ref[...]
    cos = jnp.cos(ang)
    sin = jnp.sin(ang)
    lane = _iota((tm, MLA_HEAD_PAD), 1)
    half = MLA_ROPE // 2
    in_x1 = (lane >= MLA_NOPE) & (lane < MLA_NOPE + half)
    in_x2 = (lane >= MLA_NOPE + half) & (lane < MLA_NOPE + MLA_ROPE)
    c_tab = jnp.where(in_x1 | in_x2, cos, 1.0)
    s1_tab = jnp.where(in_x1, -sin, 0.0)
    s2_tab = jnp.where(in_x2, sin, 0.0)

    def rope(t):
        return (t * c_tab + pltpu.roll(t, MLA_HEAD_PAD - half, 1) * s1_tab
                + pltpu.roll(t, half, 1) * s2_tab)

    scale = (MLA_NOPE + MLA_ROPE) ** -0.5
    q = _dot(_rms(cq, nq_ref[...]).astype(bf16), wuq_ref[...])
    kvn = _rms(ckv, nkv_ref[...]).astype(bf16)
    kn = _dot(kvn, wuk_ref[...])
    v_ref[...] = _dot(kvn, wuv_ref[...]).astype(bf16)
    kpe_rot = rope(kpe)
    for h in range(MLA_HEADS):
        sl = slice(h * MLA_HEAD_PAD, (h + 1) * MLA_HEAD_PAD)
        q_ref[:, sl] = (rope(q[:, sl]) * scale).astype(bf16)
        k_ref[:, sl] = (kn[:, sl] + kpe_rot).astype(bf16)


def _mla_pre(z_mla, pos2, invf, nq, nkv, wuq, wuk, wuv):
    t = z_mla.shape[0]
    tm = TOKEN_TILE
    hp = MLA_HEADS * MLA_HEAD_PAD
    hv = MLA_HEADS * MLA_V
    return pl.pallas_call(
        _mla_pre_kernel,
        out_shape=(jax.ShapeDtypeStruct((t, hp), bf16),
                   jax.ShapeDtypeStruct((t, hp), bf16),
                   jax.ShapeDtypeStruct((t, hv), bf16)),
        grid=(t // tm,),
        in_specs=[pl.BlockSpec((tm, W_MLA), lambda i: (i, 0)),
                  pl.BlockSpec((tm, 1), lambda i: (i, 0)),
                  _const_spec((1, MLA_HEAD_PAD)),
                  _const_spec((1, MLA_Q_RANK)),
                  _const_spec((1, MLA_KV_RANK)),
                  _const_spec((MLA_Q_RANK, hp)),
                  _const_spec((MLA_KV_RANK, hp)),
                  _const_spec((MLA_KV_RANK, hv))],
        out_specs=(pl.BlockSpec((tm, hp), lambda i: (i, 0)),
                   pl.BlockSpec((tm, hp), lambda i: (i, 0)),
                   pl.BlockSpec((tm, hv), lambda i: (i, 0))),
        compiler_params=_cparams(1),
        name="mla_pre",
    )(z_mla, pos2, invf, nq, nkv, wuq, wuk, wuv)


def _mla_attn_kernel(q_ref, k_ref, v_ref, o_ref):
    i = pl.program_id(1)
    tq = q_ref.shape[0]
    row = _iota((tq, tq), 0)
    col = _iota((tq, tq), 1)
    outs = []
    for h in range(MLA_HEADS):
        qsl = slice(h * MLA_HEAD_PAD, (h + 1) * MLA_HEAD_PAD)
        vsl = slice(h * MLA_V, (h + 1) * MLA_V)
        q = q_ref[:, qsl]

        def step(j, carry, masked):
            m, l, acc = carry
            start = pl.multiple_of(j * tq, tq)
            kb = k_ref[pl.ds(start, tq), qsl]
            vb = v_ref[pl.ds(start, tq), vsl]
            s = _dot_nt(q, kb)
            if masked:
                s = jnp.where(row >= col, s, -jnp.inf)
            m_new = jnp.maximum(m, jnp.max(s, axis=-1, keepdims=True))
            alpha = jnp.exp(m - m_new)
            p = jnp.exp(s - m_new)
            l = alpha * l + jnp.sum(p, axis=-1, keepdims=True)
            acc = alpha * acc + _dot(p.astype(bf16), vb)
            return m_new, l, acc

        init = (jnp.full((tq, 1), -jnp.inf, f32), jnp.zeros((tq, 1), f32),
                jnp.zeros((tq, MLA_V), f32))
        carry = lax.fori_loop(0, i, functools.partial(step, masked=False), init)
        m, l, acc = step(i, carry, True)
        outs.append(acc / l)
    o_ref[...] = jnp.concatenate(outs, axis=-1).astype(bf16)


def _mla_attn(q, k, v, batch, seq):
    tq = ATTN_TILE
    nq = seq // tq
    hp = MLA_HEADS * MLA_HEAD_PAD
    hv = MLA_HEADS * MLA_V
    return pl.pallas_call(
        _mla_attn_kernel,
        out_shape=jax.ShapeDtypeStruct((batch * seq, hv), bf16),
        grid=(batch, nq),
        in_specs=[pl.BlockSpec((tq, hp), lambda b, i: (b * nq + i, 0)),
                  pl.BlockSpec((seq, hp), lambda b, i: (b, 0)),
                  pl.BlockSpec((seq, hv), lambda b, i: (b, 0))],
        out_specs=pl.BlockSpec((tq, hv), lambda b, i: (b * nq + i, 0)),
        compiler_params=_cparams(2),
        name="mla_attn",
    )(q, k, v)


def _sg_kernel(z_ref, lng_ref, lnb_ref, ws_ref, bst_ref, o_ref):
    z = z_ref[...]
    tm = z.shape[0]
    uv = 0.5 * z * (1.0 + jnp.tanh(np.sqrt(2.0 / np.pi) * (z + 0.044715 * (z * z * z))))
    u = uv[:, :SG_WIDTH]
    v = uv[:, SG_WIDTH:]
    mu = jnp.mean(v, axis=-1, keepdims=True)
    vc = v - mu
    vn = vc * lax.rsqrt(jnp.mean(vc * vc, axis=-1, keepdims=True) + NORM_EPS)
    vn = (vn * lng_ref[...] + lnb_ref[...]).astype(bf16)

    gw = SG_WIDTH // SG_GROUPS
    lane_group = _iota((SG_CHUNK, SG_WIDTH), 1) // gw
    tril = _iota((SG_CHUNK, SG_CHUNK), 0) >= _iota((SG_CHUNK, SG_CHUNK), 1)
    bias = jnp.zeros((SG_CHUNK, SG_WIDTH), f32)
    wm = []
    for g in range(SG_GROUPS):
        wm.append(jnp.where(tril, ws_ref[g], 0.0).astype(bf16))
        bias = jnp.where(lane_group == g,
                         jnp.broadcast_to(bst_ref[:, g:g + 1], (SG_CHUNK, SG_WIDTH)), bias)
    for c in range(tm // SG_CHUNK):
        rows = slice(c * SG_CHUNK, (c + 1) * SG_CHUNK)
        vch = vn[rows]
        mixed = bias
        for g in range(SG_GROUPS):
            mixed = mixed + jnp.where(lane_group == g, _dot(wm[g], vch), 0.0)
        o_ref[rows, :] = (u[rows] * mixed).astype(bf16)


def _sg(z_sg, lng, lnb, ws, bst):
    t = z_sg.shape[0]
    tm = TOKEN_TILE
    return pl.pallas_call(
        _sg_kernel,
        out_shape=jax.ShapeDtypeStruct((t, SG_WIDTH), bf16),
        grid=(t // tm,),
        in_specs=[pl.BlockSpec((tm, W_SG), lambda i: (i, 0)),
                  _const_spec((1, SG_WIDTH)),
                  _const_spec((1, SG_WIDTH)),
                  _const_spec((SG_GROUPS, SG_CHUNK, SG_CHUNK)),
                  _const_spec((SG_CHUNK, SG_GROUPS))],
        out_specs=pl.BlockSpec((tm, SG_WIDTH), lambda i: (i, 0)),
        compiler_params=_cparams(1),
        name="spatial_gating",
    )(z_sg, lng, lnb, ws, bst)


GLA_QK = GLA_HEADS * GLA_DK
GLA_V = GLA_HEADS * GLA_DV
GLA_J_GROUP = 8


def _gla_kernel(z_ref, wg_ref, bg_ref, no_ref, o_ref, st_ref, la_ref, cum_ref, kc_ref, vc_ref):
    tc = z_ref.shape[0]
    c_len = GLA_CHUNK

    @pl.when(pl.program_id(1) == 0)
    def _():
        st_ref[...] = jnp.zeros_like(st_ref)

    zg = _dot(z_ref[:, 768:896].astype(bf16), wg_ref[...]) + bg_ref[...]
    log_sig = jnp.minimum(zg, 0.0) - jnp.log1p(jnp.exp(-jnp.abs(zg)))
    la_ref[...] = log_sig * (1.0 / GLA_GATE_TEMP)

    tri = (_iota((c_len, c_len), 0) >= _iota((c_len, c_len), 1)).astype(f32)
    expand = (_iota((GLA_QK, GLA_V), 0) // GLA_DK == _iota((GLA_QK, GLA_V), 1) // GLA_DV).astype(bf16)
    st_mask = (_iota((GLA_V, GLA_QK), 0) // GLA_DV == _iota((GLA_V, GLA_QK), 1) // GLA_DK).astype(f32)
    mean_mat = _group_mean_matrix(GLA_V, GLA_DV)
    row = _iota((c_len, GLA_QK), 0)

    def chunk(c, carry):
        base = pl.multiple_of(c * c_len, c_len)
        rows = pl.ds(base, c_len)
        cum = _dot_hi(tri, la_ref[rows, :])
        cum_ref[...] = cum
        q = z_ref[rows, 0:128] * (GLA_DK ** -0.5)
        k = z_ref[rows, 128:256]
        v = z_ref[rows, 256:512]
        kc_ref[...] = k
        vc_ref[...] = v
        st = st_ref[...]
        o = _dot_nt((q * jnp.exp(cum)).astype(bf16), st.astype(bf16))
        last = cum[c_len - 1:c_len, :]
        kdec = k * jnp.exp(last - cum)
        st_ref[...] = jnp.exp(last) * st + _dot_tn(v.astype(bf16), kdec.astype(bf16)) * st_mask

        def jgroup(jg, acc):
            jb = jg * GLA_J_GROUP
            ts = []
            for jj in range(GLA_J_GROUP):
                j = jb + jj
                kj = kc_ref[pl.ds(j, 1), :]
                cj = cum_ref[pl.ds(j, 1), :]
                dec = jnp.exp(jnp.where(row >= j, cum - cj, -1e30))
                ts.append((q * kj * dec).astype(bf16))
            r = _dot(jnp.concatenate(ts, axis=0), expand)
            for jj in range(GLA_J_GROUP):
                vj = vc_ref[pl.ds(jb + jj, 1), :]
                acc = acc + r[jj * c_len:(jj + 1) * c_len] * vj
            return acc

        o = o + lax.fori_loop(0, c_len // GLA_J_GROUP, jgroup, jnp.zeros((c_len, GLA_V), f32))
        ms = _dot_hi(o * o, mean_mat)
        og = z_ref[rows, 512:768]
        o_ref[rows, :] = (o * lax.rsqrt(ms + NORM_EPS) * no_ref[...] * (og * _sigmoid(og))).astype(bf16)
        return carry

    lax.fori_loop(0, tc // c_len, chunk, 0)


def _gla(z_gla, wg, bg, no, batch, seq):
    tc = SCAN_TILE
    nt = seq // tc
    return pl.pallas_call(
        _gla_kernel,
        out_shape=jax.ShapeDtypeStruct((batch * seq, GLA_V), bf16),
        grid=(batch, nt),
        in_specs=[pl.BlockSpec((tc, W_GLA), lambda b, i: (b * nt + i, 0)),
                  _const_spec((128, GLA_QK)),
                  _const_spec((1, GLA_QK)),
                  _const_spec((1, GLA_V))],
        out_specs=pl.BlockSpec((tc, GLA_V), lambda b, i: (b * nt + i, 0)),
        scratch_shapes=[pltpu.VMEM((GLA_V, GLA_QK), f32),
                        pltpu.VMEM((tc, GLA_QK), f32),
                        pltpu.VMEM((GLA_CHUNK, GLA_QK), f32),
                        pltpu.VMEM((GLA_CHUNK, GLA_QK), f32),
                        pltpu.VMEM((GLA_CHUNK, GLA_V), f32)],
        compiler_params=_cparams(2),
        name="gla",
    )(z_gla, wg, bg, no)


DN_W = DN_HEADS * DN_DK
DN_CONV_PAD = 8


def _unit_lower_inverse(a):
    c = a.shape[0]
    r_i = _iota((c, c), 0)
    c_i = _iota((c, c), 1)
    eye = (r_i == c_i).astype(f32)
    same_block = (r_i // DN_INV_BLOCK) == (c_i // DN_INV_BLOCK)
    d = jnp.where(same_block, a, 0.0)
    n = a - d
    d2 = _dot_hi(d, d)
    d4 = _dot_hi(d2, d2)
    d8 = _dot_hi(d4, d4)
    td = _dot_hi(_dot_hi(eye - d, eye + d2), _dot_hi(eye + d4, eye + d8))
    m = _dot_hi(td, n)
    m2 = _dot_hi(m, m)
    return _dot_hi(_dot_hi(eye - m, eye + m2), td)


def _dn_kernel(z_ref, cw_ref, alog_ref, dtb_ref, no_ref, o_ref,
               st_ref, xbuf_ref, q_s, k_s, v_s, beta_s, g_s):
    tc = z_ref.shape[0]
    c_len = DN_CHUNK
    qkv_w = 3 * DN_W

    @pl.when(pl.program_id(1) == 0)
    def _():
        st_ref[...] = jnp.zeros_like(st_ref)
        xbuf_ref[0:DN_CONV_PAD, :] = jnp.zeros((DN_CONV_PAD, qkv_w), f32)

    xbuf_ref[DN_CONV_PAD:DN_CONV_PAD + tc, :] = z_ref[:, 0:qkv_w]
    y = jnp.zeros((tc, qkv_w), f32)
    for kk in range(DN_CONV):
        y = y + cw_ref[kk:kk + 1, :] * xbuf_ref[pl.ds(DN_CONV_PAD - (DN_CONV - 1) + kk, tc), :]
    xbuf_ref[0:DN_CONV_PAD, :] = xbuf_ref[tc:tc + DN_CONV_PAD, :]
    y = y * _sigmoid(y)

    sum_mat = _group_mean_matrix(DN_W, DN_DK) * float(DN_DK)
    q = y[:, 0:DN_W]
    k = y[:, DN_W:2 * DN_W]
    q_s[...] = q * lax.rsqrt(_dot_hi(q * q, sum_mat) + NORM_EPS) * (DN_DK ** -0.5)
    k_s[...] = k * lax.rsqrt(_dot_hi(k * k, sum_mat) + NORM_EPS)
    v_s[...] = y[:, 2 * DN_W:3 * DN_W]

    ab = z_ref[:, 4 * DN_W:4 * DN_W + 128]
    lane_head = _iota((tc, DN_W), 1) // DN_DK
    a_rep = jnp.zeros((tc, DN_W), f32)
    b_rep = jnp.zeros((tc, DN_W), f32)
    for h in range(DN_HEADS):
        a_rep = jnp.where(lane_head == h, jnp.broadcast_to(ab[:, h:h + 1], (tc, DN_W)), a_rep)
        b_rep = jnp.where(lane_head == h,
                          jnp.broadcast_to(ab[:, DN_HEADS + h:DN_HEADS + h + 1], (tc, DN_W)), b_rep)
    beta_s[...] = _sigmoid(b_rep)
    g_s[...] = -jnp.exp(alog_ref[...]) * _softplus(a_rep + dtb_ref[...])

    r64 = _iota((c_len, c_len), 0)
    c64 = _iota((c_len, c_len), 1)
    tri = (r64 >= c64).astype(f32)
    ones = jnp.ones((c_len, c_len), f32)
    row = _iota((c_len, DN_W), 0)
    colj = _iota((c_len, DN_W), 1) % c_len
    eye4 = (row == colj).astype(f32)
    incl = row >= colj
    strict = row > colj
    mean_mat = _group_mean_matrix(DN_W, DN_DV)

    def chunk(c, carry):
        base = pl.multiple_of(c * c_len, c_len)
        rows = pl.ds(base, c_len)
        gam = _dot_hi(tri, g_s[rows, :])
        gam_row = _dot_hi(ones, eye4 * gam)
        dec_incl = jnp.exp(jnp.where(incl, gam - gam_row, -1e30))
        dec_strict = jnp.where(strict, dec_incl, 0.0)
        eg = jnp.exp(gam)
        g_last = gam[c_len - 1:c_len, :]
        eg_last = jnp.exp(g_last)
        k_scale = jnp.exp(g_last - gam)
        q = q_s[rows, :]
        k = k_s[rows, :]
        v = v_s[rows, :]
        beta = beta_s[rows, :]
        outs = []
        for h in range(DN_HEADS):
            sl = slice(h * DN_DK, (h + 1) * DN_DK)
            qh, kh, vh, bh = q[:, sl], k[:, sl], v[:, sl], beta[:, sl]
            khb = kh.astype(bf16)
            a_mat = bh * _dot_nt(khb, khb) * dec_strict[:, sl]
            t_inv = _unit_lower_inverse(a_mat)
            rhs = jnp.concatenate([bh * vh, bh * eg[:, sl] * kh], axis=-1)
            sol = _dot_hi(t_inv, rhs)
            w_c = sol[:, :DN_DV]
            x_c = sol[:, DN_DV:]
            attn = _dot_nt(qh.astype(bf16), khb) * dec_incl[:, sl]
            st = st_ref[h]
            stb = st.astype(bf16)
            u = w_c - _dot(x_c.astype(bf16), stb)
            ub = u.astype(bf16)
            outs.append(_dot((qh * eg[:, sl]).astype(bf16), stb) + _dot(attn.astype(bf16), ub))
            st_ref[h] = eg_last[:, sl] * st + _dot_tn((kh * k_scale[:, sl]).astype(bf16), ub)
        o = jnp.concatenate(outs, axis=-1)
        ms = _dot_hi(o * o, mean_mat)
        zg = z_ref[rows, 3 * DN_W:4 * DN_W]
        o_ref[rows, :] = (o * lax.rsqrt(ms + NORM_EPS) * no_ref[...] * (zg * _sigmoid(zg))).astype(bf16)
        return carry

    lax.fori_loop(0, tc // c_len, chunk, 0)


def _dn(z_dn, cw, alog, dtb, no, batch, seq):
    tc = SCAN_TILE
    nt = seq // tc
    scr = lambda: pltpu.VMEM((tc, DN_W), f32)
    return pl.pallas_call(
        _dn_kernel,
        out_shape=jax.ShapeDtypeStruct((batch * seq, DN_W), bf16),
        grid=(batch, nt),
        in_specs=[pl.BlockSpec((tc, W_DN), lambda b, i: (b * nt + i, 0)),
                  _const_spec((DN_CONV, 3 * DN_W)),
                  _const_spec((1, DN_W)),
                  _const_spec((1, DN_W)),
                  _const_spec((1, DN_W))],
        out_specs=pl.BlockSpec((tc, DN_W), lambda b, i: (b * nt + i, 0)),
        scratch_shapes=[pltpu.VMEM((DN_HEADS, DN_DK, DN_DV), f32),
                        pltpu.VMEM((tc + 2 * DN_CONV_PAD, 3 * DN_W), f32),
                        scr(), scr(), scr(), scr(), scr()],
        compiler_params=_cparams(2),
        name="deltanet",
    )(z_dn, cw, alog, dtb, no)


def _merge_kernel(x_ref, g_ref, oa_ref, ob_ref, oc_ref, od_ref, wg_ref, wb_ref, wo_ref, out_ref):
    x = x_ref[...]
    h = _rms(x, g_ref[...]).astype(bf16)
    merged = jnp.zeros(x.shape, f32)
    for n, o_ref in enumerate((oa_ref, ob_ref, oc_ref, od_ref)):
        gate = _sigmoid(_dot(h, wg_ref[:, n * D_MODEL:(n + 1) * D_MODEL]))
        merged = merged + gate * _dot(o_ref[...], wb_ref[n])
    out_ref[...] = x + _dot(merged.astype(bf16), wo_ref[...])


def _merge(x2, g, oa, ob, oc, od, wg, wb, wo):
    t = x2.shape[0]
    tm = TOKEN_TILE
    tok = lambda w: pl.BlockSpec((tm, w), lambda i: (i, 0))
    return pl.pallas_call(
        _merge_kernel,
        out_shape=jax.ShapeDtypeStruct((t, D_MODEL), f32),
        grid=(t // tm,),
        in_specs=[tok(D_MODEL), _const_spec((1, D_MODEL)),
                  tok(BRANCH_WIDTH), tok(BRANCH_WIDTH), tok(BRANCH_WIDTH), tok(BRANCH_WIDTH),
                  _const_spec((D_MODEL, N_BRANCH * D_MODEL)),
                  _const_spec((N_BRANCH, BRANCH_WIDTH, D_MODEL)),
                  _const_spec((D_MODEL, D_MODEL))],
        out_specs=tok(D_MODEL),
        compiler_params=_cparams(1),
        name="merge",
    )(x2, g, oa, ob, oc, od, wg, wb, wo)


def _memkv_kernel(mem_ref, g_ref, wk_ref, wv_ref, k_ref, v_ref):
    mn = _rms(mem_ref[...], g_ref[...]).astype(bf16)
    k_ref[...] = _dot(mn, wk_ref[...]).astype(bf16)
    v_ref[...] = _dot(mn, wv_ref[...]).astype(bf16)


def _memkv(mem2, g, wk, wv, mem_len):
    t = mem2.shape[0]
    blk = lambda: pl.BlockSpec((mem_len, D_MODEL), lambda i: (i, 0))
    return pl.pallas_call(
        _memkv_kernel,
        out_shape=(jax.ShapeDtypeStruct((t, D_MODEL), bf16), jax.ShapeDtypeStruct((t, D_MODEL), bf16)),
        grid=(t // mem_len,),
        in_specs=[blk(), _const_spec((1, D_MODEL)),
                  _const_spec((D_MODEL, D_MODEL)), _const_spec((D_MODEL, D_MODEL))],
        out_specs=(blk(), blk()),
        compiler_params=_cparams(1),
        name="mem_kv",
    )(mem2, g, wk, wv)


def _xattn_kernel(x_ref, g_ref, k_ref, v_ref, wq_ref, wo_ref, out_ref):
    x = x_ref[...]
    h = _rms(x, g_ref[...]).astype(bf16)
    q = (_dot(h, wq_ref[...]) * (XATTN_DIM ** -0.5)).astype(bf16)
    outs = []
    for hh in range(XATTN_HEADS):
        sl = slice(hh * XATTN_DIM, (hh + 1) * XATTN_DIM)
        s = _dot_nt(q[:, sl], k_ref[:, sl])
        p = jnp.exp(s - jnp.max(s, axis=-1, keepdims=True))
        p = p / jnp.sum(p, axis=-1, keepdims=True)
        outs.append(_dot(p.astype(bf16), v_ref[:, sl]).astype(bf16))
    o = jnp.concatenate(outs, axis=-1)
    out_ref[...] = x + _dot(o, wo_ref[...])


def _xattn(x2, g, kmem, vmem, wq, wo, batch, seq, mem_len):
    tm = TOKEN_TILE
    nt = seq // tm
    tok = pl.BlockSpec((tm, D_MODEL), lambda b, i: (b * nt + i, 0))
    memb = lambda: pl.BlockSpec((mem_len, D_MODEL), lambda b, i: (b, 0))
    return pl.pallas_call(
        _xattn_kernel,
        out_shape=jax.ShapeDtypeStruct((batch * seq, D_MODEL), f32),
        grid=(batch, nt),
        in_specs=[tok, _const_spec((1, D_MODEL)), memb(), memb(),
                  _const_spec((D_MODEL, D_MODEL)), _const_spec((D_MODEL, D_MODEL))],
        out_specs=tok,
        compiler_params=_cparams(2),
        name="xattn",
    )(x2, g, kmem, vmem, wq, wo)


def _mlp_kernel(x_ref, g_ref, wu_ref, wd_ref, gf_ref, out_ref, *, final):
    x = x_ref[...]
    h = _rms(x, g_ref[...]).astype(bf16)
    acc = x
    for c in range(MLP_HIDDEN // MLP_CHUNK):
        sl = slice(c * MLP_CHUNK, (c + 1) * MLP_CHUNK)
        a = jnp.maximum(_dot(h, wu_ref[:, sl]), 0.0)
        acc = acc + _dot((a * a).astype(bf16), wd_ref[sl, :])
    if final:
        acc = _rms(acc, gf_ref[...])
    out_ref[...] = acc


def _mlp(x2, g, wu, wd, gf, final):
    t = x2.shape[0]
    tm = TOKEN_TILE
    tok = pl.BlockSpec((tm, D_MODEL), lambda i: (i, 0))
    return pl.pallas_call(
        functools.partial(_mlp_kernel, final=final),
        out_shape=jax.ShapeDtypeStruct((t, D_MODEL), f32),
        grid=(t // tm,),
        in_specs=[tok, _const_spec((1, D_MODEL)),
                  _const_spec((D_MODEL, MLP_HIDDEN)), _const_spec((MLP_HIDDEN, D_MODEL)),
                  _const_spec((1, D_MODEL))],
        out_specs=tok,
        compiler_params=_cparams(1),
        name="mlp",
    )(x2, g, wu, wd, gf)


def _pad_cols(w, width):
    return jnp.pad(w, ((0, 0), (0, width - w.shape[1])))


def _prep_w_in(w_in):
    col = lambda i: w_in[:, IN_OFFS[i]:IN_OFFS[i + 1]]
    zeros = lambda n: jnp.zeros((D_MODEL, n), w_in.dtype)
    (cq, ckv, kpe, sg_uv, gq, gk, gv, g_lr, g_og, dq, dk, dv, da, db, dz) = (col(i) for i in range(15))
    small = jnp.concatenate([
        cq, ckv, zeros(MLA_NOPE), kpe, zeros(MLA_HEAD_PAD - MLA_NOPE - MLA_ROPE),
        sg_uv,
        gq, gk, gv, g_og, _pad_cols(g_lr, 128),
        dq, dk, dv, dz, _pad_cols(jnp.concatenate([da, db], axis=1), 128),
    ], axis=1)
    assert small.shape[1] == W_SMALL
    return small.astype(bf16), col(15).astype(bf16)


def _rope_lane_table():
    inv_freq = ROPE_THETA ** (-jnp.arange(0, MLA_ROPE, 2, dtype=f32) / MLA_ROPE)
    half = MLA_ROPE // 2
    tab = jnp.zeros((MLA_HEAD_PAD,), f32)
    tab = tab.at[MLA_NOPE:MLA_NOPE + half].set(inv_freq)
    tab = tab.at[MLA_NOPE + half:MLA_NOPE + MLA_ROPE].set(inv_freq)
    return tab[None, :]


def kernel(x, mem, positions, norm_mix, w_in, mla_norm_q, mla_norm_kv, mla_w_uq, mla_w_ukv, sg_ln_g, sg_ln_b, sg_w, sg_b, gla_w_gate, gla_b_gate, gla_norm, dn_conv, dn_a_log, dn_dt_bias, dn_norm, w_branch, w_out, norm_xattn, norm_mem, xattn_wq, xattn_wk, xattn_wv, xattn_wo, norm_mlp, w_up, w_down, norm_final):
    batch, seq, _ = x.shape
    mem_len = mem.shape[1]
    depth = w_in.shape[0]
    assert seq % TOKEN_TILE == 0 and seq % SCAN_TILE == 0 and seq % ATTN_TILE == 0
    x2 = x.reshape(batch * seq, D_MODEL)
    mem2 = mem.reshape(batch * mem_len, D_MODEL)
    pos2 = positions.reshape(batch * seq, 1)
    invf = _rope_lane_table()
    row = lambda v: v[None, :]

    for l in range(depth):
        w_small, w_gates = _prep_w_in(w_in[l])
        z_mla, z_sg, z_gla, z_dn = _inproj(x2, row(norm_mix[l]), w_small)

        wuq = jnp.pad(mla_w_uq[l].reshape(MLA_Q_RANK, MLA_HEADS, MLA_NOPE + MLA_ROPE),
                      ((0, 0), (0, 0), (0, MLA_HEAD_PAD - MLA_NOPE - MLA_ROPE)))
        wuq = wuq.reshape(MLA_Q_RANK, MLA_HEADS * MLA_HEAD_PAD).astype(bf16)
        wukv = mla_w_ukv[l].reshape(MLA_KV_RANK, MLA_HEADS, MLA_NOPE + MLA_V)
        wuk = jnp.pad(wukv[:, :, :MLA_NOPE], ((0, 0), (0, 0), (0, MLA_HEAD_PAD - MLA_NOPE)))
        wuk = wuk.reshape(MLA_KV_RANK, MLA_HEADS * MLA_HEAD_PAD).astype(bf16)
        wuv = wukv[:, :, MLA_NOPE:].reshape(MLA_KV_RANK, MLA_HEADS * MLA_V).astype(bf16)
        q_a, k_a, v_a = _mla_pre(z_mla, pos2, invf, row(mla_norm_q[l]), row(mla_norm_kv[l]), wuq, wuk, wuv)
        o_a = _mla_attn(q_a, k_a, v_a, batch, seq)

        o_b = _sg(z_sg, row(sg_ln_g[l]), row(sg_ln_b[l]), sg_w[l], sg_b[l].T)

        wg = jnp.pad(gla_w_gate[l], ((0, 128 - GLA_GATE_RANK), (0, 0))).astype(bf16)
        o_c = _gla(z_gla, wg, row(gla_b_gate[l]), row(jnp.tile(gla_norm[l], GLA_HEADS)), batch, seq)

        o_d = _dn(z_dn, dn_conv[l], row(jnp.repeat(dn_a_log[l], DN_DK)), row(jnp.repeat(dn_dt_bias[l], DN_DK)),
                  row(jnp.tile(dn_norm[l], DN_HEADS)), batch, seq)

        x2 = _merge(x2, row(norm_mix[l]), o_a, o_b, o_c, o_d, w_gates,
                    w_branch[l].astype(bf16), w_out[l].astype(bf16))

        k_m, v_m = _memkv(mem2, row(norm_mem[l]), xattn_wk[l].astype(bf16), xattn_wv[l].astype(bf16), mem_len)
        x2 = _xattn(x2, row(norm_xattn[l]), k_m, v_m, xattn_wq[l].astype(bf16), xattn_wo[l].astype(bf16),
                    batch, seq, mem_len)
        x2 = _mlp(x2, row(norm_mlp[l]), w_up[l].astype(bf16), w_down[l].astype(bf16), row(norm_final),
                  final=(l == depth - 1))
    return x2.reshape(batch, seq, D_MODEL)
```

```python
import functools

import jax
import jax.numpy as jnp
import numpy as np
from jax import lax
from jax.experimental import pallas as pl
from jax.experimental.pallas import tpu as pltpu

f32 = jnp.float32
bf16 = jnp.bfloat16
HIGHEST = lax.Precision.HIGHEST

D_MODEL = 1024
NORM_EPS = 1e-6
N_BRANCH = 4
BRANCH_WIDTH = 256

MLA_HEADS = 4
MLA_Q_RANK = 256
MLA_KV_RANK = 128
MLA_NOPE = 64
MLA_ROPE = 32
MLA_V = 64
ROPE_THETA = 10000.0
MLA_HEAD_PAD = 128

SG_GROUPS = 4
SG_WIDTH = 256
SG_CHUNK = 128

GLA_HEADS = 4
GLA_DK = 32
GLA_DV = 64
GLA_GATE_RANK = 16
GLA_GATE_TEMP = 16.0
GLA_CHUNK = 64

DN_HEADS = 4
DN_DK = 64
DN_DV = 64
DN_CONV = 4
DN_CHUNK = 64
DN_INV_BLOCK = 16

XATTN_HEADS = 4
XATTN_DIM = D_MODEL // XATTN_HEADS
MLP_HIDDEN = 4 * D_MODEL

IN_SIZES = (
    MLA_Q_RANK, MLA_KV_RANK, MLA_ROPE,
    2 * SG_WIDTH,
    GLA_HEADS * GLA_DK, GLA_HEADS * GLA_DK, GLA_HEADS * GLA_DV,
    GLA_GATE_RANK, GLA_HEADS * GLA_DV,
    DN_HEADS * DN_DK, DN_HEADS * DN_DK, DN_HEADS * DN_DV,
    DN_HEADS, DN_HEADS, DN_HEADS * DN_DV,
    N_BRANCH * D_MODEL,
)
IN_OFFS = tuple(int(v) for v in np.cumsum((0,) + IN_SIZES))

W_MLA = 512
W_SG = 512
W_GLA = 896
W_DN = 1152
W_SMALL = W_MLA + W_SG + W_GLA + W_DN

TOKEN_TILE = 512
ATTN_TILE = 256
SCAN_TILE = 512
MLP_CHUNK = 1024
VMEM_LIMIT = 56 * 1024 * 1024


def _cparams(n_axes):
    return pltpu.CompilerParams(dimension_semantics=("arbitrary",) * n_axes,
                                vmem_limit_bytes=VMEM_LIMIT)


def _const_spec(shape):
    nd = len(shape)
    return pl.BlockSpec(shape, lambda *_: (0,) * nd, pipeline_mode=pl.Buffered(1))


def _rms(x, g):
    return x * lax.rsqrt(jnp.mean(x * x, axis=-1, keepdims=True) + NORM_EPS) * g


def _sigmoid(x):
    return 1.0 / (1.0 + jnp.exp(-x))


def _softplus(x):
    return jnp.maximum(x, 0.0) + jnp.log1p(jnp.exp(-jnp.abs(x)))


def _dot(a, b):
    return jnp.dot(a, b, preferred_element_type=f32)


def _dot_hi(a, b):
    return jnp.dot(a, b, preferred_element_type=f32, precision=HIGHEST)


def _dot_nt(a, b):
    return lax.dot_general(a, b, (((1,), (1,)), ((), ())), preferred_element_type=f32)


def _dot_tn(a, b):
    return lax.dot_general(a, b, (((0,), (0,)), ((), ())), preferred_element_type=f32)


def _iota(shape, dim):
    return lax.broadcasted_iota(jnp.int32, shape, dim)


def _group_mean_matrix(width, group):
    r = _iota((width, width), 0) // group
    c = _iota((width, width), 1) // group
    return jnp.where(r == c, 1.0 / group, 0.0).astype(f32)


def _inproj_kernel(x_ref, g_ref, w_ref, mla_ref, sg_ref, gla_ref, dn_ref):
    h = _rms(x_ref[...], g_ref[...]).astype(bf16)
    lo = 0
    for ref, width in ((mla_ref, W_MLA), (sg_ref, W_SG), (gla_ref, W_GLA), (dn_ref, W_DN)):
        ref[...] = _dot(h, w_ref[:, lo:lo + width])
        lo += width


def _inproj(x2, g, w_small):
    t = x2.shape[0]
    tm = TOKEN_TILE
    outs = tuple(jax.ShapeDtypeStruct((t, w), f32) for w in (W_MLA, W_SG, W_GLA, W_DN))
    return pl.pallas_call(
        _inproj_kernel,
        out_shape=outs,
        grid=(t // tm,),
        in_specs=[pl.BlockSpec((tm, D_MODEL), lambda i: (i, 0)),
                  _const_spec((1, D_MODEL)),
                  _const_spec((D_MODEL, W_SMALL))],
        out_specs=tuple(pl.BlockSpec((tm, w), lambda i: (i, 0)) for w in (W_MLA, W_SG, W_GLA, W_DN)),
        compiler_params=_cparams(1),
        name="inproj",
    )(x2, g, w_small)


def _mla_pre_kernel(z_ref, pos_ref, invf_ref, nq_ref, nkv_ref, wuq_ref, wuk_ref, wuv_ref,
                    q_ref, k_ref, v_ref):
    z = z_ref[...]
    tm = z.shape[0]
    cq = z[:, :MLA_Q_RANK]
    ckv = z[:, MLA_Q_RANK:MLA_Q_RANK + MLA_KV_RANK]
    kpe = z[:, MLA_Q_RANK + MLA_KV_RANK:]

    ang = pos_ref[...].astype(f32) * invf_ref[...]
    cos = jnp.cos(ang)
    sin = jnp.sin(ang)
    lane = _iota((tm, MLA_HEAD_PAD), 1)
    half = MLA_ROPE // 2
    in_x1 = (lane >= MLA_NOPE) & (lane < MLA_NOPE + half)
    in_x2 = (lane >= MLA_NOPE + half) & (lane < MLA_NOPE + MLA_ROPE)
    c_tab = jnp.where(in_x1 | in_x2, cos, 1.0)
    s1_tab = jnp.where(in_x1, -sin, 0.0)
    s2_tab = jnp.where(in_x2, sin, 0.0)

    def rope(t):
        return (t * c_tab + pltpu.roll(t, MLA_HEAD_PAD - half, 1) * s1_tab
                + pltpu.roll(t, half, 1) * s2_tab)

    scale = (MLA_NOPE + MLA_ROPE) ** -0.5
    q = _dot(_rms(cq, nq_ref[...]).astype(bf16), wuq_ref[...])
    kvn = _rms(ckv, nkv_ref[...]).astype(bf16)
    kn = _dot(kvn, wuk_ref[...])
    v_ref[...] = _dot(kvn, wuv_ref[...]).astype(bf16)
    kpe_rot = rope(kpe)
    for h in range(MLA_HEADS):
        sl = slice(h * MLA_HEAD_PAD, (h + 1) * MLA_HEAD_PAD)
        q_ref[:, sl] = (rope(q[:, sl]) * scale).astype(bf16)
        k_ref[:, sl] = (kn[:, sl] + kpe_rot).astype(bf16)


def _mla_pre(z_mla, pos2, invf, nq, nkv, wuq, wuk, wuv):
    t = z_mla.shape[0]
    tm = TOKEN_TILE
    hp = MLA_HEADS * MLA_HEAD_PAD
    hv = MLA_HEADS * MLA_V
    return pl.pallas_call(
        _mla_pre_kernel,
        out_shape=(jax.ShapeDtypeStruct((t, hp), bf16),
                   jax.ShapeDtypeStruct((t, hp), bf16),
                   jax.ShapeDtypeStruct((t, hv), bf16)),
        grid=(t // tm,),
        in_specs=[pl.BlockSpec((tm, W_MLA), lambda i: (i, 0)),
                  pl.BlockSpec((tm, 1), lambda i: (i, 0)),
                  _const_spec((1, MLA_HEAD_PAD)),
                  _const_spec((1, MLA_Q_RANK)),
                  _const_spec((1, MLA_KV_RANK)),
                  _const_spec((MLA_Q_RANK, hp)),
                  _const_spec((MLA_KV_RANK, hp)),
                  _const_spec((MLA_KV_RANK, hv))],
        out_specs=(pl.BlockSpec((tm, hp), lambda i: (i, 0)),
                   pl.BlockSpec((tm, hp), lambda i: (i, 0)),
                   pl.BlockSpec((tm, hv), lambda i: (i, 0))),
        compiler_params=_cparams(1),
        name="mla_pre",
    )(z_mla, pos2, invf, nq, nkv, wuq, wuk, wuv)


def _mla_attn_kernel(q_ref, k_ref, v_ref, o_ref):
    i = pl.program_id(1)
    tq = q_ref.shape[0]
    row = _iota((tq, tq), 0)
    col = _iota((tq, tq), 1)

    def step(j, carry, masked):
        start = pl.multiple_of(j * tq, tq)
        new = []
        for h in range(MLA_HEADS):
            qsl = slice(h * MLA_HEAD_PAD, (h + 1) * MLA_HEAD_PAD)
            vsl = slice(h * MLA_V, (h + 1) * MLA_V)
            m, l, acc = carry[h]
            s = _dot_nt(q_ref[:, qsl], k_ref[pl.ds(start, tq), qsl])
            if masked:
                s = jnp.where(row >= col, s, -jnp.inf)
            m_new = jnp.maximum(m, jnp.max(s, axis=-1, keepdims=True))
            alpha = jnp.exp(m - m_new)
            p = jnp.exp(s - m_new)
            l = alpha * l + jnp.sum(p, axis=-1, keepdims=True)
            acc = alpha * acc + _dot(p.astype(bf16), v_ref[pl.ds(start, tq), vsl])
            new.append((m_new, l, acc))
        return tuple(new)

    init = tuple((jnp.full((tq, 1), -jnp.inf, f32), jnp.zeros((tq, 1), f32),
                  jnp.zeros((tq, MLA_V), f32)) for _ in range(MLA_HEADS))
    carry = lax.fori_loop(0, i, functools.partial(step, masked=False), init)
    carry = step(i, carry, True)
    o_ref[...] = jnp.concatenate([acc / l for _, l, acc in carry], axis=-1).astype(bf16)


def _mla_attn(q, k, v, batch, seq):
    tq = ATTN_TILE
    nq = seq // tq
    hp = MLA_HEADS * MLA_HEAD_PAD
    hv = MLA_HEADS * MLA_V
    return pl.pallas_call(
        _mla_attn_kernel,
        out_shape=jax.ShapeDtypeStruct((batch * seq, hv), bf16),
        grid=(batch, nq),
        in_specs=[pl.BlockSpec((tq, hp), lambda b, i: (b * nq + i, 0)),
                  pl.BlockSpec((seq, hp), lambda b, i: (b, 0)),
                  pl.BlockSpec((seq, hv), lambda b, i: (b, 0))],
        out_specs=pl.BlockSpec((tq, hv), lambda b, i: (b * nq + i, 0)),
        compiler_params=_cparams(2),
        name="mla_attn",
    )(q, k, v)


def _sg_kernel(z_ref, lng_ref, lnb_ref, ws_ref, bst_ref, o_ref):
    z = z_ref[...]
    tm = z.shape[0]
    uv = 0.5 * z * (1.0 + jnp.tanh(np.sqrt(2.0 / np.pi) * (z + 0.044715 * (z * z * z))))
    u = uv[:, :SG_WIDTH]
    v = uv[:, SG_WIDTH:]
    mu = jnp.mean(v, axis=-1, keepdims=True)
    vc = v - mu
    vn = vc * lax.rsqrt(jnp.mean(vc * vc, axis=-1, keepdims=True) + NORM_EPS)
    vn = (vn * lng_ref[...] + lnb_ref[...]).astype(bf16)

    gw = SG_WIDTH // SG_GROUPS
    lane_group = _iota((SG_CHUNK, SG_WIDTH), 1) // gw
    tril = _iota((SG_CHUNK, SG_CHUNK), 0) >= _iota((SG_CHUNK, SG_CHUNK), 1)
    bias = jnp.zeros((SG_CHUNK, SG_WIDTH), f32)
    wm = []
    for g in range(SG_GROUPS):
        wm.append(jnp.where(tril, ws_ref[g], 0.0).astype(bf16))
        bias = jnp.where(lane_group == g,
                         jnp.broadcast_to(bst_ref[:, g:g + 1], (SG_CHUNK, SG_WIDTH)), bias)
    for c in range(tm // SG_CHUNK):
        rows = slice(c * SG_CHUNK, (c + 1) * SG_CHUNK)
        vch = vn[rows]
        mixed = bias
        for g in range(SG_GROUPS):
            mixed = mixed + jnp.where(lane_group == g, _dot(wm[g], vch), 0.0)
        o_ref[rows, :] = (u[rows] * mixed).astype(bf16)


def _sg(z_sg, lng, lnb, ws, bst):
    t = z_sg.shape[0]
    tm = TOKEN_TILE
    return pl.pallas_call(
        _sg_kernel,
        out_shape=jax.ShapeDtypeStruct((t, SG_WIDTH), bf16),
        grid=(t // tm,),
        in_specs=[pl.BlockSpec((tm, W_SG), lambda i: (i, 0)),
                  _const_spec((1, SG_WIDTH)),
                  _const_spec((1, SG_WIDTH)),
                  _const_spec((SG_GROUPS, SG_CHUNK, SG_CHUNK)),
                  _const_spec((SG_CHUNK, SG_GROUPS))],
        out_specs=pl.BlockSpec((tm, SG_WIDTH), lambda i: (i, 0)),
        compiler_params=_cparams(1),
        name="spatial_gating",
    )(z_sg, lng, lnb, ws, bst)


GLA_QK = GLA_HEADS * GLA_DK
GLA_V = GLA_HEADS * GLA_DV
GLA_SUB = 16


def _gla_kernel(z_ref, wg_ref, bg_ref, no_ref, o_ref, st_ref, la_ref, cum_ref, kc_ref, vc_ref):
    tc = z_ref.shape[0]
    c_len = GLA_CHUNK

    @pl.when(pl.program_id(1) == 0)
    def _():
        st_ref[...] = jnp.zeros_like(st_ref)

    zg = _dot(z_ref[:, 768:896].astype(bf16), wg_ref[...]) + bg_ref[...]
    log_sig = jnp.minimum(zg, 0.0) - jnp.log1p(jnp.exp(-jnp.abs(zg)))
    la_ref[...] = log_sig * (1.0 / GLA_GATE_TEMP)

    tri = (_iota((c_len, c_len), 0) >= _iota((c_len, c_len), 1)).astype(f32)
    expand = (_iota((GLA_QK, GLA_V), 0) // GLA_DK == _iota((GLA_QK, GLA_V), 1) // GLA_DV).astype(bf16)
    st_mask = (_iota((GLA_V, GLA_QK), 0) // GLA_DV == _iota((GLA_V, GLA_QK), 1) // GLA_DK).astype(f32)
    mean_mat = _group_mean_matrix(GLA_V, GLA_DV)
    sub = GLA_SUB
    row = _iota((sub, GLA_QK), 0)

    def chunk(c, carry):
        base = pl.multiple_of(c * c_len, c_len)
        rows = pl.ds(base, c_len)
        cum = _dot_hi(tri, la_ref[rows, :])
        cum_ref[...] = cum
        q = z_ref[rows, 0:128] * (GLA_DK ** -0.5)
        k = z_ref[rows, 128:256]
        v = z_ref[rows, 256:512]
        kc_ref[...] = k
        vc_ref[...] = v
        v_t = v.T.astype(bf16)
        st = st_ref[...]
        o_inter = _dot_nt((q * jnp.exp(cum)).astype(bf16), st.astype(bf16))
        last = cum[c_len - 1:c_len, :]
        kdec = k * jnp.exp(last - cum)
        st_ref[...] = jnp.exp(last) * st + _dot(v_t, kdec.astype(bf16)) * st_mask

        blocks = []
        for sb in range(c_len // sub):
            r0 = sb * sub
            q_i = q[r0:r0 + sub]
            cum_i = cum[r0:r0 + sub]
            ts = []
            for jj in range(sub):
                kj = kc_ref[r0 + jj:r0 + jj + 1, :]
                cj = cum_ref[r0 + jj:r0 + jj + 1, :]
                dec = jnp.exp(jnp.where(row >= jj, cum_i - cj, -1e30))
                ts.append((q_i * kj * dec).astype(bf16))
            r = _dot(jnp.concatenate(ts, axis=0), expand)
            o_sb = jnp.zeros((sub, GLA_V), f32)
            for jj in range(sub):
                o_sb = o_sb + r[jj * sub:(jj + 1) * sub] * vc_ref[r0 + jj:r0 + jj + 1, :]
            if sb > 0:
                c_prev = cum[r0 - 1:r0, :]
                kx = k[0:r0] * jnp.exp(c_prev - cum[0:r0])
                part = _dot(v_t[:, 0:r0], kx.astype(bf16)) * st_mask
                qx = q_i * jnp.exp(cum_i - c_prev)
                o_sb = o_sb + _dot_nt(qx.astype(bf16), part.astype(bf16))
            blocks.append(o_sb)
        o = o_inter + jnp.concatenate(blocks, axis=0)
        ms = _dot_hi(o * o, mean_mat)
        og = z_ref[rows, 512:768]
        o_ref[rows, :] = (o * lax.rsqrt(ms + NORM_EPS) * no_ref[...] * (og * _sigmoid(og))).astype(bf16)
        return carry

    lax.fori_loop(0, tc // c_len, chunk, 0)


def _gla(z_gla, wg, bg, no, batch, seq):
    tc = SCAN_TILE
    nt = seq // tc
    return pl.pallas_call(
        _gla_kernel,
        out_shape=jax.ShapeDtypeStruct((batch * seq, GLA_V), bf16),
        grid=(batch, nt),
        in_specs=[pl.BlockSpec((tc, W_GLA), lambda b, i: (b * nt + i, 0)),
                  _const_spec((128, GLA_QK)),
                  _const_spec((1, GLA_QK)),
                  _const_spec((1, GLA_V))],
        out_specs=pl.BlockSpec((tc, GLA_V), lambda b, i: (b * nt + i, 0)),
        scratch_shapes=[pltpu.VMEM((GLA_V, GLA_QK), f32),
                        pltpu.VMEM((tc, GLA_QK), f32),
                        pltpu.VMEM((GLA_CHUNK, GLA_QK), f32),
                        pltpu.VMEM((GLA_CHUNK, GLA_QK), f32),
                        pltpu.VMEM((GLA_CHUNK, GLA_V), f32)],
        compiler_params=_cparams(2),
        name="gla",
    )(z_gla, wg, bg, no)


DN_W = DN_HEADS * DN_DK
DN_CONV_PAD = 8


def _dot_3x(a, b):
    a_hi = a.astype(bf16)
    b_hi = b.astype(bf16)
    a_lo = (a - a_hi.astype(f32)).astype(bf16)
    b_lo = (b - b_hi.astype(f32)).astype(bf16)
    return _dot(a_hi, b_hi) + (_dot(a_hi, b_lo) + _dot(a_lo, b_hi))


def _dn_pre_kernel(z_ref, ab_ref, cw_ref, alog_ref, dtb_ref,
                   a_ref, attn_ref, bv_ref, bk_ref, qd_ref, kd_ref, egl_ref,
                   xbuf_ref, q_s, k_s, v_s, beta_s, g_s):
    tc = z_ref.shape[0]
    c_len = DN_CHUNK
    qkv_w = 3 * DN_W

    @pl.when(pl.program_id(1) == 0)
    def _():
        xbuf_ref[0:DN_CONV_PAD, :] = jnp.zeros((DN_CONV_PAD, qkv_w), f32)

    xbuf_ref[DN_CONV_PAD:DN_CONV_PAD + tc, :] = z_ref[...]
    y = jnp.zeros((tc, qkv_w), f32)
    for kk in range(DN_CONV):
        y = y + cw_ref[kk:kk + 1, :] * xbuf_ref[pl.ds(DN_CONV_PAD - (DN_CONV - 1) + kk, tc), :]
    xbuf_ref[0:DN_CONV_PAD, :] = xbuf_ref[tc:tc + DN_CONV_PAD, :]
    y = y * _sigmoid(y)

    sum_mat = _group_mean_matrix(DN_W, DN_DK) * float(DN_DK)
    q = y[:, 0:DN_W]
    k = y[:, DN_W:2 * DN_W]
    q_s[...] = q * lax.rsqrt(_dot_hi(q * q, sum_mat) + NORM_EPS) * (DN_DK ** -0.5)
    k_s[...] = k * lax.rsqrt(_dot_hi(k * k, sum_mat) + NORM_EPS)
    v_s[...] = y[:, 2 * DN_W:3 * DN_W]

    ab = ab_ref[...]
    lane_head = _iota((tc, DN_W), 1) // DN_DK
    a_rep = jnp.zeros((tc, DN_W), f32)
    b_rep = jnp.zeros((tc, DN_W), f32)
    for h in range(DN_HEADS):
        a_rep = jnp.where(lane_head == h, jnp.broadcast_to(ab[:, h:h + 1], (tc, DN_W)), a_rep)
        b_rep = jnp.where(lane_head == h,
                          jnp.broadcast_to(ab[:, DN_HEADS + h:DN_HEADS + h + 1], (tc, DN_W)), b_rep)
    beta_s[...] = _sigmoid(b_rep)
    g_s[...] = -jnp.exp(alog_ref[...]) * _softplus(a_rep + dtb_ref[...])

    r64 = _iota((c_len, c_len), 0)
    c64 = _iota((c_len, c_len), 1)
    tri = (r64 >= c64).astype(f32)
    ones = jnp.ones((c_len, c_len), f32)
    row = _iota((c_len, DN_W), 0)
    colj = _iota((c_len, DN_W), 1) % c_len
    eye4 = (row == colj).astype(f32)
    incl = row >= colj
    strict = row > colj

    def chunk(c, carry):
        base = pl.multiple_of(c * c_len, c_len)
        rows = pl.ds(base, c_len)
        gam = _dot_hi(tri, g_s[rows, :])
        gam_row = _dot_hi(ones, eye4 * gam)
        dec_incl = jnp.exp(jnp.where(incl, gam - gam_row, -1e30))
        dec_strict = jnp.where(strict, dec_incl, 0.0)
        eg = jnp.exp(gam)
        g_last = gam[c_len - 1:c_len, :]
        q = q_s[rows, :]
        k = k_s[rows, :]
        beta = beta_s[rows, :]
        bv_ref[rows, :] = beta * v_s[rows, :]
        bk_ref[rows, :] = beta * eg * k
        qd_ref[rows, :] = (q * eg).astype(bf16)
        kd_t = (k * jnp.exp(g_last - gam)).T.astype(bf16)
        egl_ref[c] = jnp.broadcast_to(jnp.exp(g_last), (8, DN_W))
        qb = q.astype(bf16)
        kb = k.astype(bf16)
        for h in range(DN_HEADS):
            sl = slice(h * DN_DK, (h + 1) * DN_DK)
            kd_ref[c * DN_HEADS + h] = kd_t[sl, :]
            a_ref[c * DN_HEADS + h] = beta[:, sl] * _dot_nt(kb[:, sl], kb[:, sl]) * dec_strict[:, sl]
            attn_ref[c * DN_HEADS + h] = (_dot_nt(qb[:, sl], kb[:, sl]) * dec_incl[:, sl]).astype(bf16)
        return carry

    lax.fori_loop(0, tc // c_len, chunk, 0)


def _dn_pre(z_dn, cw, alog, dtb, batch, seq):
    tc = SCAN_TILE
    nt = seq // tc
    t = batch * seq
    npb = (tc // DN_CHUNK) * DN_HEADS
    n_prob = (t // DN_CHUNK) * DN_HEADS
    scr = lambda: pltpu.VMEM((tc, DN_W), f32)
    tok = lambda: pl.BlockSpec((tc, DN_W), lambda b, i: (b * nt + i, 0))
    mat = lambda: pl.BlockSpec((npb, DN_CHUNK, DN_CHUNK), lambda b, i: (b * nt + i, 0, 0))
    return pl.pallas_call(
        _dn_pre_kernel,
        out_shape=(jax.ShapeDtypeStruct((n_prob, DN_CHUNK, DN_CHUNK), f32),
                   jax.ShapeDtypeStruct((n_prob, DN_CHUNK, DN_CHUNK), bf16),
                   jax.ShapeDtypeStruct((t, DN_W), f32),
                   jax.ShapeDtypeStruct((t, DN_W), f32),
                   jax.ShapeDtypeStruct((t, DN_W), bf16),
                   jax.ShapeDtypeStruct((n_prob, DN_DK, DN_CHUNK), bf16),
                   jax.ShapeDtypeStruct((t // DN_CHUNK, 8, DN_W), f32)),
        grid=(batch, nt),
        in_specs=[pl.BlockSpec((tc, 3 * DN_W), lambda b, i: (b * nt + i, 0)),
                  pl.BlockSpec((tc, 128), lambda b, i: (b * nt + i, 4 * DN_W // 128)),
                  _const_spec((DN_CONV, 3 * DN_W)),
                  _const_spec((1, DN_W)),
                  _const_spec((1, DN_W))],
        out_specs=(mat(), mat(), tok(), tok(), tok(), mat(),
                   pl.BlockSpec((tc // DN_CHUNK, 8, DN_W), lambda b, i: (b * nt + i, 0, 0))),
        scratch_shapes=[pltpu.VMEM((tc + 2 * DN_CONV_PAD, 3 * DN_W), f32),
                        scr(), scr(), scr(), scr(), scr()],
        compiler_params=_cparams(2),
        name="dn_pre",
    )(z_dn, z_dn, cw, alog, dtb)


def _dn_solve_kernel(a_ref, t_ref):
    c_len = DN_CHUNK
    blk = DN_INV_BLOCK
    vshape = a_ref.shape[1:]

    def row_i(i, carry):
        for mb in range(c_len // blk):
            m0 = mb * blk

            def col_j(j, acc):
                a_ij = a_ref[i * c_len + j]
                return tuple(acc[m] - a_ij * t_ref[j * c_len + m0 + m] for m in range(blk))

            acc = lax.fori_loop(m0, i, col_j, tuple(jnp.zeros(vshape, f32) for _ in range(blk)))
            for m in range(blk):
                t_ref[i * c_len + m0 + m] = acc[m] + jnp.where(i == m0 + m, 1.0, 0.0)
        return carry

    lax.fori_loop(0, c_len, row_i, 0)


def _dn_solve(a_t):
    n_ent, n_grp, lanes = a_t.shape
    g = min(n_grp, 8)
    spec = lambda: pl.BlockSpec((n_ent, g, lanes), lambda i: (0, i, 0), pipeline_mode=pl.Buffered(1))
    return pl.pallas_call(
        _dn_solve_kernel,
        out_shape=jax.ShapeDtypeStruct(a_t.shape, f32),
        grid=(n_grp // g,),
        in_specs=[spec()],
        out_specs=spec(),
        compiler_params=_cparams(1),
        name="dn_solve",
    )(a_t)


def _dn_scan_kernel(t_ref, attn_ref, bv_ref, bk_ref, qd_ref, kd_ref, egl_ref, zg_ref, no_ref,
                    o_ref, st_ref):
    tc = bv_ref.shape[0]
    c_len = DN_CHUNK

    @pl.when(pl.program_id(1) == 0)
    def _():
        st_ref[...] = jnp.zeros_like(st_ref)

    mean_mat = _group_mean_matrix(DN_W, DN_DV)

    def chunk(c, carry):
        base = pl.multiple_of(c * c_len, c_len)
        rows = pl.ds(base, c_len)
        bv = bv_ref[rows, :]
        bk = bk_ref[rows, :]
        qd = qd_ref[rows, :]
        eg_last = egl_ref[c][0:1, :]
        outs = []
        for h in range(DN_HEADS):
            sl = slice(h * DN_DK, (h + 1) * DN_DK)
            rhs = jnp.concatenate([bv[:, sl], bk[:, sl]], axis=-1)
            sol = _dot_3x(t_ref[c * DN_HEADS + h], rhs)
            w_c = sol[:, :DN_DV]
            x_c = sol[:, DN_DV:]
            st = st_ref[h]
            stb = st.astype(bf16)
            ub = (w_c - _dot(x_c.astype(bf16), stb)).astype(bf16)
            outs.append(_dot(qd[:, sl], stb) + _dot(attn_ref[c * DN_HEADS + h], ub))
            st_ref[h] = eg_last[:, sl] * st + _dot(kd_ref[c * DN_HEADS + h], ub)
        o = jnp.concatenate(outs, axis=-1)
        ms = _dot_hi(o * o, mean_mat)
        zg = zg_ref[rows, :]
        o_ref[rows, :] = (o * lax.rsqrt(ms + NORM_EPS) * no_ref[...] * (zg * _sigmoid(zg))).astype(bf16)
        return carry

    lax.fori_loop(0, tc // c_len, chunk, 0)


def _dn_scan(t_all, attn, bv, bk, qd, kd, egl, z_dn, no, batch, seq):
    tc = SCAN_TILE
    nt = seq // tc
    npb = (tc // DN_CHUNK) * DN_HEADS
    tok = lambda: pl.BlockSpec((tc, DN_W), lambda b, i: (b * nt + i, 0))
    mat = lambda: pl.BlockSpec((npb, DN_CHUNK, DN_CHUNK), lambda b, i: (b * nt + i, 0, 0))
    return pl.pallas_call(
        _dn_scan_kernel,
        out_shape=jax.ShapeDtypeStruct((batch * seq, DN_W), bf16),
        grid=(batch, nt),
        in_specs=[mat(), mat(), tok(), tok(), tok(), mat(),
                  pl.BlockSpec((tc // DN_CHUNK, 8, DN_W), lambda b, i: (b * nt + i, 0, 0)),
                  pl.BlockSpec((tc, DN_W), lambda b, i: (b * nt + i, 3)),
                  _const_spec((1, DN_W))],
        out_specs=tok(),
        scratch_shapes=[pltpu.VMEM((DN_HEADS, DN_DK, DN_DV), f32)],
        compiler_params=_cparams(2),
        name="dn_scan",
    )(t_all, attn, bv, bk, qd, kd, egl, z_dn, no)


def _dn(z_dn, cw, alog, dtb, no, batch, seq):
    a_all, attn, bv, bk, qd, kd, egl = _dn_pre(z_dn, cw, alog, dtb, batch, seq)
    n_prob = a_all.shape[0]
    n_ent = DN_CHUNK * DN_CHUNK
    a_t = a_all.reshape(n_prob, n_ent).T.reshape(n_ent, n_prob // 128, 128)
    t_all = _dn_solve(a_t).reshape(n_ent, n_prob).T.reshape(n_prob, DN_CHUNK, DN_CHUNK)
    return _dn_scan(t_all, attn, bv, bk, qd, kd, egl, z_dn, no, batch, seq)


def _merge_kernel(x_ref, g_ref, oa_ref, ob_ref, oc_ref, od_ref, wg_ref, wb_ref, wo_ref, out_ref):
    x = x_ref[...]
    h = _rms(x, g_ref[...]).astype(bf16)
    merged = jnp.zeros(x.shape, f32)
    for n, o_ref in enumerate((oa_ref, ob_ref, oc_ref, od_ref)):
        gate = _sigmoid(_dot(h, wg_ref[:, n * D_MODEL:(n + 1) * D_MODEL]))
        merged = merged + gate * _dot(o_ref[...], wb_ref[n])
    out_ref[...] = x + _dot(merged.astype(bf16), wo_ref[...])


def _merge(x2, g, oa, ob, oc, od, wg, wb, wo):
    t = x2.shape[0]
    tm = TOKEN_TILE
    tok = lambda w: pl.BlockSpec((tm, w), lambda i: (i, 0))
    return pl.pallas_call(
        _merge_kernel,
        out_shape=jax.ShapeDtypeStruct((t, D_MODEL), f32),
        grid=(t // tm,),
        in_specs=[tok(D_MODEL), _const_spec((1, D_MODEL)),
                  tok(BRANCH_WIDTH), tok(BRANCH_WIDTH), tok(BRANCH_WIDTH), tok(BRANCH_WIDTH),
                  _const_spec((D_MODEL, N_BRANCH * D_MODEL)),
                  _const_spec((N_BRANCH, BRANCH_WIDTH, D_MODEL)),
                  _const_spec((D_MODEL, D_MODEL))],
        out_specs=tok(D_MODEL),
        compiler_params=_cparams(1),
        name="merge",
    )(x2, g, oa, ob, oc, od, wg, wb, wo)


def _memkv_kernel(mem_ref, g_ref, wk_ref, wv_ref, k_ref, v_ref):
    mn = _rms(mem_ref[...], g_ref[...]).astype(bf16)
    k_ref[...] = _dot(mn, wk_ref[...]).astype(bf16)
    v_ref[...] = _dot(mn, wv_ref[...]).astype(bf16)


def _memkv(mem2, g, wk, wv, mem_len):
    t = mem2.shape[0]
    blk = lambda: pl.BlockSpec((mem_len, D_MODEL), lambda i: (i, 0))
    return pl.pallas_call(
        _memkv_kernel,
        out_shape=(jax.ShapeDtypeStruct((t, D_MODEL), bf16), jax.ShapeDtypeStruct((t, D_MODEL), bf16)),
        grid=(t // mem_len,),
        in_specs=[blk(), _const_spec((1, D_MODEL)),
                  _const_spec((D_MODEL, D_MODEL)), _const_spec((D_MODEL, D_MODEL))],
        out_specs=(blk(), blk()),
        compiler_params=_cparams(1),
        name="mem_kv",
    )(mem2, g, wk, wv)


def _xattn_kernel(x_ref, g_ref, k_ref, v_ref, wq_ref, wo_ref, out_ref):
    x = x_ref[...]
    h = _rms(x, g_ref[...]).astype(bf16)
    q = (_dot(h, wq_ref[...]) * (XATTN_DIM ** -0.5)).astype(bf16)
    outs = []
    for hh in range(XATTN_HEADS):
        sl = slice(hh * XATTN_DIM, (hh + 1) * XATTN_DIM)
        s = _dot_nt(q[:, sl], k_ref[:, sl])
        p = jnp.exp(s - jnp.max(s, axis=-1, keepdims=True))
        p = p / jnp.sum(p, axis=-1, keepdims=True)
        outs.append(_dot(p.astype(bf16), v_ref[:, sl]).astype(bf16))
    o = jnp.concatenate(outs, axis=-1)
    out_ref[...] = x + _dot(o, wo_ref[...])


def _xattn(x2, g, kmem, vmem, wq, wo, batch, seq, mem_len):
    tm = TOKEN_TILE
    nt = seq // tm
    tok = pl.BlockSpec((tm, D_MODEL), lambda b, i: (b * nt + i, 0))
    memb = lambda: pl.BlockSpec((mem_len, D_MODEL), lambda b, i: (b, 0))
    return pl.pallas_call(
        _xattn_kernel,
        out_shape=jax.ShapeDtypeStruct((batch * seq, D_MODEL), f32),
        grid=(batch, nt),
        in_specs=[tok, _const_spec((1, D_MODEL)), memb(), memb(),
                  _const_spec((D_MODEL, D_MODEL)), _const_spec((D_MODEL, D_MODEL))],
        out_specs=tok,
        compiler_params=_cparams(2),
        name="xattn",
    )(x2, g, kmem, vmem, wq, wo)


def _mlp_kernel(x_ref, g_ref, wu_ref, wd_ref, gf_ref, out_ref, *, final):
    x = x_ref[...]
    h = _rms(x, g_ref[...]).astype(bf16)
    acc = x
    for c in range(MLP_HIDDEN // MLP_CHUNK):
        sl = slice(c * MLP_CHUNK, (c + 1) * MLP_CHUNK)
        a = jnp.maximum(_dot(h, wu_ref[:, sl]), 0.0)
        acc = acc + _dot((a * a).astype(bf16), wd_ref[sl, :])
    if final:
        acc = _rms(acc, gf_ref[...])
    out_ref[...] = acc


def _mlp(x2, g, wu, wd, gf, final):
    t = x2.shape[0]
    tm = TOKEN_TILE
    tok = pl.BlockSpec((tm, D_MODEL), lambda i: (i, 0))
    return pl.pallas_call(
        functools.partial(_mlp_kernel, final=final),
        out_shape=jax.ShapeDtypeStruct((t, D_MODEL), f32),
        grid=(t // tm,),
        in_specs=[tok, _const_spec((1, D_MODEL)),
                  _const_spec((D_MODEL, MLP_HIDDEN)), _const_spec((MLP_HIDDEN, D_MODEL)),
                  _const_spec((1, D_MODEL))],
        out_specs=tok,
        compiler_params=_cparams(1),
        name="mlp",
    )(x2, g, wu, wd, gf)


def _pad_cols(w, width):
    return jnp.pad(w, ((0, 0), (0, width - w.shape[1])))


def _prep_w_in(w_in):
    col = lambda i: w_in[:, IN_OFFS[i]:IN_OFFS[i + 1]]
    zeros = lambda n: jnp.zeros((D_MODEL, n), w_in.dtype)
    (cq, ckv, kpe, sg_uv, gq, gk, gv, g_lr, g_og, dq, dk, dv, da, db, dz) = (col(i) for i in range(15))
    small = jnp.concatenate([
        cq, ckv, zeros(MLA_NOPE), kpe, zeros(MLA_HEAD_PAD - MLA_NOPE - MLA_ROPE),
        sg_uv,
        gq, gk, gv, g_og, _pad_cols(g_lr, 128),
        dq, dk, dv, dz, _pad_cols(jnp.concatenate([da, db], axis=1), 128),
    ], axis=1)
    assert small.shape[1] == W_SMALL
    return small.astype(bf16), col(15).astype(bf16)


def _rope_lane_table():
    inv_freq = ROPE_THETA ** (-jnp.arange(0, MLA_ROPE, 2, dtype=f32) / MLA_ROPE)
    half = MLA_ROPE // 2
    tab = jnp.zeros((MLA_HEAD_PAD,), f32)
    tab = tab.at[MLA_NOPE:MLA_NOPE + half].set(inv_freq)
    tab = tab.at[MLA_NOPE + half:MLA_NOPE + MLA_ROPE].set(inv_freq)
    return tab[None, :]


def kernel(x, mem, positions, norm_mix, w_in, mla_norm_q, mla_norm_kv, mla_w_uq, mla_w_ukv, sg_ln_g, sg_ln_b, sg_w, sg_b, gla_w_gate, gla_b_gate, gla_norm, dn_conv, dn_a_log, dn_dt_bias, dn_norm, w_branch, w_out, norm_xattn, norm_mem, xattn_wq, xattn_wk, xattn_wv, xattn_wo, norm_mlp, w_up, w_down, norm_final):
    batch, seq, _ = x.shape
    mem_len = mem.shape[1]
    depth = w_in.shape[0]
    assert seq % TOKEN_TILE == 0 and seq % SCAN_TILE == 0 and seq % ATTN_TILE == 0
    x2 = x.reshape(batch * seq, D_MODEL)
    mem2 = mem.reshape(batch * mem_len, D_MODEL)
    pos2 = positions.reshape(batch * seq, 1)
    invf = _rope_lane_table()
    row = lambda v: v[None, :]

    for l in range(depth):
        w_small, w_gates = _prep_w_in(w_in[l])
        z_mla, z_sg, z_gla, z_dn = _inproj(x2, row(norm_mix[l]), w_small)

        wuq = jnp.pad(mla_w_uq[l].reshape(MLA_Q_RANK, MLA_HEADS, MLA_NOPE + MLA_ROPE),
                      ((0, 0), (0, 0), (0, MLA_HEAD_PAD - MLA_NOPE - MLA_ROPE)))
        wuq = wuq.reshape(MLA_Q_RANK, MLA_HEADS * MLA_HEAD_PAD).astype(bf16)
        wukv = mla_w_ukv[l].reshape(MLA_KV_RANK, MLA_HEADS, MLA_NOPE + MLA_V)
        wuk = jnp.pad(wukv[:, :, :MLA_NOPE], ((0, 0), (0, 0), (0, MLA_HEAD_PAD - MLA_NOPE)))
        wuk = wuk.reshape(MLA_KV_RANK, MLA_HEADS * MLA_HEAD_PAD).astype(bf16)
        wuv = wukv[:, :, MLA_NOPE:].reshape(MLA_KV_RANK, MLA_HEADS * MLA_V).astype(bf16)
        q_a, k_a, v_a = _mla_pre(z_mla, pos2, invf, row(mla_norm_q[l]), row(mla_norm_kv[l]), wuq, wuk, wuv)
        o_a = _mla_attn(q_a, k_a, v_a, batch, seq)

        o_b = _sg(z_sg, row(sg_ln_g[l]), row(sg_ln_b[l]), sg_w[l], sg_b[l].T)

        wg = jnp.pad(gla_w_gate[l], ((0, 128 - GLA_GATE_RANK), (0, 0))).astype(bf16)
        o_c = _gla(z_gla, wg, row(gla_b_gate[l]), row(jnp.tile(gla_norm[l], GLA_HEADS)), batch, seq)

        o_d = _dn(z_dn, dn_conv[l], row(jnp.repeat(dn_a_log[l], DN_DK)), row(jnp.repeat(dn_dt_bias[l], DN_DK)),
                  row(jnp.tile(dn_norm[l], DN_HEADS)), batch, seq)

        x2 = _merge(x2, row(norm_mix[l]), o_a, o_b, o_c, o_d, w_gates,
                    w_branch[l].astype(bf16), w_out[l].astype(bf16))

        k_m, v_m = _memkv(mem2, row(norm_mem[l]), xattn_wk[l].astype(bf16), xattn_wv[l].astype(bf16), mem_len)
        x2 = _xattn(x2, row(norm_xattn[l]), k_m, v_m, xattn_wq[l].astype(bf16), xattn_wo[l].astype(bf16),
                    batch, seq, mem_len)
        x2 = _mlp(x2, row(norm_mlp[l]), w_up[l].astype(bf16), w_down[l].astype(bf16), row(norm_final),
                  final=(l == depth - 1))
    return x2.reshape(batch, seq, D_MODEL)
```

```python
import functools

import jax
import jax.numpy as jnp
import numpy as np
from jax import lax
from jax.experimental import pallas as pl
from jax.experimental.pallas import tpu as pltpu

f32 = jnp.float32
bf16 = jnp.bfloat16
HIGHEST = lax.Precision.HIGHEST

D_MODEL = 1024
NORM_EPS = 1e-6
N_BRANCH = 4
BRANCH_WIDTH = 256

MLA_HEADS = 4
MLA_Q_RANK = 256
MLA_KV_RANK = 128
MLA_NOPE = 64
MLA_ROPE = 32
MLA_V = 64
ROPE_THETA = 10000.0
MLA_HEAD_PAD = 128
MLA_VT_ROWS = 80

SG_GROUPS = 4
SG_WIDTH = 256
SG_CHUNK = 128

GLA_HEADS = 4
GLA_DK = 32
GLA_DV = 64
GLA_GATE_RANK = 16
GLA_GATE_TEMP = 16.0
GLA_CHUNK = 64

DN_HEADS = 4
DN_DK = 64
DN_DV = 64
DN_CONV = 4
DN_CHUNK = 64
DN_INV_BLOCK = 16

XATTN_HEADS = 4
XATTN_DIM = D_MODEL // XATTN_HEADS
MLP_HIDDEN = 4 * D_MODEL

IN_SIZES = (
    MLA_Q_RANK, MLA_KV_RANK, MLA_ROPE,
    2 * SG_WIDTH,
    GLA_HEADS * GLA_DK, GLA_HEADS * GLA_DK, GLA_HEADS * GLA_DV,
    GLA_GATE_RANK, GLA_HEADS * GLA_DV,
    DN_HEADS * DN_DK, DN_HEADS * DN_DK, DN_HEADS * DN_DV,
    DN_HEADS, DN_HEADS, DN_HEADS * DN_DV,
    N_BRANCH * D_MODEL,
)
IN_OFFS = tuple(int(v) for v in np.cumsum((0,) + IN_SIZES))

W_MLA = 512
W_SG = 512
W_GLA = 896
W_DN = 1152
W_SMALL = W_MLA + W_SG + W_GLA + W_DN

TOKEN_TILE = 512
ATTN_TILE = 256
ATTN_KEY_TILE = 128
SCAN_TILE = 512
MLP_CHUNK = 1024
VMEM_LIMIT = 56 * 1024 * 1024


def _cparams(n_axes):
    return pltpu.CompilerParams(dimension_semantics=("arbitrary",) * n_axes,
                                vmem_limit_bytes=VMEM_LIMIT)


def _const_spec(shape):
    nd = len(shape)
    return pl.BlockSpec(shape, lambda *_: (0,) * nd, pipeline_mode=pl.Buffered(1))


def _rms(x, g):
    return x * lax.rsqrt(jnp.mean(x * x, axis=-1, keepdims=True) + NORM_EPS) * g


def _sigmoid(x):
    return 1.0 / (1.0 + jnp.exp(-x))


def _softplus(x):
    return jnp.maximum(x, 0.0) + jnp.log1p(jnp.exp(-jnp.abs(x)))


def _dot(a, b):
    return jnp.dot(a, b, preferred_element_type=f32)


def _split_bf16(x, terms):
    parts = []
    for _ in range(terms):
        p = x.astype(bf16)
        parts.append(p)
        x = x - p.astype(f32)
    return parts


def _dot_sel(sel, x, terms=3):
    sel = sel.astype(bf16)
    parts = [_dot(sel, p) for p in _split_bf16(x, terms)]
    return functools.reduce(lambda a, b: a + b, reversed(parts))


def _group_stat(x, mat, terms=2):
    mat = mat.astype(bf16)
    parts = [_dot(p, mat) for p in _split_bf16(x, terms)]
    return functools.reduce(lambda a, b: a + b, reversed(parts))


def _dot_nt(a, b):
    return lax.dot_general(a, b, (((1,), (1,)), ((), ())), preferred_element_type=f32)


def _dot_tn(a, b):
    return lax.dot_general(a, b, (((0,), (0,)), ((), ())), preferred_element_type=f32)


def _iota(shape, dim):
    return lax.broadcasted_iota(jnp.int32, shape, dim)


def _group_mean_matrix(width, group):
    r = _iota((width, width), 0) // group
    c = _iota((width, width), 1) // group
    return jnp.where(r == c, 1.0 / group, 0.0).astype(f32)


def _inproj_kernel(x_ref, g_ref, w_ref, mla_ref, sg_ref, gla_ref, dn_ref):
    h = _rms(x_ref[...], g_ref[...]).astype(bf16)
    lo = 0
    for ref, width in ((mla_ref, W_MLA), (sg_ref, W_SG), (gla_ref, W_GLA), (dn_ref, W_DN)):
        ref[...] = _dot(h, w_ref[:, lo:lo + width])
        lo += width


def _inproj(x2, g, w_small):
    t = x2.shape[0]
    tm = TOKEN_TILE
    outs = tuple(jax.ShapeDtypeStruct((t, w), f32) for w in (W_MLA, W_SG, W_GLA, W_DN))
    return pl.pallas_call(
        _inproj_kernel,
        out_shape=outs,
        grid=(t // tm,),
        in_specs=[pl.BlockSpec((tm, D_MODEL), lambda i: (i, 0)),
                  _const_spec((1, D_MODEL)),
                  _const_spec((D_MODEL, W_SMALL))],
        out_specs=tuple(pl.BlockSpec((tm, w), lambda i: (i, 0)) for w in (W_MLA, W_SG, W_GLA, W_DN)),
        compiler_params=_cparams(1),
        name="inproj",
    )(x2, g, w_small)


def _mla_pre_kernel(z_ref, pos_ref, invf_ref, nq_ref, nkv_ref, wuq_ref, wuk_ref, wuv_ref,
                    qt_ref, k_ref, vt_ref):
    z = z_ref[...]
    tm = z.shape[0]
    cq = z[:, :MLA_Q_RANK]
    ckv = z[:, MLA_Q_RANK:MLA_Q_RANK + MLA_KV_RANK]
    kpe = z[:, MLA_Q_RANK + MLA_KV_RANK:]

    ang = pos_ref[...].astype(f32) * invf_ref[...]
    cos = jnp.cos(ang)
    sin = jnp.sin(ang)
    lane = _iota((tm, MLA_HEAD_PAD), 1)
    half = MLA_ROPE // 2
    in_x1 = (lane >= MLA_NOPE) & (lane < MLA_NOPE + half)
    in_x2 = (lane >= MLA_NOPE + half) & (lane < MLA_NOPE + MLA_ROPE)
    c_tab = jnp.where(in_x1 | in_x2, cos, 1.0)
    s1_tab = jnp.where(in_x1, -sin, 0.0)
    s2_tab = jnp.where(in_x2, sin, 0.0)

    def rope(t):
        return (t * c_tab + pltpu.roll(t, MLA_HEAD_PAD - half, 1) * s1_tab
                + pltpu.roll(t, half, 1) * s2_tab)

    scale = (MLA_NOPE + MLA_ROPE) ** -0.5
    q = _dot(_rms(cq, nq_ref[...]).astype(bf16), wuq_ref[...])
    kvn = _rms(ckv, nkv_ref[...]).astype(bf16)
    kn = _dot(kvn, wuk_ref[...])
    v_t = _dot(kvn, wuv_ref[...]).T
    kpe_rot = rope(kpe)
    q_rot = []
    for h in range(MLA_HEADS):
        sl = slice(h * MLA_HEAD_PAD, (h + 1) * MLA_HEAD_PAD)
        q_rot.append(rope(q[:, sl]) * scale)
        k_ref[:, sl] = (kn[:, sl] + kpe_rot).astype(bf16)
    qt_ref[...] = jnp.concatenate(q_rot, axis=-1).T.astype(bf16)
    pad = jnp.concatenate([jnp.ones((1, tm), f32), jnp.zeros((MLA_VT_ROWS - MLA_V - 1, tm), f32)], axis=0)
    rows = []
    for h in range(MLA_HEADS):
        rows += [v_t[h * MLA_V:(h + 1) * MLA_V], pad]
    vt_ref[...] = jnp.concatenate(rows, axis=0).astype(bf16)


def _mla_pre(z_mla, pos2, invf, nq, nkv, wuq, wuk, wuv, batch, seq):
    t = z_mla.shape[0]
    tm = TOKEN_TILE
    nt = seq // tm
    hp = MLA_HEADS * MLA_HEAD_PAD
    hv = MLA_HEADS * MLA_V
    hvt = MLA_HEADS * MLA_VT_ROWS
    return pl.pallas_call(
        _mla_pre_kernel,
        out_shape=(jax.ShapeDtypeStruct((batch, hp, seq), bf16),
                   jax.ShapeDtypeStruct((t, hp), bf16),
                   jax.ShapeDtypeStruct((batch, hvt, seq), bf16)),
        grid=(t // tm,),
        in_specs=[pl.BlockSpec((tm, W_MLA), lambda i: (i, 0)),
                  pl.BlockSpec((tm, 1), lambda i: (i, 0)),
                  _const_spec((1, MLA_HEAD_PAD)),
                  _const_spec((1, MLA_Q_RANK)),
                  _const_spec((1, MLA_KV_RANK)),
                  _const_spec((MLA_Q_RANK, hp)),
                  _const_spec((MLA_KV_RANK, hp)),
                  _const_spec((MLA_KV_RANK, hv))],
        out_specs=(pl.BlockSpec((None, hp, tm), lambda i: (i // nt, 0, i % nt)),
                   pl.BlockSpec((tm, hp), lambda i: (i, 0)),
                   pl.BlockSpec((None, hvt, tm), lambda i: (i // nt, 0, i % nt))),
        compiler_params=_cparams(1),
        name="mla_pre",
    )(z_mla, pos2, invf, nq, nkv, wuq, wuk, wuv)


def _mla_attn_kernel(qt_ref, k_ref, vt_ref, o_ref, sa_ref, sb_ref):
    i = pl.program_id(1)
    tq = qt_ref.shape[1]
    tk = sa_ref.shape[1]
    per = tq // tk
    key = _iota((tk, tq), 0)
    qry = _iota((tk, tq), 1)

    def issue_scores(b, buf):
        start = pl.multiple_of(b * tk, tk)
        cms = []
        for h in range(MLA_HEADS):
            hsl = slice(h * MLA_HEAD_PAD, (h + 1) * MLA_HEAD_PAD)
            s = _dot(k_ref[pl.ds(start, tk), hsl], qt_ref[hsl, :])
            buf[h] = s
            cms.append(jnp.max(s, axis=0, keepdims=True))
        return tuple(cms)

    def consume(b, buf, cms, state, diag_offset=None):
        start = pl.multiple_of(b * tk, tk)
        probs = []
        for h in range(MLA_HEADS):
            s = buf[h]
            if diag_offset is None:
                cm = cms[h]
            else:
                s = jnp.where(key + diag_offset <= qry, s, -jnp.inf)
                cm = jnp.max(s, axis=0, keepdims=True)
            m_new = jnp.maximum(state[h][0], cm)
            probs.append((m_new, jnp.exp(s - m_new).astype(bf16)))
        new = []
        for h in range(MLA_HEADS):
            vsl = slice(h * MLA_VT_ROWS, (h + 1) * MLA_VT_ROWS)
            m, acc = state[h]
            m_new, p = probs[h]
            acc = jnp.exp(m - m_new) * acc + _dot(vt_ref[vsl, pl.ds(start, tk)], p)
            new.append((m_new, acc))
        return tuple(new)

    def body(t, carry):
        state, cm_a = carry
        cm_b = issue_scores(per * t + 1, sb_ref)
        state = consume(per * t, sa_ref, cm_a, state)
        cm_a = issue_scores(per * t + 2, sa_ref)
        state = consume(per * t + 1, sb_ref, cm_b, state)
        return state, cm_a

    assert per == 2
    init = tuple((jnp.full((1, tq), -jnp.inf, f32), jnp.zeros((MLA_VT_ROWS, tq), f32))
                 for _ in range(MLA_HEADS))
    state, _ = lax.fori_loop(0, i, body, (init, issue_scores(0, sa_ref)))
    issue_scores(per * i + 1, sb_ref)
    state = consume(per * i, sa_ref, None, state, diag_offset=0)
    state = consume(per * i + 1, sb_ref, None, state, diag_offset=tk)
    o_t = jnp.concatenate([acc[:MLA_V] / acc[MLA_V:MLA_V + 1] for _, acc in state], axis=0)
    o_ref[...] = o_t.T.astype(bf16)


def _mla_attn(q_t, k, v_t, batch, seq):
    tq = ATTN_TILE
    nq = seq // tq
    hp = MLA_HEADS * MLA_HEAD_PAD
    hv = MLA_HEADS * MLA_V
    hvt = MLA_HEADS * MLA_VT_ROWS
    return pl.pallas_call(
        _mla_attn_kernel,
        out_shape=jax.ShapeDtypeStruct((batch * seq, hv), bf16),
        grid=(batch, nq),
        in_specs=[pl.BlockSpec((None, hp, tq), lambda b, i: (b, 0, i)),
                  pl.BlockSpec((seq, hp), lambda b, i: (b, 0)),
                  pl.BlockSpec((None, hvt, seq), lambda b, i: (b, 0, 0))],
        out_specs=pl.BlockSpec((tq, hv), lambda b, i: (b * nq + i, 0)),
        scratch_shapes=[pltpu.VMEM((MLA_HEADS, ATTN_KEY_TILE, tq), f32),
                        pltpu.VMEM((MLA_HEADS, ATTN_KEY_TILE, tq), f32)],
        compiler_params=_cparams(2),
        name="mla_attn",
    )(q_t, k, v_t)


def _sg_kernel(z_ref, lng_ref, lnb_ref, ws_ref, bst_ref, o_ref):
    z = z_ref[...]
    tm = z.shape[0]
    uv = 0.5 * z * (1.0 + jnp.tanh(np.sqrt(2.0 / np.pi) * (z + 0.044715 * (z * z * z))))
    u = uv[:, :SG_WIDTH]
    v = uv[:, SG_WIDTH:]
    mu = jnp.mean(v, axis=-1, keepdims=True)
    vc = v - mu
    vn = vc * lax.rsqrt(jnp.mean(vc * vc, axis=-1, keepdims=True) + NORM_EPS)
    vn = (vn * lng_ref[...] + lnb_ref[...]).astype(bf16)

    gw = SG_WIDTH // SG_GROUPS
    lane_group = _iota((SG_CHUNK, SG_WIDTH), 1) // gw
    tril = _iota((SG_CHUNK, SG_CHUNK), 0) >= _iota((SG_CHUNK, SG_CHUNK), 1)
    bias = jnp.zeros((SG_CHUNK, SG_WIDTH), f32)
    wm = []
    for g in range(SG_GROUPS):
        wm.append(jnp.where(tril, ws_ref[g], 0.0).astype(bf16))
        bias = jnp.where(lane_group == g,
                         jnp.broadcast_to(bst_ref[:, g:g + 1], (SG_CHUNK, SG_WIDTH)), bias)
    for c in range(tm // SG_CHUNK):
        rows = slice(c * SG_CHUNK, (c + 1) * SG_CHUNK)
        vch = vn[rows]
        mixed = bias
        for g in range(SG_GROUPS):
            mixed = mixed + jnp.where(lane_group == g, _dot(wm[g], vch), 0.0)
        o_ref[rows, :] = (u[rows] * mixed).astype(bf16)


def _sg(z_sg, lng, lnb, ws, bst):
    t = z_sg.shape[0]
    tm = TOKEN_TILE
    return pl.pallas_call(
        _sg_kernel,
        out_shape=jax.ShapeDtypeStruct((t, SG_WIDTH), bf16),
        grid=(t // tm,),
        in_specs=[pl.BlockSpec((tm, W_SG), lambda i: (i, 0)),
                  _const_spec((1, SG_WIDTH)),
                  _const_spec((1, SG_WIDTH)),
                  _const_spec((SG_GROUPS, SG_CHUNK, SG_CHUNK)),
                  _const_spec((SG_CHUNK, SG_GROUPS))],
        out_specs=pl.BlockSpec((tm, SG_WIDTH), lambda i: (i, 0)),
        compiler_params=_cparams(1),
        name="spatial_gating",
    )(z_sg, lng, lnb, ws, bst)


GLA_QK = GLA_HEADS * GLA_DK
GLA_V = GLA_HEADS * GLA_DV
GLA_SUB = 16


def _gla_kernel(z_ref, wg_ref, bg_ref, no_ref, o_ref,
                st_ref, cum_ref, kc_ref, vc_ref, qd_ref, upd_ref, dl_ref, stc_ref, oraw_ref):
    tc = z_ref.shape[0]
    c_len = GLA_CHUNK
    n_chunks = tc // c_len

    @pl.when(pl.program_id(1) == 0)
    def _():
        st_ref[...] = jnp.zeros_like(st_ref)

    zg = _dot(z_ref[:, 768:896].astype(bf16), wg_ref[...]) + bg_ref[...]
    log_a = (jnp.minimum(zg, 0.0) - jnp.log1p(jnp.exp(-jnp.abs(zg)))) * (1.0 / GLA_GATE_TEMP)

    tri = (_iota((c_len, c_len), 0) >= _iota((c_len, c_len), 1)).astype(f32)
    for c in range(n_chunks):
        cum_ref[c * c_len:(c + 1) * c_len, :] = _dot_sel(tri, log_a[c * c_len:(c + 1) * c_len])

    expand = (_iota((GLA_QK, GLA_V), 0) // GLA_DK == _iota((GLA_QK, GLA_V), 1) // GLA_DV).astype(bf16)
    st_mask = (_iota((GLA_V, GLA_QK), 0) // GLA_DV == _iota((GLA_V, GLA_QK), 1) // GLA_DK).astype(f32)
    sub = GLA_SUB
    row = _iota((sub, GLA_QK), 0)

    def intra(c, carry):
        base = pl.multiple_of(c * c_len, c_len)
        rows = pl.ds(base, c_len)
        cum = cum_ref[rows, :]
        q = z_ref[rows, 0:128] * (GLA_DK ** -0.5)
        k = z_ref[rows, 128:256]
        v = z_ref[rows, 256:512]
        kc_ref[...] = k
        vc_ref[...] = v
        v_t = v.T.astype(bf16)
        qd_ref[rows, :] = (q * jnp.exp(cum)).astype(bf16)
        last = cum[c_len - 1:c_len, :]
        dl_ref[c] = jnp.broadcast_to(jnp.exp(last), (8, GLA_QK))
        upd_ref[c] = _dot(v_t, (k * jnp.exp(last - cum)).astype(bf16)) * st_mask

        blocks = []
        for sb in range(c_len // sub):
            r0 = sb * sub
            q_i = q[r0:r0 + sub]
            cum_i = cum[r0:r0 + sub]
            ts = []
            for jj in range(sub):
                kj = kc_ref[r0 + jj:r0 + jj + 1, :]
                cj = cum_ref[pl.ds(base + r0 + jj, 1), :]
                dec = jnp.exp(jnp.where(row >= jj, cum_i - cj, -1e30))
                ts.append((q_i * kj * dec).astype(bf16))
            r = _dot(jnp.concatenate(ts, axis=0), expand)
            o_sb = jnp.zeros((sub, GLA_V), f32)
            for jj in range(sub):
                o_sb = o_sb + r[jj * sub:(jj + 1) * sub] * vc_ref[r0 + jj:r0 + jj + 1, :]
            if sb > 0:
                c_prev = cum[r0 - 1:r0, :]
                kx = k[0:r0] * jnp.exp(c_prev - cum[0:r0])
                part = _dot(v_t[:, 0:r0], kx.astype(bf16)) * st_mask
                qx = q_i * jnp.exp(cum_i - c_prev)
                o_sb = o_sb + _dot_nt(qx.astype(bf16), part.astype(bf16))
            blocks.append(o_sb)
        oraw_ref[rows, :] = jnp.concatenate(blocks, axis=0)
        return carry

    lax.fori_loop(0, n_chunks, intra, 0)

    def recur(c, st):
        stc_ref[c] = st.astype(bf16)
        return dl_ref[c][0:1, :] * st + upd_ref[c]

    st_ref[...] = lax.fori_loop(0, n_chunks, recur, st_ref[...])
    for c in range(n_chunks):
        rows = slice(c * c_len, (c + 1) * c_len)
        oraw_ref[rows, :] += _dot_nt(qd_ref[rows, :], stc_ref[c])

    o = oraw_ref[...]
    ms = _group_stat(o * o, _group_mean_matrix(GLA_V, GLA_DV))
    og = z_ref[:, 512:768]
    o_ref[...] = (o * lax.rsqrt(ms + NORM_EPS) * no_ref[...] * (og * _sigmoid(og))).astype(bf16)


def _gla(z_gla, wg, bg, no, batch, seq):
    tc = SCAN_TILE
    nt = seq // tc
    n_chunks = tc // GLA_CHUNK
    return pl.pallas_call(
        _gla_kernel,
        out_shape=jax.ShapeDtypeStruct((batch * seq, GLA_V), bf16),
        grid=(batch, nt),
        in_specs=[pl.BlockSpec((tc, W_GLA), lambda b, i: (b * nt + i, 0)),
                  _const_spec((128, GLA_QK)),
                  _const_spec((1, GLA_QK)),
                  _const_spec((1, GLA_V))],
        out_specs=pl.BlockSpec((tc, GLA_V), lambda b, i: (b * nt + i, 0)),
        scratch_shapes=[pltpu.VMEM((GLA_V, GLA_QK), f32),
                        pltpu.VMEM((tc, GLA_QK), f32),
                        pltpu.VMEM((GLA_CHUNK, GLA_QK), f32),
                        pltpu.VMEM((GLA_CHUNK, GLA_V), f32),
                        pltpu.VMEM((tc, GLA_QK), bf16),
                        pltpu.VMEM((n_chunks, GLA_V, GLA_QK), f32),
                        pltpu.VMEM((n_chunks, 8, GLA_QK), f32),
                        pltpu.VMEM((n_chunks, GLA_V, GLA_QK), bf16),
                        pltpu.VMEM((tc, GLA_V), f32)],
        compiler_params=_cparams(2),
        name="gla",
    )(z_gla, wg, bg, no)


DN_W = DN_HEADS * DN_DK
DN_CONV_PAD = 8


def _dot_3x(a, b):
    a_hi = a.astype(bf16)
    b_hi = b.astype(bf16)
    a_lo = (a - a_hi.astype(f32)).astype(bf16)
    b_lo = (b - b_hi.astype(f32)).astype(bf16)
    return _dot(a_hi, b_hi) + (_dot(a_hi, b_lo) + _dot(a_lo, b_hi))


def _dn_pre_kernel(z_ref, ab_ref, cw_ref, alog_ref, dtb_ref,
                   a_ref, attn_ref, bv_ref, bk_ref, qd_ref, kd_ref, egl_ref,
                   xbuf_ref, q_s, k_s, v_s, beta_s, g_s, gam_s, gamr_s):
    tc = z_ref.shape[0]
    c_len = DN_CHUNK
    qkv_w = 3 * DN_W

    @pl.when(pl.program_id(1) == 0)
    def _():
        xbuf_ref[0:DN_CONV_PAD, :] = jnp.zeros((DN_CONV_PAD, qkv_w), f32)

    xbuf_ref[DN_CONV_PAD:DN_CONV_PAD + tc, :] = z_ref[...]
    y = jnp.zeros((tc, qkv_w), f32)
    for kk in range(DN_CONV):
        y = y + cw_ref[kk:kk + 1, :] * xbuf_ref[pl.ds(DN_CONV_PAD - (DN_CONV - 1) + kk, tc), :]
    xbuf_ref[0:DN_CONV_PAD, :] = xbuf_ref[tc:tc + DN_CONV_PAD, :]
    y = y * _sigmoid(y)

    sum_mat = _group_mean_matrix(DN_W, DN_DK) * float(DN_DK)
    q = y[:, 0:DN_W]
    k = y[:, DN_W:2 * DN_W]
    q_s[...] = q * lax.rsqrt(_group_stat(q * q, sum_mat) + NORM_EPS) * (DN_DK ** -0.5)
    k_s[...] = k * lax.rsqrt(_group_stat(k * k, sum_mat) + NORM_EPS)
    v_s[...] = y[:, 2 * DN_W:3 * DN_W]

    ab = ab_ref[...]
    lane_head = _iota((tc, DN_W), 1) // DN_DK
    a_rep = jnp.zeros((tc, DN_W), f32)
    b_rep = jnp.zeros((tc, DN_W), f32)
    for h in range(DN_HEADS):
        a_rep = jnp.where(lane_head == h, jnp.broadcast_to(ab[:, h:h + 1], (tc, DN_W)), a_rep)
        b_rep = jnp.where(lane_head == h,
                          jnp.broadcast_to(ab[:, DN_HEADS + h:DN_HEADS + h + 1], (tc, DN_W)), b_rep)
    beta_s[...] = _sigmoid(b_rep)
    g_s[...] = -jnp.exp(alog_ref[...]) * _softplus(a_rep + dtb_ref[...])

    r64 = _iota((c_len, c_len), 0)
    c64 = _iota((c_len, c_len), 1)
    tri = (r64 >= c64).astype(f32)
    ones = jnp.ones((c_len, c_len), f32)
    row = _iota((c_len, DN_W), 0)
    colj = _iota((c_len, DN_W), 1) % c_len
    eye4 = (row == colj).astype(f32)
    incl = row >= colj
    strict = row > colj

    n_chunks = tc // c_len
    for c in range(n_chunks):
        rows = slice(c * c_len, (c + 1) * c_len)
        gam_s[rows, :] = _dot_sel(tri, g_s[rows, :])
    for c in range(n_chunks):
        rows = slice(c * c_len, (c + 1) * c_len)
        gamr_s[rows, :] = _dot_sel(ones, eye4 * gam_s[rows, :])

    def chunk(c, carry):
        base = pl.multiple_of(c * c_len, c_len)
        rows = pl.ds(base, c_len)
        gam = gam_s[rows, :]
        gam_row = gamr_s[rows, :]
        dec_incl = jnp.exp(jnp.where(incl, gam - gam_row, -1e30))
        dec_strict = jnp.where(strict, dec_incl, 0.0)
        eg = jnp.exp(gam)
        g_last = gam[c_len - 1:c_len, :]
        q = q_s[rows, :]
        k = k_s[rows, :]
        beta = beta_s[rows, :]
        bv_ref[rows, :] = beta * v_s[rows, :]
        bk_ref[rows, :] = beta * eg * k
        qd_ref[rows, :] = (q * eg).astype(bf16)
        kd_t = (k * jnp.exp(g_last - gam)).T.astype(bf16)
        egl_ref[c] = jnp.broadcast_to(jnp.exp(g_last), (8, DN_W))
        qb = q.astype(bf16)
        kb = k.astype(bf16)
        for h in range(DN_HEADS):
            sl = slice(h * DN_DK, (h + 1) * DN_DK)
            kd_ref[c * DN_HEADS + h] = kd_t[sl, :]
            a_ref[c * DN_HEADS + h] = beta[:, sl] * _dot_nt(kb[:, sl], kb[:, sl]) * dec_strict[:, sl]
            attn_ref[c * DN_HEADS + h] = (_dot_nt(qb[:, sl], kb[:, sl]) * dec_incl[:, sl]).astype(bf16)
        return carry

    lax.fori_loop(0, tc // c_len, chunk, 0)


def _dn_pre(z_dn, cw, alog, dtb, batch, seq):
    tc = SCAN_TILE
    nt = seq // tc
    t = batch * seq
    npb = (tc // DN_CHUNK) * DN_HEADS
    n_prob = (t // DN_CHUNK) * DN_HEADS
    scr = lambda: pltpu.VMEM((tc, DN_W), f32)
    tok = lambda: pl.BlockSpec((tc, DN_W), lambda b, i: (b * nt + i, 0))
    mat = lambda: pl.BlockSpec((npb, DN_CHUNK, DN_CHUNK), lambda b, i: (b * nt + i, 0, 0))
    return pl.pallas_call(
        _dn_pre_kernel,
        out_shape=(jax.ShapeDtypeStruct((n_prob, DN_CHUNK, DN_CHUNK), f32),
                   jax.ShapeDtypeStruct((n_prob, DN_CHUNK, DN_CHUNK), bf16),
                   jax.ShapeDtypeStruct((t, DN_W), f32),
                   jax.ShapeDtypeStruct((t, DN_W), f32),
                   jax.ShapeDtypeStruct((t, DN_W), bf16),
                   jax.ShapeDtypeStruct((n_prob, DN_DK, DN_CHUNK), bf16),
                   jax.ShapeDtypeStruct((t // DN_CHUNK, 8, DN_W), f32)),
        grid=(batch, nt),
        in_specs=[pl.BlockSpec((tc, 3 * DN_W), lambda b, i: (b * nt + i, 0)),
                  pl.BlockSpec((tc, 128), lambda b, i: (b * nt + i, 4 * DN_W // 128)),
                  _const_spec((DN_CONV, 3 * DN_W)),
                  _const_spec((1, DN_W)),
                  _const_spec((1, DN_W))],
        out_specs=(mat(), mat(), tok(), tok(), tok(), mat(),
                   pl.BlockSpec((tc // DN_CHUNK, 8, DN_W), lambda b, i: (b * nt + i, 0, 0))),
        scratch_shapes=[pltpu.VMEM((tc + 2 * DN_CONV_PAD, 3 * DN_W), f32),
                        scr(), scr(), scr(), scr(), scr(), scr(), scr()],
        compiler_params=_cparams(2),
        name="dn_pre",
    )(z_dn, z_dn, cw, alog, dtb)


def _dn_solve_kernel(a_ref, t_ref):
    c_len = DN_CHUNK
    blk = DN_INV_BLOCK
    vshape = a_ref.shape[1:]

    def row_i(i, carry):
        for mb in range(c_len // blk):
            m0 = mb * blk

            def col_j(j, acc):
                a_ij = a_ref[i * c_len + j]
                return tuple(acc[m] - a_ij * t_ref[j * c_len + m0 + m] for m in range(blk))

            acc = lax.fori_loop(m0, i, col_j, tuple(jnp.zeros(vshape, f32) for _ in range(blk)))
            for m in range(blk):
                t_ref[i * c_len + m0 + m] = acc[m] + jnp.where(i == m0 + m, 1.0, 0.0)
        return carry

    lax.fori_loop(0, c_len, row_i, 0)


def _dn_solve(a_t):
    n_ent, n_grp, lanes = a_t.shape
    g = min(n_grp, 8)
    spec = lambda: pl.BlockSpec((n_ent, g, lanes), lambda i: (0, i, 0), pipeline_mode=pl.Buffered(1))
    return pl.pallas_call(
        _dn_solve_kernel,
        out_shape=jax.ShapeDtypeStruct(a_t.shape, f32),
        grid=(n_grp // g,),
        in_specs=[spec()],
        out_specs=spec(),
        compiler_params=_cparams(1),
        name="dn_solve",
    )(a_t)


def _dn_scan_kernel(t_ref, attn_ref, bv_ref, bk_ref, qd_ref, kd_ref, egl_ref, zg_ref, no_ref,
                    o_ref, st_ref, sol_ref, oraw_ref):
    tc = bv_ref.shape[0]
    c_len = DN_CHUNK
    heads = range(DN_HEADS)
    hsl = [slice(h * DN_DK, (h + 1) * DN_DK) for h in heads]

    @pl.when(pl.program_id(1) == 0)
    def _():
        st_ref[...] = jnp.zeros_like(st_ref)

    def solve(c, carry):
        rows = pl.ds(pl.multiple_of(c * c_len, c_len), c_len)
        bv = bv_ref[rows, :]
        bk = bk_ref[rows, :]
        sols = [_dot_3x(t_ref[c * DN_HEADS + h], jnp.concatenate([bv[:, hsl[h]], bk[:, hsl[h]]], axis=-1))
                for h in heads]
        for h in heads:
            sol_ref[c * DN_HEADS + h] = sols[h]
        return carry

    lax.fori_loop(0, tc // c_len, solve, 0)

    def chunk(c, carry):
        rows = pl.ds(pl.multiple_of(c * c_len, c_len), c_len)
        qd = qd_ref[rows, :]
        eg_last = egl_ref[c][0:1, :]
        sols = [sol_ref[c * DN_HEADS + h] for h in heads]
        sts = [st_ref[h] for h in heads]
        xq = [_dot(jnp.concatenate([sols[h][:, DN_DV:].astype(bf16), qd[:, hsl[h]]], axis=0),
                   sts[h].astype(bf16)) for h in heads]
        ub = [(sols[h][:, :DN_DV] - xq[h][:c_len]).astype(bf16) for h in heads]
        au = [_dot(jnp.concatenate([attn_ref[c * DN_HEADS + h], kd_ref[c * DN_HEADS + h]], axis=0), ub[h])
              for h in heads]
        for h in heads:
            st_ref[h] = eg_last[:, hsl[h]] * sts[h] + au[h][c_len:]
        oraw_ref[rows, :] = jnp.concatenate([xq[h][c_len:] + au[h][:c_len] for h in heads], axis=-1)
        return carry

    lax.fori_loop(0, tc // c_len, chunk, 0)

    o = oraw_ref[...]
    ms = _group_stat(o * o, _group_mean_matrix(DN_W, DN_DV))
    zg = zg_ref[...]
    o_ref[...] = (o * lax.rsqrt(ms + NORM_EPS) * no_ref[...] * (zg * _sigmoid(zg))).astype(bf16)


def _dn_scan(t_all, attn, bv, bk, qd, kd, egl, z_dn, no, batch, seq):
    tc = SCAN_TILE
    nt = seq // tc
    npb = (tc // DN_CHUNK) * DN_HEADS
    tok = lambda: pl.BlockSpec((tc, DN_W), lambda b, i: (b * nt + i, 0))
    mat = lambda: pl.BlockSpec((npb, DN_CHUNK, DN_CHUNK), lambda b, i: (b * nt + i, 0, 0))
    return pl.pallas_call(
        _dn_scan_kernel,
        out_shape=jax.ShapeDtypeStruct((batch * seq, DN_W), bf16),
        grid=(batch, nt),
        in_specs=[mat(), mat(), tok(), tok(), tok(), mat(),
                  pl.BlockSpec((tc // DN_CHUNK, 8, DN_W), lambda b, i: (b * nt + i, 0, 0)),
                  pl.BlockSpec((tc, DN_W), lambda b, i: (b * nt + i, 3)),
                  _const_spec((1, DN_W))],
        out_specs=tok(),
        scratch_shapes=[pltpu.VMEM((DN_HEADS, DN_DK, DN_DV), f32),
                        pltpu.VMEM((npb, DN_CHUNK, 2 * DN_DV), f32),
                        pltpu.VMEM((tc, DN_W), f32)],
        compiler_params=_cparams(2),
        name="dn_scan",
    )(t_all, attn, bv, bk, qd, kd, egl, z_dn, no)


def _dn(z_dn, cw, alog, dtb, no, batch, seq):
    a_all, attn, bv, bk, qd, kd, egl = _dn_pre(z_dn, cw, alog, dtb, batch, seq)
    n_prob = a_all.shape[0]
    n_ent = DN_CHUNK * DN_CHUNK
    a_t = a_all.reshape(n_prob, n_ent).T.reshape(n_ent, n_prob // 128, 128)
    t_all = _dn_solve(a_t).reshape(n_ent, n_prob).T.reshape(n_prob, DN_CHUNK, DN_CHUNK)
    return _dn_scan(t_all, attn, bv, bk, qd, kd, egl, z_dn, no, batch, seq)


def _merge_kernel(x_ref, g_ref, oa_ref, ob_ref, oc_ref, od_ref, wg_ref, wb_ref, wo_ref, out_ref):
    x = x_ref[...]
    h = _rms(x, g_ref[...]).astype(bf16)
    merged = jnp.zeros(x.shape, f32)
    for n, o_ref in enumerate((oa_ref, ob_ref, oc_ref, od_ref)):
        gate = _sigmoid(_dot(h, wg_ref[:, n * D_MODEL:(n + 1) * D_MODEL]))
        merged = merged + gate * _dot(o_ref[...], wb_ref[n])
    out_ref[...] = x + _dot(merged.astype(bf16), wo_ref[...])


def _merge(x2, g, oa, ob, oc, od, wg, wb, wo):
    t = x2.shape[0]
    tm = TOKEN_TILE
    tok = lambda w: pl.BlockSpec((tm, w), lambda i: (i, 0))
    return pl.pallas_call(
        _merge_kernel,
        out_shape=jax.ShapeDtypeStruct((t, D_MODEL), f32),
        grid=(t // tm,),
        in_specs=[tok(D_MODEL), _const_spec((1, D_MODEL)),
                  tok(BRANCH_WIDTH), tok(BRANCH_WIDTH), tok(BRANCH_WIDTH), tok(BRANCH_WIDTH),
                  _const_spec((D_MODEL, N_BRANCH * D_MODEL)),
                  _const_spec((N_BRANCH, BRANCH_WIDTH, D_MODEL)),
                  _const_spec((D_MODEL, D_MODEL))],
        out_specs=tok(D_MODEL),
        compiler_params=_cparams(1),
        name="merge",
    )(x2, g, oa, ob, oc, od, wg, wb, wo)


def _memkv_kernel(mem_ref, g_ref, wk_ref, wv_ref, k_ref, v_ref):
    mn = _rms(mem_ref[...], g_ref[...]).astype(bf16)
    k_ref[...] = _dot(mn, wk_ref[...]).astype(bf16)
    v_ref[...] = _dot(mn, wv_ref[...]).astype(bf16)


def _memkv(mem2, g, wk, wv, mem_len):
    t = mem2.shape[0]
    blk = lambda: pl.BlockSpec((mem_len, D_MODEL), lambda i: (i, 0))
    return pl.pallas_call(
        _memkv_kernel,
        out_shape=(jax.ShapeDtypeStruct((t, D_MODEL), bf16), jax.ShapeDtypeStruct((t, D_MODEL), bf16)),
        grid=(t // mem_len,),
        in_specs=[blk(), _const_spec((1, D_MODEL)),
                  _const_spec((D_MODEL, D_MODEL)), _const_spec((D_MODEL, D_MODEL))],
        out_specs=(blk(), blk()),
        compiler_params=_cparams(1),
        name="mem_kv",
    )(mem2, g, wk, wv)


def _xattn_kernel(x_ref, g_ref, k_ref, v_ref, wq_ref, wo_ref, out_ref):
    x = x_ref[...]
    h = _rms(x, g_ref[...]).astype(bf16)
    q = (_dot(h, wq_ref[...]) * (XATTN_DIM ** -0.5)).astype(bf16)
    outs = []
    for hh in range(XATTN_HEADS):
        sl = slice(hh * XATTN_DIM, (hh + 1) * XATTN_DIM)
        s = _dot_nt(q[:, sl], k_ref[:, sl])
        p = jnp.exp(s - jnp.max(s, axis=-1, keepdims=True))
        p = p / jnp.sum(p, axis=-1, keepdims=True)
        outs.append(_dot(p.astype(bf16), v_ref[:, sl]).astype(bf16))
    o = jnp.concatenate(outs, axis=-1)
    out_ref[...] = x + _dot(o, wo_ref[...])


def _xattn(x2, g, kmem, vmem, wq, wo, batch, seq, mem_len):
    tm = TOKEN_TILE
    nt = seq // tm
    tok = pl.BlockSpec((tm, D_MODEL), lambda b, i: (b * nt + i, 0))
    memb = lambda: pl.BlockSpec((mem_len, D_MODEL), lambda b, i: (b, 0))
    return pl.pallas_call(
        _xattn_kernel,
        out_shape=jax.ShapeDtypeStruct((batch * seq, D_MODEL), f32),
        grid=(batch, nt),
        in_specs=[tok, _const_spec((1, D_MODEL)), memb(), memb(),
                  _const_spec((D_MODEL, D_MODEL)), _const_spec((D_MODEL, D_MODEL))],
        out_specs=tok,
        compiler_params=_cparams(2),
        name="xattn",
    )(x2, g, kmem, vmem, wq, wo)


def _mlp_kernel(x_ref, g_ref, wu_ref, wd_ref, gf_ref, out_ref, *, final):
    x = x_ref[...]
    h = _rms(x, g_ref[...]).astype(bf16)
    acc = x
    for c in range(MLP_HIDDEN // MLP_CHUNK):
        sl = slice(c * MLP_CHUNK, (c + 1) * MLP_CHUNK)
        a = jnp.maximum(_dot(h, wu_ref[:, sl]), 0.0)
        acc = acc + _dot((a * a).astype(bf16), wd_ref[sl, :])
    if final:
        acc = _rms(acc, gf_ref[...])
    out_ref[...] = acc


def _mlp(x2, g, wu, wd, gf, final):
    t = x2.shape[0]
    tm = TOKEN_TILE
    tok = pl.BlockSpec((tm, D_MODEL), lambda i: (i, 0))
    return pl.pallas_call(
        functools.partial(_mlp_kernel, final=final),
        out_shape=jax.ShapeDtypeStruct((t, D_MODEL), f32),
        grid=(t // tm,),
        in_specs=[tok, _const_spec((1, D_MODEL)),
                  _const_spec((D_MODEL, MLP_HIDDEN)), _const_spec((MLP_HIDDEN, D_MODEL)),
                  _const_spec((1, D_MODEL))],
        out_specs=tok,
        compiler_params=_cparams(1),
        name="mlp",
    )(x2, g, wu, wd, gf)


def _pad_cols(w, width):
    return jnp.pad(w, ((0, 0), (0, width - w.shape[1])))


def _prep_w_in(w_in):
    col = lambda i: w_in[:, IN_OFFS[i]:IN_OFFS[i + 1]]
    zeros = lambda n: jnp.zeros((D_MODEL, n), w_in.dtype)
    (cq, ckv, kpe, sg_uv, gq, gk, gv, g_lr, g_og, dq, dk, dv, da, db, dz) = (col(i) for i in range(15))
    small = jnp.concatenate([
        cq, ckv, zeros(MLA_NOPE), kpe, zeros(MLA_HEAD_PAD - MLA_NOPE - MLA_ROPE),
        sg_uv,
        gq, gk, gv, g_og, _pad_cols(g_lr, 128),
        dq, dk, dv, dz, _pad_cols(jnp.concatenate([da, db], axis=1), 128),
    ], axis=1)
    assert small.shape[1] == W_SMALL
    return small.astype(bf16), col(15).astype(bf16)


def _rope_lane_table():
    inv_freq = ROPE_THETA ** (-jnp.arange(0, MLA_ROPE, 2, dtype=f32) / MLA_ROPE)
    half = MLA_ROPE // 2
    tab = jnp.zeros((MLA_HEAD_PAD,), f32)
    tab = tab.at[MLA_NOPE:MLA_NOPE + half].set(inv_freq)
    tab = tab.at[MLA_NOPE + half:MLA_NOPE + MLA_ROPE].set(inv_freq)
    return tab[None, :]


def kernel(x, mem, positions, norm_mix, w_in, mla_norm_q, mla_norm_kv, mla_w_uq, mla_w_ukv, sg_ln_g, sg_ln_b, sg_w, sg_b, gla_w_gate, gla_b_gate, gla_norm, dn_conv, dn_a_log, dn_dt_bias, dn_norm, w_branch, w_out, norm_xattn, norm_mem, xattn_wq, xattn_wk, xattn_wv, xattn_wo, norm_mlp, w_up, w_down, norm_final):
    batch, seq, _ = x.shape
    mem_len = mem.shape[1]
    depth = w_in.shape[0]
    assert seq % TOKEN_TILE == 0 and seq % SCAN_TILE == 0 and seq % ATTN_TILE == 0
    x2 = x.reshape(batch * seq, D_MODEL)
    mem2 = mem.reshape(batch * mem_len, D_MODEL)
    pos2 = positions.reshape(batch * seq, 1)
    invf = _rope_lane_table()
    row = lambda v: v[None, :]

    for l in range(depth):
        w_small, w_gates = _prep_w_in(w_in[l])
        z_mla, z_sg, z_gla, z_dn = _inproj(x2, row(norm_mix[l]), w_small)

        wuq = jnp.pad(mla_w_uq[l].reshape(MLA_Q_RANK, MLA_HEADS, MLA_NOPE + MLA_ROPE),
                      ((0, 0), (0, 0), (0, MLA_HEAD_PAD - MLA_NOPE - MLA_ROPE)))
        wuq = wuq.reshape(MLA_Q_RANK, MLA_HEADS * MLA_HEAD_PAD).astype(bf16)
        wukv = mla_w_ukv[l].reshape(MLA_KV_RANK, MLA_HEADS, MLA_NOPE + MLA_V)
        wuk = jnp.pad(wukv[:, :, :MLA_NOPE], ((0, 0), (0, 0), (0, MLA_HEAD_PAD - MLA_NOPE)))
        wuk = wuk.reshape(MLA_KV_RANK, MLA_HEADS * MLA_HEAD_PAD).astype(bf16)
        wuv = wukv[:, :, MLA_NOPE:].reshape(MLA_KV_RANK, MLA_HEADS * MLA_V).astype(bf16)
        q_a, k_a, v_a = _mla_pre(z_mla, pos2, invf, row(mla_norm_q[l]), row(mla_norm_kv[l]), wuq, wuk, wuv,
                                 batch, seq)
        o_a = _mla_attn(q_a, k_a, v_a, batch, seq)

        o_b = _sg(z_sg, row(sg_ln_g[l]), row(sg_ln_b[l]), sg_w[l], sg_b[l].T)

        wg = jnp.pad(gla_w_gate[l], ((0, 128 - GLA_GATE_RANK), (0, 0))).astype(bf16)
        o_c = _gla(z_gla, wg, row(gla_b_gate[l]), row(jnp.tile(gla_norm[l], GLA_HEADS)), batch, seq)

        o_d = _dn(z_dn, dn_conv[l], row(jnp.repeat(dn_a_log[l], DN_DK)), row(jnp.repeat(dn_dt_bias[l], DN_DK)),
                  row(jnp.tile(dn_norm[l], DN_HEADS)), batch, seq)

        x2 = _merge(x2, row(norm_mix[l]), o_a, o_b, o_c, o_d, w_gates,
                    w_branch[l].astype(bf16), w_out[l].astype(bf16))

        k_m, v_m = _memkv(mem2, row(norm_mem[l]), xattn_wk[l].astype(bf16), xattn_wv[l].astype(bf16), mem_len)
        x2 = _xattn(x2, row(norm_xattn[l]), k_m, v_m, xattn_wq[l].astype(bf16), xattn_wo[l].astype(bf16),
                    batch, seq, mem_len)
        x2 = _mlp(x2, row(norm_mlp[l]), w_up[l].astype(bf16), w_down[l].astype(bf16), row(norm_final),
                  final=(l == depth - 1))
    return x2.reshape(batch, seq, D_MODEL)
```

```python
import functools

import jax
import jax.numpy as jnp
import numpy as np
from jax import lax
from jax.experimental import pallas as pl
from jax.experimental.pallas import tpu as pltpu

f32 = jnp.float32
bf16 = jnp.bfloat16
HIGHEST = lax.Precision.HIGHEST

D_MODEL = 1024
NORM_EPS = 1e-6
N_BRANCH = 4
BRANCH_WIDTH = 256

MLA_HEADS = 4
MLA_Q_RANK = 256
MLA_KV_RANK = 128
MLA_NOPE = 64
MLA_ROPE = 32
MLA_V = 64
ROPE_THETA = 10000.0
MLA_HEAD_PAD = 128
MLA_VT_ROWS = 80

SG_GROUPS = 4
SG_WIDTH = 256
SG_CHUNK = 128

GLA_HEADS = 4
GLA_DK = 32
GLA_DV = 64
GLA_GATE_RANK = 16
GLA_GATE_TEMP = 16.0
GLA_CHUNK = 64

DN_HEADS = 4
DN_DK = 64
DN_DV = 64
DN_CONV = 4
DN_CHUNK = 64
DN_INV_BLOCK = 16

XATTN_HEADS = 4
XATTN_DIM = D_MODEL // XATTN_HEADS
MLP_HIDDEN = 4 * D_MODEL

IN_SIZES = (
    MLA_Q_RANK, MLA_KV_RANK, MLA_ROPE,
    2 * SG_WIDTH,
    GLA_HEADS * GLA_DK, GLA_HEADS * GLA_DK, GLA_HEADS * GLA_DV,
    GLA_GATE_RANK, GLA_HEADS * GLA_DV,
    DN_HEADS * DN_DK, DN_HEADS * DN_DK, DN_HEADS * DN_DV,
    DN_HEADS, DN_HEADS, DN_HEADS * DN_DV,
    N_BRANCH * D_MODEL,
)
IN_OFFS = tuple(int(v) for v in np.cumsum((0,) + IN_SIZES))

W_MLA = 512
W_SG = 512
W_GLA = 896
W_DN = 1152
W_SMALL = W_MLA + W_SG + W_GLA + W_DN

TOKEN_TILE = 512
ATTN_TILE = 256
ATTN_KEY_TILE = 128
SCAN_TILE = 512
MLP_CHUNK = 1024
VMEM_LIMIT = 56 * 1024 * 1024


def _cparams(n_axes):
    return pltpu.CompilerParams(dimension_semantics=("arbitrary",) * n_axes,
                                vmem_limit_bytes=VMEM_LIMIT)


def _const_spec(shape):
    nd = len(shape)
    return pl.BlockSpec(shape, lambda *_: (0,) * nd, pipeline_mode=pl.Buffered(1))


def _rms(x, g):
    return x * lax.rsqrt(jnp.mean(x * x, axis=-1, keepdims=True) + NORM_EPS) * g


def _sigmoid(x):
    return 1.0 / (1.0 + jnp.exp(-x))


def _softplus(x):
    return jnp.maximum(x, 0.0) + jnp.log1p(jnp.exp(-jnp.abs(x)))


def _dot(a, b):
    return jnp.dot(a, b, preferred_element_type=f32)


def _split_bf16(x, terms):
    parts = []
    for _ in range(terms):
        p = x.astype(bf16)
        parts.append(p)
        x = x - p.astype(f32)
    return parts


def _dot_sel(sel, x, terms=3):
    sel = sel.astype(bf16)
    parts = [_dot(sel, p) for p in _split_bf16(x, terms)]
    return functools.reduce(lambda a, b: a + b, reversed(parts))


def _group_stat(x, mat, terms=2):
    mat = mat.astype(bf16)
    parts = [_dot(p, mat) for p in _split_bf16(x, terms)]
    return functools.reduce(lambda a, b: a + b, reversed(parts))


def _dot_nt(a, b):
    return lax.dot_general(a, b, (((1,), (1,)), ((), ())), preferred_element_type=f32)


def _dot_tn(a, b):
    return lax.dot_general(a, b, (((0,), (0,)), ((), ())), preferred_element_type=f32)


def _iota(shape, dim):
    return lax.broadcasted_iota(jnp.int32, shape, dim)


def _group_mean_matrix(width, group):
    r = _iota((width, width), 0) // group
    c = _iota((width, width), 1) // group
    return jnp.where(r == c, 1.0 / group, 0.0).astype(f32)


def _inproj_kernel(x_ref, g_ref, w_ref, mla_ref, sg_ref, gla_ref, dn_ref):
    h = _rms(x_ref[...], g_ref[...]).astype(bf16)
    lo = 0
    for ref, width in ((mla_ref, W_MLA), (sg_ref, W_SG), (gla_ref, W_GLA), (dn_ref, W_DN)):
        ref[...] = _dot(h, w_ref[:, lo:lo + width])
        lo += width


def _inproj(x2, g, w_small):
    t = x2.shape[0]
    tm = TOKEN_TILE
    outs = tuple(jax.ShapeDtypeStruct((t, w), f32) for w in (W_MLA, W_SG, W_GLA, W_DN))
    return pl.pallas_call(
        _inproj_kernel,
        out_shape=outs,
        grid=(t // tm,),
        in_specs=[pl.BlockSpec((tm, D_MODEL), lambda i: (i, 0)),
                  _const_spec((1, D_MODEL)),
                  _const_spec((D_MODEL, W_SMALL))],
        out_specs=tuple(pl.BlockSpec((tm, w), lambda i: (i, 0)) for w in (W_MLA, W_SG, W_GLA, W_DN)),
        compiler_params=_cparams(1),
        name="inproj",
    )(x2, g, w_small)


def _mla_pre_kernel(z_ref, pos_ref, invf_ref, nq_ref, nkv_ref, wuq_ref, wuk_ref, wuv_ref,
                    qt_ref, k_ref, vt_ref):
    z = z_ref[...]
    tm = z.shape[0]
    cq = z[:, :MLA_Q_RANK]
    ckv = z[:, MLA_Q_RANK:MLA_Q_RANK + MLA_KV_RANK]
    kpe = z[:, MLA_Q_RANK + MLA_KV_RANK:]

    ang = pos_ref[...].astype(f32) * invf_ref[...]
    cos = jnp.cos(ang)
    sin = jnp.sin(ang)
    lane = _iota((tm, MLA_HEAD_PAD), 1)
    half = MLA_ROPE // 2
    in_x1 = (lane >= MLA_NOPE) & (lane < MLA_NOPE + half)
    in_x2 = (lane >= MLA_NOPE + half) & (lane < MLA_NOPE + MLA_ROPE)
    c_tab = jnp.where(in_x1 | in_x2, cos, 1.0)
    s1_tab = jnp.where(in_x1, -sin, 0.0)
    s2_tab = jnp.where(in_x2, sin, 0.0)

    def rope(t):
        return (t * c_tab + pltpu.roll(t, MLA_HEAD_PAD - half, 1) * s1_tab
                + pltpu.roll(t, half, 1) * s2_tab)

    scale = (MLA_NOPE + MLA_ROPE) ** -0.5
    q = _dot(_rms(cq, nq_ref[...]).astype(bf16), wuq_ref[...])
    kvn = _rms(ckv, nkv_ref[...]).astype(bf16)
    kn = _dot(kvn, wuk_ref[...])
    v_t = _dot(kvn, wuv_ref[...]).T
    kpe_rot = rope(kpe)
    q_rot = []
    for h in range(MLA_HEADS):
        sl = slice(h * MLA_HEAD_PAD, (h + 1) * MLA_HEAD_PAD)
        q_rot.append(rope(q[:, sl]) * scale)
        k_ref[:, sl] = (kn[:, sl] + kpe_rot).astype(bf16)
    qt_ref[...] = jnp.concatenate(q_rot, axis=-1).T.astype(bf16)
    pad = jnp.concatenate([jnp.ones((1, tm), f32), jnp.zeros((MLA_VT_ROWS - MLA_V - 1, tm), f32)], axis=0)
    rows = []
    for h in range(MLA_HEADS):
        rows += [v_t[h * MLA_V:(h + 1) * MLA_V], pad]
    vt_ref[...] = jnp.concatenate(rows, axis=0).astype(bf16)


def _mla_pre(z_mla, pos2, invf, nq, nkv, wuq, wuk, wuv, batch, seq):
    t = z_mla.shape[0]
    tm = TOKEN_TILE
    nt = seq // tm
    hp = MLA_HEADS * MLA_HEAD_PAD
    hv = MLA_HEADS * MLA_V
    hvt = MLA_HEADS * MLA_VT_ROWS
    return pl.pallas_call(
        _mla_pre_kernel,
        out_shape=(jax.ShapeDtypeStruct((batch, hp, seq), bf16),
                   jax.ShapeDtypeStruct((t, hp), bf16),
                   jax.ShapeDtypeStruct((batch, hvt, seq), bf16)),
        grid=(t // tm,),
        in_specs=[pl.BlockSpec((tm, W_MLA), lambda i: (i, 0)),
                  pl.BlockSpec((tm, 1), lambda i: (i, 0)),
                  _const_spec((1, MLA_HEAD_PAD)),
                  _const_spec((1, MLA_Q_RANK)),
                  _const_spec((1, MLA_KV_RANK)),
                  _const_spec((MLA_Q_RANK, hp)),
                  _const_spec((MLA_KV_RANK, hp)),
                  _const_spec((MLA_KV_RANK, hv))],
        out_specs=(pl.BlockSpec((None, hp, tm), lambda i: (i // nt, 0, i % nt)),
                   pl.BlockSpec((tm, hp), lambda i: (i, 0)),
                   pl.BlockSpec((None, hvt, tm), lambda i: (i // nt, 0, i % nt))),
        compiler_params=_cparams(1),
        name="mla_pre",
    )(z_mla, pos2, invf, nq, nkv, wuq, wuk, wuv)


def _mla_attn_kernel(qt_ref, k_ref, vt_ref, o_ref, sa_ref, sb_ref):
    i = pl.program_id(1)
    tq = qt_ref.shape[1]
    tk = sa_ref.shape[1]
    per = tq // tk
    key = _iota((tk, tq), 0)
    qry = _iota((tk, tq), 1)

    def issue_scores(b, buf):
        start = pl.multiple_of(b * tk, tk)
        cms = []
        for h in range(MLA_HEADS):
            hsl = slice(h * MLA_HEAD_PAD, (h + 1) * MLA_HEAD_PAD)
            s = _dot(k_ref[pl.ds(start, tk), hsl], qt_ref[hsl, :])
            buf[h] = s
            cms.append(jnp.max(s, axis=0, keepdims=True))
        return tuple(cms)

    def consume(b, buf, cms, state, diag_offset=None):
        start = pl.multiple_of(b * tk, tk)
        probs = []
        for h in range(MLA_HEADS):
            s = buf[h]
            if diag_offset is None:
                cm = cms[h]
            else:
                s = jnp.where(key + diag_offset <= qry, s, -jnp.inf)
                cm = jnp.max(s, axis=0, keepdims=True)
            m_new = jnp.maximum(state[h][0], cm)
            probs.append((m_new, jnp.exp(s - m_new).astype(bf16)))
        new = []
        for h in range(MLA_HEADS):
            vsl = slice(h * MLA_VT_ROWS, (h + 1) * MLA_VT_ROWS)
            m, acc = state[h]
            m_new, p = probs[h]
            acc = jnp.exp(m - m_new) * acc + _dot(vt_ref[vsl, pl.ds(start, tk)], p)
            new.append((m_new, acc))
        return tuple(new)

    def body(t, carry):
        state, cm_a = carry
        cm_b = issue_scores(per * t + 1, sb_ref)
        state = consume(per * t, sa_ref, cm_a, state)
        cm_a = issue_scores(per * t + 2, sa_ref)
        state = consume(per * t + 1, sb_ref, cm_b, state)
        return state, cm_a

    assert per == 2
    init = tuple((jnp.full((1, tq), -jnp.inf, f32), jnp.zeros((MLA_VT_ROWS, tq), f32))
                 for _ in range(MLA_HEADS))
    carry = (init, issue_scores(0, sa_ref))
    carry = lax.fori_loop(0, i // 2, lambda u, c: body(2 * u + 1, body(2 * u, c)), carry)
    state, _ = lax.fori_loop(0, i % 2, lambda _, c: body(i - 1, c), carry)
    issue_scores(per * i + 1, sb_ref)
    state = consume(per * i, sa_ref, None, state, diag_offset=0)
    state = consume(per * i + 1, sb_ref, None, state, diag_offset=tk)
    o_t = jnp.concatenate([acc[:MLA_V] / acc[MLA_V:MLA_V + 1] for _, acc in state], axis=0)
    o_ref[...] = o_t.T.astype(bf16)


def _mla_attn(q_t, k, v_t, batch, seq):
    tq = ATTN_TILE
    nq = seq // tq
    hp = MLA_HEADS * MLA_HEAD_PAD
    hv = MLA_HEADS * MLA_V
    hvt = MLA_HEADS * MLA_VT_ROWS
    return pl.pallas_call(
        _mla_attn_kernel,
        out_shape=jax.ShapeDtypeStruct((batch * seq, hv), bf16),
        grid=(batch, nq),
        in_specs=[pl.BlockSpec((None, hp, tq), lambda b, i: (b, 0, i)),
                  pl.BlockSpec((seq, hp), lambda b, i: (b, 0)),
                  pl.BlockSpec((None, hvt, seq), lambda b, i: (b, 0, 0))],
        out_specs=pl.BlockSpec((tq, hv), lambda b, i: (b * nq + i, 0)),
        scratch_shapes=[pltpu.VMEM((MLA_HEADS, ATTN_KEY_TILE, tq), f32),
                        pltpu.VMEM((MLA_HEADS, ATTN_KEY_TILE, tq), f32)],
        compiler_params=_cparams(2),
        name="mla_attn",
    )(q_t, k, v_t)


def _sg_kernel(z_ref, lng_ref, lnb_ref, ws_ref, bst_ref, o_ref):
    z = z_ref[...]
    tm = z.shape[0]
    uv = 0.5 * z * (1.0 + jnp.tanh(np.sqrt(2.0 / np.pi) * (z + 0.044715 * (z * z * z))))
    u = uv[:, :SG_WIDTH]
    v = uv[:, SG_WIDTH:]
    mu = jnp.mean(v, axis=-1, keepdims=True)
    vc = v - mu
    vn = vc * lax.rsqrt(jnp.mean(vc * vc, axis=-1, keepdims=True) + NORM_EPS)
    vn = (vn * lng_ref[...] + lnb_ref[...]).astype(bf16)

    gw = SG_WIDTH // SG_GROUPS
    lane_group = _iota((SG_CHUNK, SG_WIDTH), 1) // gw
    tril = _iota((SG_CHUNK, SG_CHUNK), 0) >= _iota((SG_CHUNK, SG_CHUNK), 1)
    bias = jnp.zeros((SG_CHUNK, SG_WIDTH), f32)
    wm = []
    for g in range(SG_GROUPS):
        wm.append(jnp.where(tril, ws_ref[g], 0.0).astype(bf16))
        bias = jnp.where(lane_group == g,
                         jnp.broadcast_to(bst_ref[:, g:g + 1], (SG_CHUNK, SG_WIDTH)), bias)
    for c in range(tm // SG_CHUNK):
        rows = slice(c * SG_CHUNK, (c + 1) * SG_CHUNK)
        vch = vn[rows]
        mixed = bias
        for g in range(SG_GROUPS):
            mixed = mixed + jnp.where(lane_group == g, _dot(wm[g], vch), 0.0)
        o_ref[rows, :] = (u[rows] * mixed).astype(bf16)


def _sg(z_sg, lng, lnb, ws, bst):
    t = z_sg.shape[0]
    tm = TOKEN_TILE
    return pl.pallas_call(
        _sg_kernel,
        out_shape=jax.ShapeDtypeStruct((t, SG_WIDTH), bf16),
        grid=(t // tm,),
        in_specs=[pl.BlockSpec((tm, W_SG), lambda i: (i, 0)),
                  _const_spec((1, SG_WIDTH)),
                  _const_spec((1, SG_WIDTH)),
                  _const_spec((SG_GROUPS, SG_CHUNK, SG_CHUNK)),
                  _const_spec((SG_CHUNK, SG_GROUPS))],
        out_specs=pl.BlockSpec((tm, SG_WIDTH), lambda i: (i, 0)),
        compiler_params=_cparams(1),
        name="spatial_gating",
    )(z_sg, lng, lnb, ws, bst)


GLA_QK = GLA_HEADS * GLA_DK
GLA_V = GLA_HEADS * GLA_DV
GLA_SUB = 16


def _gla_kernel(z_ref, wg_ref, bg_ref, no_ref, o_ref,
                st_ref, cum_ref, qd_ref, upd_ref, dl_ref, stc_ref, oraw_ref):
    tc = z_ref.shape[0]
    c_len = GLA_CHUNK
    n_chunks = tc // c_len

    @pl.when(pl.program_id(1) == 0)
    def _():
        st_ref[...] = jnp.zeros_like(st_ref)

    zg = _dot(z_ref[:, 768:896].astype(bf16), wg_ref[...]) + bg_ref[...]
    log_a = (jnp.minimum(zg, 0.0) - jnp.log1p(jnp.exp(-jnp.abs(zg)))) * (1.0 / GLA_GATE_TEMP)

    tri = (_iota((c_len, c_len), 0) >= _iota((c_len, c_len), 1)).astype(f32)
    for c in range(n_chunks):
        cum_ref[c * c_len:(c + 1) * c_len, :] = _dot_sel(tri, log_a[c * c_len:(c + 1) * c_len])

    expand = (_iota((GLA_QK, GLA_V), 0) // GLA_DK == _iota((GLA_QK, GLA_V), 1) // GLA_DV).astype(bf16)
    st_mask = (_iota((GLA_V, GLA_QK), 0) // GLA_DV == _iota((GLA_V, GLA_QK), 1) // GLA_DK).astype(f32)
    sub = GLA_SUB
    row = _iota((sub, GLA_QK), 0)

    def intra(c):
        base = c * c_len
        rows = slice(base, base + c_len)
        cum = cum_ref[rows, :]
        q = z_ref[rows, 0:128] * (GLA_DK ** -0.5)
        k = z_ref[rows, 128:256]
        v = z_ref[rows, 256:512]
        v_t = v.T.astype(bf16)
        qd_ref[rows, :] = (q * jnp.exp(cum)).astype(bf16)
        last = cum[c_len - 1:c_len, :]
        dl_ref[c] = jnp.broadcast_to(jnp.exp(last), (8, GLA_QK))
        upd_ref[c] = _dot(v_t, (k * jnp.exp(last - cum)).astype(bf16)) * st_mask

        blocks = []
        for sb in range(c_len // sub):
            r0 = sb * sub
            q_i = q[r0:r0 + sub]
            cum_i = cum[r0:r0 + sub]
            ts = []
            for jj in range(sub):
                kj = z_ref[base + r0 + jj:base + r0 + jj + 1, 128:256]
                cj = cum_ref[base + r0 + jj:base + r0 + jj + 1, :]
                dec = jnp.exp(jnp.where(row >= jj, cum_i - cj, -1e30))
                ts.append((q_i * kj * dec).astype(bf16))
            r = _dot(jnp.concatenate(ts, axis=0), expand)
            o_sb = jnp.zeros((sub, GLA_V), f32)
            for jj in range(sub):
                o_sb = o_sb + r[jj * sub:(jj + 1) * sub] * z_ref[base + r0 + jj:base + r0 + jj + 1, 256:512]
            if sb > 0:
                c_prev = cum[r0 - 1:r0, :]
                kx = k[0:r0] * jnp.exp(c_prev - cum[0:r0])
                part = _dot(v_t[:, 0:r0], kx.astype(bf16)) * st_mask
                qx = q_i * jnp.exp(cum_i - c_prev)
                o_sb = o_sb + _dot_nt(qx.astype(bf16), part.astype(bf16))
            blocks.append(o_sb)
        oraw_ref[rows, :] = jnp.concatenate(blocks, axis=0)

    for c in range(n_chunks):
        intra(c)

    def recur(c, st):
        stc_ref[c] = st.astype(bf16)
        return dl_ref[c][0:1, :] * st + upd_ref[c]

    st_ref[...] = lax.fori_loop(0, n_chunks, recur, st_ref[...])
    for c in range(n_chunks):
        rows = slice(c * c_len, (c + 1) * c_len)
        oraw_ref[rows, :] += _dot_nt(qd_ref[rows, :], stc_ref[c])

    o = oraw_ref[...]
    ms = _group_stat(o * o, _group_mean_matrix(GLA_V, GLA_DV))
    og = z_ref[:, 512:768]
    o_ref[...] = (o * lax.rsqrt(ms + NORM_EPS) * no_ref[...] * (og * _sigmoid(og))).astype(bf16)


def _gla(z_gla, wg, bg, no, batch, seq):
    tc = SCAN_TILE
    nt = seq // tc
    n_chunks = tc // GLA_CHUNK
    return pl.pallas_call(
        _gla_kernel,
        out_shape=jax.ShapeDtypeStruct((batch * seq, GLA_V), bf16),
        grid=(batch, nt),
        in_specs=[pl.BlockSpec((tc, W_GLA), lambda b, i: (b * nt + i, 0)),
                  _const_spec((128, GLA_QK)),
                  _const_spec((1, GLA_QK)),
                  _const_spec((1, GLA_V))],
        out_specs=pl.BlockSpec((tc, GLA_V), lambda b, i: (b * nt + i, 0)),
        scratch_shapes=[pltpu.VMEM((GLA_V, GLA_QK), f32),
                        pltpu.VMEM((tc, GLA_QK), f32),
                        pltpu.VMEM((tc, GLA_QK), bf16),
                        pltpu.VMEM((n_chunks, GLA_V, GLA_QK), f32),
                        pltpu.VMEM((n_chunks, 8, GLA_QK), f32),
                        pltpu.VMEM((n_chunks, GLA_V, GLA_QK), bf16),
                        pltpu.VMEM((tc, GLA_V), f32)],
        compiler_params=_cparams(2),
        name="gla",
    )(z_gla, wg, bg, no)


DN_W = DN_HEADS * DN_DK
DN_CONV_PAD = 8


def _dot_3x(a, b):
    a_hi = a.astype(bf16)
    b_hi = b.astype(bf16)
    a_lo = (a - a_hi.astype(f32)).astype(bf16)
    b_lo = (b - b_hi.astype(f32)).astype(bf16)
    return _dot(a_hi, b_hi) + (_dot(a_hi, b_lo) + _dot(a_lo, b_hi))


def _dn_pre_kernel(z_ref, ab_ref, cw_ref, alog_ref, dtb_ref,
                   a_ref, attn_ref, bv_ref, bk_ref, qd_ref, kd_ref, egl_ref,
                   xbuf_ref, q_s, k_s, v_s, beta_s, g_s, gam_s, gamr_s):
    tc = z_ref.shape[0]
    c_len = DN_CHUNK
    qkv_w = 3 * DN_W

    @pl.when(pl.program_id(1) == 0)
    def _():
        xbuf_ref[0:DN_CONV_PAD, :] = jnp.zeros((DN_CONV_PAD, qkv_w), f32)

    xbuf_ref[DN_CONV_PAD:DN_CONV_PAD + tc, :] = z_ref[...]
    y = jnp.zeros((tc, qkv_w), f32)
    for kk in range(DN_CONV):
        y = y + cw_ref[kk:kk + 1, :] * xbuf_ref[pl.ds(DN_CONV_PAD - (DN_CONV - 1) + kk, tc), :]
    xbuf_ref[0:DN_CONV_PAD, :] = xbuf_ref[tc:tc + DN_CONV_PAD, :]
    y = y * _sigmoid(y)

    sum_mat = _group_mean_matrix(DN_W, DN_DK) * float(DN_DK)
    q = y[:, 0:DN_W]
    k = y[:, DN_W:2 * DN_W]
    q_s[...] = q * lax.rsqrt(_group_stat(q * q, sum_mat) + NORM_EPS) * (DN_DK ** -0.5)
    k_s[...] = k * lax.rsqrt(_group_stat(k * k, sum_mat) + NORM_EPS)
    v_s[...] = y[:, 2 * DN_W:3 * DN_W]

    ab = ab_ref[...]
    lane_head = _iota((tc, DN_W), 1) // DN_DK
    a_rep = jnp.zeros((tc, DN_W), f32)
    b_rep = jnp.zeros((tc, DN_W), f32)
    for h in range(DN_HEADS):
        a_rep = jnp.where(lane_head == h, jnp.broadcast_to(ab[:, h:h + 1], (tc, DN_W)), a_rep)
        b_rep = jnp.where(lane_head == h,
                          jnp.broadcast_to(ab[:, DN_HEADS + h:DN_HEADS + h + 1], (tc, DN_W)), b_rep)
    beta_s[...] = _sigmoid(b_rep)
    g_s[...] = -jnp.exp(alog_ref[...]) * _softplus(a_rep + dtb_ref[...])

    r64 = _iota((c_len, c_len), 0)
    c64 = _iota((c_len, c_len), 1)
    tri = (r64 >= c64).astype(f32)
    ones = jnp.ones((c_len, c_len), f32)
    row = _iota((c_len, DN_W), 0)
    colj = _iota((c_len, DN_W), 1) % c_len
    eye4 = (row == colj).astype(f32)
    incl = row >= colj
    strict = row > colj

    n_chunks = tc // c_len
    for c in range(n_chunks):
        rows = slice(c * c_len, (c + 1) * c_len)
        gam_s[rows, :] = _dot_sel(tri, g_s[rows, :])
    for c in range(n_chunks):
        rows = slice(c * c_len, (c + 1) * c_len)
        gamr_s[rows, :] = _dot_sel(ones, eye4 * gam_s[rows, :])

    def chunk(c):
        rows = slice(c * c_len, (c + 1) * c_len)
        gam = gam_s[rows, :]
        gam_row = gamr_s[rows, :]
        dec_incl = jnp.exp(jnp.where(incl, gam - gam_row, -1e30))
        dec_strict = jnp.where(strict, dec_incl, 0.0)
        eg = jnp.exp(gam)
        g_last = gam[c_len - 1:c_len, :]
        q = q_s[rows, :]
        k = k_s[rows, :]
        beta = beta_s[rows, :]
        bv_ref[rows, :] = beta * v_s[rows, :]
        bk_ref[rows, :] = beta * eg * k
        qd_ref[rows, :] = (q * eg).astype(bf16)
        kd_t = (k * jnp.exp(g_last - gam)).T.astype(bf16)
        egl_ref[c] = jnp.broadcast_to(jnp.exp(g_last), (8, DN_W))
        qb = q.astype(bf16)
        kb = k.astype(bf16)
        for h in range(DN_HEADS):
            sl = slice(h * DN_DK, (h + 1) * DN_DK)
            kd_ref[c * DN_HEADS + h] = kd_t[sl, :]
            a_ref[c * DN_HEADS + h] = beta[:, sl] * _dot_nt(kb[:, sl], kb[:, sl]) * dec_strict[:, sl]
            attn_ref[c * DN_HEADS + h] = (_dot_nt(qb[:, sl], kb[:, sl]) * dec_incl[:, sl]).astype(bf16)

    for c in range(n_chunks):
        chunk(c)


def _dn_pre(z_dn, cw, alog, dtb, batch, seq):
    tc = SCAN_TILE
    nt = seq // tc
    t = batch * seq
    npb = (tc // DN_CHUNK) * DN_HEADS
    n_prob = (t // DN_CHUNK) * DN_HEADS
    scr = lambda: pltpu.VMEM((tc, DN_W), f32)
    tok = lambda: pl.BlockSpec((tc, DN_W), lambda b, i: (b * nt + i, 0))
    mat = lambda: pl.BlockSpec((npb, DN_CHUNK, DN_CHUNK), lambda b, i: (b * nt + i, 0, 0))
    return pl.pallas_call(
        _dn_pre_kernel,
        out_shape=(jax.ShapeDtypeStruct((n_prob, DN_CHUNK, DN_CHUNK), f32),
                   jax.ShapeDtypeStruct((n_prob, DN_CHUNK, DN_CHUNK), bf16),
                   jax.ShapeDtypeStruct((t, DN_W), f32),
                   jax.ShapeDtypeStruct((t, DN_W), f32),
                   jax.ShapeDtypeStruct((t, DN_W), bf16),
                   jax.ShapeDtypeStruct((n_prob, DN_DK, DN_CHUNK), bf16),
                   jax.ShapeDtypeStruct((t // DN_CHUNK, 8, DN_W), f32)),
        grid=(batch, nt),
        in_specs=[pl.BlockSpec((tc, 3 * DN_W), lambda b, i: (b * nt + i, 0)),
                  pl.BlockSpec((tc, 128), lambda b, i: (b * nt + i, 4 * DN_W // 128)),
                  _const_spec((DN_CONV, 3 * DN_W)),
                  _const_spec((1, DN_W)),
                  _const_spec((1, DN_W))],
        out_specs=(mat(), mat(), tok(), tok(), tok(), mat(),
                   pl.BlockSpec((tc // DN_CHUNK, 8, DN_W), lambda b, i: (b * nt + i, 0, 0))),
        scratch_shapes=[pltpu.VMEM((tc + 2 * DN_CONV_PAD, 3 * DN_W), f32),
                        scr(), scr(), scr(), scr(), scr(), scr(), scr()],
        compiler_params=_cparams(2),
        name="dn_pre",
    )(z_dn, z_dn, cw, alog, dtb)


def _dn_solve_kernel(a_ref, t_ref):
    c_len = DN_CHUNK
    blk = DN_INV_BLOCK
    vshape = a_ref.shape[1:]

    def row_i(i, carry):
        for mb in range(c_len // blk):
            m0 = mb * blk

            def col_j(j, acc):
                a_ij = a_ref[i * c_len + j]
                return tuple(acc[m] - a_ij * t_ref[j * c_len + m0 + m] for m in range(blk))

            acc = lax.fori_loop(m0, i, col_j, tuple(jnp.zeros(vshape, f32) for _ in range(blk)))
            for m in range(blk):
                t_ref[i * c_len + m0 + m] = acc[m] + jnp.where(i == m0 + m, 1.0, 0.0)
        return carry

    lax.fori_loop(0, c_len, row_i, 0)


def _dn_solve(a_t):
    n_ent, n_grp, lanes = a_t.shape
    g = min(n_grp, 8)
    spec = lambda: pl.BlockSpec((n_ent, g, lanes), lambda i: (0, i, 0), pipeline_mode=pl.Buffered(1))
    return pl.pallas_call(
        _dn_solve_kernel,
        out_shape=jax.ShapeDtypeStruct(a_t.shape, f32),
        grid=(n_grp // g,),
        in_specs=[spec()],
        out_specs=spec(),
        compiler_params=_cparams(1),
        name="dn_solve",
    )(a_t)


def _dn_scan_kernel(t_ref, attn_ref, bv_ref, bk_ref, qd_ref, kd_ref, egl_ref, zg_ref, no_ref,
                    o_ref, st_ref, sol_ref, oraw_ref):
    tc = bv_ref.shape[0]
    c_len = DN_CHUNK
    heads = range(DN_HEADS)
    hsl = [slice(h * DN_DK, (h + 1) * DN_DK) for h in heads]

    @pl.when(pl.program_id(1) == 0)
    def _():
        st_ref[...] = jnp.zeros_like(st_ref)

    def solve(c):
        rows = slice(c * c_len, (c + 1) * c_len)
        bv = bv_ref[rows, :]
        bk = bk_ref[rows, :]
        sols = [_dot_3x(t_ref[c * DN_HEADS + h], jnp.concatenate([bv[:, hsl[h]], bk[:, hsl[h]]], axis=-1))
                for h in heads]
        for h in heads:
            sol_ref[c * DN_HEADS + h] = sols[h]

    for c in range(tc // c_len):
        solve(c)

    def chunk(c):
        rows = slice(c * c_len, (c + 1) * c_len)
        qd = qd_ref[rows, :]
        eg_last = egl_ref[c][0:1, :]
        sols = [sol_ref[c * DN_HEADS + h] for h in heads]
        sts = [st_ref[h] for h in heads]
        xq = [_dot(jnp.concatenate([sols[h][:, DN_DV:].astype(bf16), qd[:, hsl[h]]], axis=0),
                   sts[h].astype(bf16)) for h in heads]
        ub = [(sols[h][:, :DN_DV] - xq[h][:c_len]).astype(bf16) for h in heads]
        au = [_dot(jnp.concatenate([attn_ref[c * DN_HEADS + h], kd_ref[c * DN_HEADS + h]], axis=0), ub[h])
              for h in heads]
        for h in heads:
            st_ref[h] = eg_last[:, hsl[h]] * sts[h] + au[h][c_len:]
        oraw_ref[rows, :] = jnp.concatenate([xq[h][c_len:] + au[h][:c_len] for h in heads], axis=-1)

    for c in range(tc // c_len):
        chunk(c)

    o = oraw_ref[...]
    ms = _group_stat(o * o, _group_mean_matrix(DN_W, DN_DV))
    zg = zg_ref[...]
    o_ref[...] = (o * lax.rsqrt(ms + NORM_EPS) * no_ref[...] * (zg * _sigmoid(zg))).astype(bf16)


def _dn_scan(t_all, attn, bv, bk, qd, kd, egl, z_dn, no, batch, seq):
    tc = SCAN_TILE
    nt = seq // tc
    npb = (tc // DN_CHUNK) * DN_HEADS
    tok = lambda: pl.BlockSpec((tc, DN_W), lambda b, i: (b * nt + i, 0))
    mat = lambda: pl.BlockSpec((npb, DN_CHUNK, DN_CHUNK), lambda b, i: (b * nt + i, 0, 0))
    return pl.pallas_call(
        _dn_scan_kernel,
        out_shape=jax.ShapeDtypeStruct((batch * seq, DN_W), bf16),
        grid=(batch, nt),
        in_specs=[mat(), mat(), tok(), tok(), tok(), mat(),
                  pl.BlockSpec((tc // DN_CHUNK, 8, DN_W), lambda b, i: (b * nt + i, 0, 0)),
                  pl.BlockSpec((tc, DN_W), lambda b, i: (b * nt + i, 3)),
                  _const_spec((1, DN_W))],
        out_specs=tok(),
        scratch_shapes=[pltpu.VMEM((DN_HEADS, DN_DK, DN_DV), f32),
                        pltpu.VMEM((npb, DN_CHUNK, 2 * DN_DV), f32),
                        pltpu.VMEM((tc, DN_W), f32)],
        compiler_params=_cparams(2),
        name="dn_scan",
    )(t_all, attn, bv, bk, qd, kd, egl, z_dn, no)


def _dn(z_dn, cw, alog, dtb, no, batch, seq):
    a_all, attn, bv, bk, qd, kd, egl = _dn_pre(z_dn, cw, alog, dtb, batch, seq)
    n_prob = a_all.shape[0]
    n_ent = DN_CHUNK * DN_CHUNK
    a_t = a_all.reshape(n_prob, n_ent).T.reshape(n_ent, n_prob // 128, 128)
    t_all = _dn_solve(a_t).reshape(n_ent, n_prob).T.reshape(n_prob, DN_CHUNK, DN_CHUNK)
    return _dn_scan(t_all, attn, bv, bk, qd, kd, egl, z_dn, no, batch, seq)


def _merge_kernel(x_ref, g_ref, oa_ref, ob_ref, oc_ref, od_ref, wg_ref, wb_ref, wo_ref, out_ref):
    x = x_ref[...]
    h = _rms(x, g_ref[...]).astype(bf16)
    merged = jnp.zeros(x.shape, f32)
    for n, o_ref in enumerate((oa_ref, ob_ref, oc_ref, od_ref)):
        gate = _sigmoid(_dot(h, wg_ref[:, n * D_MODEL:(n + 1) * D_MODEL]))
        merged = merged + gate * _dot(o_ref[...], wb_ref[n])
    out_ref[...] = x + _dot(merged.astype(bf16), wo_ref[...])


def _merge(x2, g, oa, ob, oc, od, wg, wb, wo):
    t = x2.shape[0]
    tm = TOKEN_TILE
    tok = lambda w: pl.BlockSpec((tm, w), lambda i: (i, 0))
    return pl.pallas_call(
        _merge_kernel,
        out_shape=jax.ShapeDtypeStruct((t, D_MODEL), f32),
        grid=(t // tm,),
        in_specs=[tok(D_MODEL), _const_spec((1, D_MODEL)),
                  tok(BRANCH_WIDTH), tok(BRANCH_WIDTH), tok(BRANCH_WIDTH), tok(BRANCH_WIDTH),
                  _const_spec((D_MODEL, N_BRANCH * D_MODEL)),
                  _const_spec((N_BRANCH, BRANCH_WIDTH, D_MODEL)),
                  _const_spec((D_MODEL, D_MODEL))],
        out_specs=tok(D_MODEL),
        compiler_params=_cparams(1),
        name="merge",
    )(x2, g, oa, ob, oc, od, wg, wb, wo)


def _memkv_kernel(mem_ref, g_ref, wk_ref, wv_ref, k_ref, v_ref):
    mn = _rms(mem_ref[...], g_ref[...]).astype(bf16)
    k_ref[...] = _dot(mn, wk_ref[...]).astype(bf16)
    v_ref[...] = _dot(mn, wv_ref[...]).astype(bf16)


def _memkv(mem2, g, wk, wv, mem_len):
    t = mem2.shape[0]
    blk = lambda: pl.BlockSpec((mem_len, D_MODEL), lambda i: (i, 0))
    return pl.pallas_call(
        _memkv_kernel,
        out_shape=(jax.ShapeDtypeStruct((t, D_MODEL), bf16), jax.ShapeDtypeStruct((t, D_MODEL), bf16)),
        grid=(t // mem_len,),
        in_specs=[blk(), _const_spec((1, D_MODEL)),
                  _const_spec((D_MODEL, D_MODEL)), _const_spec((D_MODEL, D_MODEL))],
        out_specs=(blk(), blk()),
        compiler_params=_cparams(1),
        name="mem_kv",
    )(mem2, g, wk, wv)


def _xattn_kernel(x_ref, g_ref, k_ref, v_ref, wq_ref, wo_ref, out_ref):
    x = x_ref[...]
    h = _rms(x, g_ref[...]).astype(bf16)
    q = (_dot(h, wq_ref[...]) * (XATTN_DIM ** -0.5)).astype(bf16)
    outs = []
    for hh in range(XATTN_HEADS):
        sl = slice(hh * XATTN_DIM, (hh + 1) * XATTN_DIM)
        s = _dot_nt(q[:, sl], k_ref[:, sl])
        p = jnp.exp(s - jnp.max(s, axis=-1, keepdims=True))
        p = p / jnp.sum(p, axis=-1, keepdims=True)
        outs.append(_dot(p.astype(bf16), v_ref[:, sl]).astype(bf16))
    o = jnp.concatenate(outs, axis=-1)
    out_ref[...] = x + _dot(o, wo_ref[...])


def _xattn(x2, g, kmem, vmem, wq, wo, batch, seq, mem_len):
    tm = TOKEN_TILE
    nt = seq // tm
    tok = pl.BlockSpec((tm, D_MODEL), lambda b, i: (b * nt + i, 0))
    memb = lambda: pl.BlockSpec((mem_len, D_MODEL), lambda b, i: (b, 0))
    return pl.pallas_call(
        _xattn_kernel,
        out_shape=jax.ShapeDtypeStruct((batch * seq, D_MODEL), f32),
        grid=(batch, nt),
        in_specs=[tok, _const_spec((1, D_MODEL)), memb(), memb(),
                  _const_spec((D_MODEL, D_MODEL)), _const_spec((D_MODEL, D_MODEL))],
        out_specs=tok,
        compiler_params=_cparams(2),
        name="xattn",
    )(x2, g, kmem, vmem, wq, wo)


def _mlp_kernel(x_ref, g_ref, wu_ref, wd_ref, gf_ref, out_ref, *, final):
    x = x_ref[...]
    h = _rms(x, g_ref[...]).astype(bf16)
    acc = x
    for c in range(MLP_HIDDEN // MLP_CHUNK):
        sl = slice(c * MLP_CHUNK, (c + 1) * MLP_CHUNK)
        a = jnp.maximum(_dot(h, wu_ref[:, sl]), 0.0)
        acc = acc + _dot((a * a).astype(bf16), wd_ref[sl, :])
    if final:
        acc = _rms(acc, gf_ref[...])
    out_ref[...] = acc


def _mlp(x2, g, wu, wd, gf, final):
    t = x2.shape[0]
    tm = TOKEN_TILE
    tok = pl.BlockSpec((tm, D_MODEL), lambda i: (i, 0))
    return pl.pallas_call(
        functools.partial(_mlp_kernel, final=final),
        out_shape=jax.ShapeDtypeStruct((t, D_MODEL), f32),
        grid=(t // tm,),
        in_specs=[tok, _const_spec((1, D_MODEL)),
                  _const_spec((D_MODEL, MLP_HIDDEN)), _const_spec((MLP_HIDDEN, D_MODEL)),
                  _const_spec((1, D_MODEL))],
        out_specs=tok,
        compiler_params=_cparams(1),
        name="mlp",
    )(x2, g, wu, wd, gf)


def _pad_cols(w, width):
    return jnp.pad(w, ((0, 0), (0, width - w.shape[1])))


def _prep_w_in(w_in):
    col = lambda i: w_in[:, IN_OFFS[i]:IN_OFFS[i + 1]]
    zeros = lambda n: jnp.zeros((D_MODEL, n), w_in.dtype)
    (cq, ckv, kpe, sg_uv, gq, gk, gv, g_lr, g_og, dq, dk, dv, da, db, dz) = (col(i) for i in range(15))
    small = jnp.concatenate([
        cq, ckv, zeros(MLA_NOPE), kpe, zeros(MLA_HEAD_PAD - MLA_NOPE - MLA_ROPE),
        sg_uv,
        gq, gk, gv, g_og, _pad_cols(g_lr, 128),
        dq, dk, dv, dz, _pad_cols(jnp.concatenate([da, db], axis=1), 128),
    ], axis=1)
    assert small.shape[1] == W_SMALL
    return small.astype(bf16), col(15).astype(bf16)


def _rope_lane_table():
    inv_freq = ROPE_THETA ** (-jnp.arange(0, MLA_ROPE, 2, dtype=f32) / MLA_ROPE)
    half = MLA_ROPE // 2
    tab = jnp.zeros((MLA_HEAD_PAD,), f32)
    tab = tab.at[MLA_NOPE:MLA_NOPE + half].set(inv_freq)
    tab = tab.at[MLA_NOPE + half:MLA_NOPE + MLA_ROPE].set(inv_freq)
    return tab[None, :]


def kernel(x, mem, positions, norm_mix, w_in, mla_norm_q, mla_norm_kv, mla_w_uq, mla_w_ukv, sg_ln_g, sg_ln_b, sg_w, sg_b, gla_w_gate, gla_b_gate, gla_norm, dn_conv, dn_a_log, dn_dt_bias, dn_norm, w_branch, w_out, norm_xattn, norm_mem, xattn_wq, xattn_wk, xattn_wv, xattn_wo, norm_mlp, w_up, w_down, norm_final):
    batch, seq, _ = x.shape
    mem_len = mem.shape[1]
    depth = w_in.shape[0]
    assert seq % TOKEN_TILE == 0 and seq % SCAN_TILE == 0 and seq % ATTN_TILE == 0
    x2 = x.reshape(batch * seq, D_MODEL)
    mem2 = mem.reshape(batch * mem_len, D_MODEL)
    pos2 = positions.reshape(batch * seq, 1)
    invf = _rope_lane_table()
    row = lambda v: v[None, :]

    for l in range(depth):
        w_small, w_gates = _prep_w_in(w_in[l])
        z_mla, z_sg, z_gla, z_dn = _inproj(x2, row(norm_mix[l]), w_small)

        wuq = jnp.pad(mla_w_uq[l].reshape(MLA_Q_RANK, MLA_HEADS, MLA_NOPE + MLA_ROPE),
                      ((0, 0), (0, 0), (0, MLA_HEAD_PAD - MLA_NOPE - MLA_ROPE)))
        wuq = wuq.reshape(MLA_Q_RANK, MLA_HEADS * MLA_HEAD_PAD).astype(bf16)
        wukv = mla_w_ukv[l].reshape(MLA_KV_RANK, MLA_HEADS, MLA_NOPE + MLA_V)
        wuk = jnp.pad(wukv[:, :, :MLA_NOPE], ((0, 0), (0, 0), (0, MLA_HEAD_PAD - MLA_NOPE)))
        wuk = wuk.reshape(MLA_KV_RANK, MLA_HEADS * MLA_HEAD_PAD).astype(bf16)
        wuv = wukv[:, :, MLA_NOPE:].reshape(MLA_KV_RANK, MLA_HEADS * MLA_V).astype(bf16)
        q_a, k_a, v_a = _mla_pre(z_mla, pos2, invf, row(mla_norm_q[l]), row(mla_norm_kv[l]), wuq, wuk, wuv,
                                 batch, seq)
        o_a = _mla_attn(q_a, k_a, v_a, batch, seq)

        o_b = _sg(z_sg, row(sg_ln_g[l]), row(sg_ln_b[l]), sg_w[l], sg_b[l].T)

        wg = jnp.pad(gla_w_gate[l], ((0, 128 - GLA_GATE_RANK), (0, 0))).astype(bf16)
        o_c = _gla(z_gla, wg, row(gla_b_gate[l]), row(jnp.tile(gla_norm[l], GLA_HEADS)), batch, seq)

        o_d = _dn(z_dn, dn_conv[l], row(jnp.repeat(dn_a_log[l], DN_DK)), row(jnp.repeat(dn_dt_bias[l], DN_DK)),
                  row(jnp.tile(dn_norm[l], DN_HEADS)), batch, seq)

        x2 = _merge(x2, row(norm_mix[l]), o_a, o_b, o_c, o_d, w_gates,
                    w_branch[l].astype(bf16), w_out[l].astype(bf16))

        k_m, v_m = _memkv(mem2, row(norm_mem[l]), xattn_wk[l].astype(bf16), xattn_wv[l].astype(bf16), mem_len)
        x2 = _xattn(x2, row(norm_xattn[l]), k_m, v_m, xattn_wq[l].astype(bf16), xattn_wo[l].astype(bf16),
                    batch, seq, mem_len)
        x2 = _mlp(x2, row(norm_mlp[l]), w_up[l].astype(bf16), w_down[l].astype(bf16), row(norm_final),
                  final=(l == depth - 1))
    return x2.reshape(batch, seq, D_MODEL)
```

```python
import functools

import jax
import jax.numpy as jnp
import numpy as np
from jax import lax
from jax.experimental import pallas as pl
from jax.experimental.pallas import tpu as pltpu

f32 = jnp.float32
bf16 = jnp.bfloat16
HIGHEST = lax.Precision.HIGHEST

D_MODEL = 1024
NORM_EPS = 1e-6
N_BRANCH = 4
BRANCH_WIDTH = 256

MLA_HEADS = 4
MLA_Q_RANK = 256
MLA_KV_RANK = 128
MLA_NOPE = 64
MLA_ROPE = 32
MLA_V = 64
ROPE_THETA = 10000.0
MLA_HEAD_PAD = 128
MLA_VT_ROWS = 80

SG_GROUPS = 4
SG_WIDTH = 256
SG_CHUNK = 128

GLA_HEADS = 4
GLA_DK = 32
GLA_DV = 64
GLA_GATE_RANK = 16
GLA_GATE_TEMP = 16.0
GLA_CHUNK = 64

DN_HEADS = 4
DN_DK = 64
DN_DV = 64
DN_CONV = 4
DN_CHUNK = 64
DN_INV_BLOCK = 16

XATTN_HEADS = 4
XATTN_DIM = D_MODEL // XATTN_HEADS
MLP_HIDDEN = 4 * D_MODEL

IN_SIZES = (
    MLA_Q_RANK, MLA_KV_RANK, MLA_ROPE,
    2 * SG_WIDTH,
    GLA_HEADS * GLA_DK, GLA_HEADS * GLA_DK, GLA_HEADS * GLA_DV,
    GLA_GATE_RANK, GLA_HEADS * GLA_DV,
    DN_HEADS * DN_DK, DN_HEADS * DN_DK, DN_HEADS * DN_DV,
    DN_HEADS, DN_HEADS, DN_HEADS * DN_DV,
    N_BRANCH * D_MODEL,
)
IN_OFFS = tuple(int(v) for v in np.cumsum((0,) + IN_SIZES))

W_MLA = 512
W_SG = 512
W_GLA = 896
W_DN = 1152
W_SMALL = W_MLA + W_SG + W_GLA + W_DN

TOKEN_TILE = 512
ATTN_TILE = 256
ATTN_KEY_TILE = 128
SCAN_TILE = 512
MLP_CHUNK = 1024
VMEM_LIMIT = 56 * 1024 * 1024


def _cparams(n_axes):
    return pltpu.CompilerParams(dimension_semantics=("arbitrary",) * n_axes,
                                vmem_limit_bytes=VMEM_LIMIT)


def _const_spec(shape):
    nd = len(shape)
    return pl.BlockSpec(shape, lambda *_: (0,) * nd, pipeline_mode=pl.Buffered(1))


def _rms(x, g):
    return x * lax.rsqrt(jnp.mean(x * x, axis=-1, keepdims=True) + NORM_EPS) * g


def _sigmoid(x):
    return 1.0 / (1.0 + jnp.exp(-x))


def _softplus(x):
    return jnp.maximum(x, 0.0) + jnp.log1p(jnp.exp(-jnp.abs(x)))


def _dot(a, b):
    return jnp.dot(a, b, preferred_element_type=f32)


def _split_bf16(x, terms):
    parts = []
    for _ in range(terms):
        p = x.astype(bf16)
        parts.append(p)
        x = x - p.astype(f32)
    return parts


def _dot_sel(sel, x, terms=3):
    sel = sel.astype(bf16)
    parts = [_dot(sel, p) for p in _split_bf16(x, terms)]
    return functools.reduce(lambda a, b: a + b, reversed(parts))


def _group_stat(x, mat, terms=2):
    mat = mat.astype(bf16)
    parts = [_dot(p, mat) for p in _split_bf16(x, terms)]
    return functools.reduce(lambda a, b: a + b, reversed(parts))


def _dot_nt(a, b):
    return lax.dot_general(a, b, (((1,), (1,)), ((), ())), preferred_element_type=f32)


def _dot_tn(a, b):
    return lax.dot_general(a, b, (((0,), (0,)), ((), ())), preferred_element_type=f32)


def _iota(shape, dim):
    return lax.broadcasted_iota(jnp.int32, shape, dim)


def _group_mean_matrix(width, group):
    r = _iota((width, width), 0) // group
    c = _iota((width, width), 1) // group
    return jnp.where(r == c, 1.0 / group, 0.0).astype(f32)


DN_CONV_TAPS = 4
DN_CONV_COLS = 768
DN_CONV_PAD = 8


def _inproj_kernel(x_ref, g_ref, w_ref, cw_ref, mla_ref, sg_ref, gla_ref, dn_ref, xbuf_ref, *,
                   tiles_per_seq):
    @pl.when(pl.program_id(0) % tiles_per_seq == 0)
    def _():
        xbuf_ref[0:DN_CONV_PAD, :] = jnp.zeros((DN_CONV_PAD, DN_CONV_COLS), f32)

    h = _rms(x_ref[...], g_ref[...]).astype(bf16)
    tm = h.shape[0]
    z_dn = _dot(h, w_ref[:, W_SMALL - W_DN:])
    dn_ref[:, DN_CONV_COLS:] = z_dn[:, DN_CONV_COLS:]
    xbuf_ref[DN_CONV_PAD:DN_CONV_PAD + tm, :] = z_dn[:, :DN_CONV_COLS]
    y = jnp.zeros((tm, DN_CONV_COLS), f32)
    for kk in range(DN_CONV_TAPS):
        y = y + cw_ref[kk:kk + 1, :] * xbuf_ref[pl.ds(DN_CONV_PAD - (DN_CONV_TAPS - 1) + kk, tm), :]
    xbuf_ref[0:DN_CONV_PAD, :] = xbuf_ref[tm:tm + DN_CONV_PAD, :]
    dn_ref[:, :DN_CONV_COLS] = y * _sigmoid(y)
    lo = 0
    for ref, width in ((mla_ref, W_MLA), (sg_ref, W_SG), (gla_ref, W_GLA)):
        ref[...] = _dot(h, w_ref[:, lo:lo + width])
        lo += width


def _inproj(x2, g, w_small, conv_w, seq):
    t = x2.shape[0]
    tm = TOKEN_TILE
    outs = tuple(jax.ShapeDtypeStruct((t, w), f32) for w in (W_MLA, W_SG, W_GLA, W_DN))
    return pl.pallas_call(
        functools.partial(_inproj_kernel, tiles_per_seq=seq // tm),
        out_shape=outs,
        grid=(t // tm,),
        in_specs=[pl.BlockSpec((tm, D_MODEL), lambda i: (i, 0)),
                  _const_spec((1, D_MODEL)),
                  _const_spec((D_MODEL, W_SMALL)),
                  _const_spec((DN_CONV_TAPS, DN_CONV_COLS))],
        out_specs=tuple(pl.BlockSpec((tm, w), lambda i: (i, 0)) for w in (W_MLA, W_SG, W_GLA, W_DN)),
        scratch_shapes=[pltpu.VMEM((tm + 2 * DN_CONV_PAD, DN_CONV_COLS), f32)],
        compiler_params=_cparams(1),
        name="inproj",
    )(x2, g, w_small, conv_w)


def _mla_pre_kernel(z_ref, pos_ref, invf_ref, nq_ref, nkv_ref, wuq_ref, wuk_ref, wuv_ref,
                    qt_ref, k_ref, vt_ref):
    z = z_ref[...]
    tm = z.shape[0]
    cq = z[:, :MLA_Q_RANK]
    ckv = z[:, MLA_Q_RANK:MLA_Q_RANK + MLA_KV_RANK]
    kpe = z[:, MLA_Q_RANK + MLA_KV_RANK:]

    half = MLA_ROPE // 2
    ang_t = invf_ref[...] * pos_ref[...].astype(f32)
    f_row = _iota((half, MLA_HEAD_PAD), 0)
    f_lane = _iota((half, MLA_HEAD_PAD), 1)
    place = ((f_lane == MLA_NOPE + f_row) | (f_lane == MLA_NOPE + half + f_row)).astype(bf16)

    def to_lanes(x_t):
        parts = [_dot_tn(p, place) for p in _split_bf16(x_t, 3)]
        return functools.reduce(lambda a, b: a + b, reversed(parts))

    cos = to_lanes(jnp.cos(ang_t))
    sin = to_lanes(jnp.sin(ang_t))
    lane = _iota((tm, MLA_HEAD_PAD), 1)
    in_x1 = (lane >= MLA_NOPE) & (lane < MLA_NOPE + half)
    in_x2 = (lane >= MLA_NOPE + half) & (lane < MLA_NOPE + MLA_ROPE)
    c_tab = jnp.where(in_x1 | in_x2, cos, 1.0)
    s1_tab = jnp.where(in_x1, -sin, 0.0)
    s2_tab = jnp.where(in_x2, sin, 0.0)

    def rope(t):
        return (t * c_tab + pltpu.roll(t, MLA_HEAD_PAD - half, 1) * s1_tab
                + pltpu.roll(t, half, 1) * s2_tab)

    scale = (MLA_NOPE + MLA_ROPE) ** -0.5
    q = _dot(_rms(cq, nq_ref[...]).astype(bf16), wuq_ref[...])
    kvn = _rms(ckv, nkv_ref[...]).astype(bf16)
    kn = _dot(kvn, wuk_ref[...])
    v_t = _dot(kvn, wuv_ref[...]).T
    kpe_rot = rope(kpe)
    q_rot = []
    for h in range(MLA_HEADS):
        sl = slice(h * MLA_HEAD_PAD, (h + 1) * MLA_HEAD_PAD)
        q_rot.append(rope(q[:, sl]) * scale)
        k_ref[:, sl] = (kn[:, sl] + kpe_rot).astype(bf16)
    qt_ref[...] = jnp.concatenate(q_rot, axis=-1).T.astype(bf16)
    pad = jnp.concatenate([jnp.ones((1, tm), f32), jnp.zeros((MLA_VT_ROWS - MLA_V - 1, tm), f32)], axis=0)
    rows = []
    for h in range(MLA_HEADS):
        rows += [v_t[h * MLA_V:(h + 1) * MLA_V], pad]
    vt_ref[...] = jnp.concatenate(rows, axis=0).astype(bf16)


def _mla_pre(z_mla, pos2, invf, nq, nkv, wuq, wuk, wuv, batch, seq):
    t = z_mla.shape[0]
    tm = TOKEN_TILE
    nt = seq // tm
    hp = MLA_HEADS * MLA_HEAD_PAD
    hv = MLA_HEADS * MLA_V
    hvt = MLA_HEADS * MLA_VT_ROWS
    return pl.pallas_call(
        _mla_pre_kernel,
        out_shape=(jax.ShapeDtypeStruct((batch, hp, seq), bf16),
                   jax.ShapeDtypeStruct((t, hp), bf16),
                   jax.ShapeDtypeStruct((batch, hvt, seq), bf16)),
        grid=(t // tm,),
        in_specs=[pl.BlockSpec((tm, W_MLA), lambda i: (i, 0)),
                  pl.BlockSpec((None, 1, tm), lambda i: (i, 0, 0)),
                  _const_spec((MLA_ROPE // 2, 1)),
                  _const_spec((1, MLA_Q_RANK)),
                  _const_spec((1, MLA_KV_RANK)),
                  _const_spec((MLA_Q_RANK, hp)),
                  _const_spec((MLA_KV_RANK, hp)),
                  _const_spec((MLA_KV_RANK, hv))],
        out_specs=(pl.BlockSpec((None, hp, tm), lambda i: (i // nt, 0, i % nt)),
                   pl.BlockSpec((tm, hp), lambda i: (i, 0)),
                   pl.BlockSpec((None, hvt, tm), lambda i: (i // nt, 0, i % nt))),
        compiler_params=_cparams(1),
        name="mla_pre",
    )(z_mla, pos2, invf, nq, nkv, wuq, wuk, wuv)


def _mla_attn_kernel(qt_ref, k_ref, vt_ref, o_ref, sa_ref, sb_ref):
    i = pl.program_id(1)
    tq = qt_ref.shape[1]
    tk = sa_ref.shape[1]
    per = tq // tk
    key = _iota((tk, tq), 0)
    qry = _iota((tk, tq), 1)

    def issue_scores(b, buf):
        start = pl.multiple_of(b * tk, tk)
        cms = []
        for h in range(MLA_HEADS):
            hsl = slice(h * MLA_HEAD_PAD, (h + 1) * MLA_HEAD_PAD)
            s = _dot(k_ref[pl.ds(start, tk), hsl], qt_ref[hsl, :])
            buf[h] = s
            cms.append(jnp.max(s, axis=0, keepdims=True))
        return tuple(cms)

    def consume(b, buf, cms, state, diag_offset=None):
        start = pl.multiple_of(b * tk, tk)
        probs = []
        for h in range(MLA_HEADS):
            s = buf[h]
            if diag_offset is None:
                cm = cms[h]
            else:
                s = jnp.where(key + diag_offset <= qry, s, -jnp.inf)
                cm = jnp.max(s, axis=0, keepdims=True)
            m_new = jnp.maximum(state[h][0], cm)
            probs.append((m_new, jnp.exp(s - m_new).astype(bf16)))
        new = []
        for h in range(MLA_HEADS):
            vsl = slice(h * MLA_VT_ROWS, (h + 1) * MLA_VT_ROWS)
            m, acc = state[h]
            m_new, p = probs[h]
            acc = jnp.exp(m - m_new) * acc + _dot(vt_ref[vsl, pl.ds(start, tk)], p)
            new.append((m_new, acc))
        return tuple(new)

    def body(t, carry):
        state, cm_a = carry
        cm_b = issue_scores(per * t + 1, sb_ref)
        state = consume(per * t, sa_ref, cm_a, state)
        cm_a = issue_scores(per * t + 2, sa_ref)
        state = consume(per * t + 1, sb_ref, cm_b, state)
        return state, cm_a

    assert per == 2
    init = tuple((jnp.full((1, tq), -jnp.inf, f32), jnp.zeros((MLA_VT_ROWS, tq), f32))
                 for _ in range(MLA_HEADS))
    carry = (init, issue_scores(0, sa_ref))
    carry = lax.fori_loop(0, i // 2, lambda u, c: body(2 * u + 1, body(2 * u, c)), carry)
    state, _ = lax.fori_loop(0, i % 2, lambda _, c: body(i - 1, c), carry)
    issue_scores(per * i + 1, sb_ref)
    state = consume(per * i, sa_ref, None, state, diag_offset=0)
    state = consume(per * i + 1, sb_ref, None, state, diag_offset=tk)
    o_t = jnp.concatenate([acc[:MLA_V] / acc[MLA_V:MLA_V + 1] for _, acc in state], axis=0)
    o_ref[...] = o_t.T.astype(bf16)


def _mla_attn(q_t, k, v_t, batch, seq):
    tq = ATTN_TILE
    nq = seq // tq
    hp = MLA_HEADS * MLA_HEAD_PAD
    hv = MLA_HEADS * MLA_V
    hvt = MLA_HEADS * MLA_VT_ROWS
    return pl.pallas_call(
        _mla_attn_kernel,
        out_shape=jax.ShapeDtypeStruct((batch * seq, hv), bf16),
        grid=(batch, nq),
        in_specs=[pl.BlockSpec((None, hp, tq), lambda b, i: (b, 0, i)),
                  pl.BlockSpec((seq, hp), lambda b, i: (b, 0)),
                  pl.BlockSpec((None, hvt, seq), lambda b, i: (b, 0, 0))],
        out_specs=pl.BlockSpec((tq, hv), lambda b, i: (b * nq + i, 0)),
        scratch_shapes=[pltpu.VMEM((MLA_HEADS, ATTN_KEY_TILE, tq), f32),
                        pltpu.VMEM((MLA_HEADS, ATTN_KEY_TILE, tq), f32)],
        compiler_params=_cparams(2),
        name="mla_attn",
    )(q_t, k, v_t)


def _sg_kernel(z_ref, lng_ref, lnb_ref, ws_ref, bst_ref, o_ref):
    z = z_ref[...]
    tm = z.shape[0]
    uv = 0.5 * z * (1.0 + jnp.tanh(np.sqrt(2.0 / np.pi) * (z + 0.044715 * (z * z * z))))
    u = uv[:, :SG_WIDTH]
    v = uv[:, SG_WIDTH:]
    mu = jnp.mean(v, axis=-1, keepdims=True)
    vc = v - mu
    vn = vc * lax.rsqrt(jnp.mean(vc * vc, axis=-1, keepdims=True) + NORM_EPS)
    vn = (vn * lng_ref[...] + lnb_ref[...]).astype(bf16)

    gw = SG_WIDTH // SG_GROUPS
    lane_group = _iota((SG_CHUNK, SG_WIDTH), 1) // gw
    tril = _iota((SG_CHUNK, SG_CHUNK), 0) >= _iota((SG_CHUNK, SG_CHUNK), 1)
    bias = jnp.zeros((SG_CHUNK, SG_WIDTH), f32)
    wm = []
    for g in range(SG_GROUPS):
        wm.append(jnp.where(tril, ws_ref[g], 0.0).astype(bf16))
        bias = jnp.where(lane_group == g,
                         jnp.broadcast_to(bst_ref[:, g:g + 1], (SG_CHUNK, SG_WIDTH)), bias)
    for c in range(tm // SG_CHUNK):
        rows = slice(c * SG_CHUNK, (c + 1) * SG_CHUNK)
        vch = vn[rows]
        mixed = bias
        for g in range(SG_GROUPS):
            mixed = mixed + jnp.where(lane_group == g, _dot(wm[g], vch), 0.0)
        o_ref[rows, :] = (u[rows] * mixed).astype(bf16)


def _sg(z_sg, lng, lnb, ws, bst):
    t = z_sg.shape[0]
    tm = TOKEN_TILE
    return pl.pallas_call(
        _sg_kernel,
        out_shape=jax.ShapeDtypeStruct((t, SG_WIDTH), bf16),
        grid=(t // tm,),
        in_specs=[pl.BlockSpec((tm, W_SG), lambda i: (i, 0)),
                  _const_spec((1, SG_WIDTH)),
                  _const_spec((1, SG_WIDTH)),
                  _const_spec((SG_GROUPS, SG_CHUNK, SG_CHUNK)),
                  _const_spec((SG_CHUNK, SG_GROUPS))],
        out_specs=pl.BlockSpec((tm, SG_WIDTH), lambda i: (i, 0)),
        compiler_params=_cparams(1),
        name="spatial_gating",
    )(z_sg, lng, lnb, ws, bst)


GLA_QK = GLA_HEADS * GLA_DK
GLA_V = GLA_HEADS * GLA_DV
GLA_SUB = 16


def _gla_kernel(z_ref, wg_ref, bg_ref, no_ref, o_ref,
                st_ref, cum_ref, qd_ref, upd_ref, dl_ref, stc_ref, oraw_ref):
    tc = z_ref.shape[0]
    c_len = GLA_CHUNK
    n_chunks = tc // c_len

    @pl.when(pl.program_id(1) == 0)
    def _():
        st_ref[...] = jnp.zeros_like(st_ref)

    zg = _dot(z_ref[:, 768:896].astype(bf16), wg_ref[...]) + bg_ref[...]
    log_a = (jnp.minimum(zg, 0.0) - jnp.log1p(jnp.exp(-jnp.abs(zg)))) * (1.0 / GLA_GATE_TEMP)

    tri = (_iota((c_len, c_len), 0) >= _iota((c_len, c_len), 1)).astype(f32)
    for c in range(n_chunks):
        cum_ref[c * c_len:(c + 1) * c_len, :] = _dot_sel(tri, log_a[c * c_len:(c + 1) * c_len])

    expand = (_iota((GLA_QK, GLA_V), 0) // GLA_DK == _iota((GLA_QK, GLA_V), 1) // GLA_DV).astype(bf16)
    st_mask = (_iota((GLA_V, GLA_QK), 0) // GLA_DV == _iota((GLA_V, GLA_QK), 1) // GLA_DK).astype(f32)
    sub = GLA_SUB
    row = _iota((sub, GLA_QK), 0)

    def intra(c):
        base = c * c_len
        rows = slice(base, base + c_len)
        cum = cum_ref[rows, :]
        q = z_ref[rows, 0:128] * (GLA_DK ** -0.5)
        k = z_ref[rows, 128:256]
        v = z_ref[rows, 256:512]
        v_t = v.T.astype(bf16)
        qd_ref[rows, :] = (q * jnp.exp(cum)).astype(bf16)
        last = cum[c_len - 1:c_len, :]
        dl_ref[c] = jnp.broadcast_to(jnp.exp(last), (8, GLA_QK))
        upd_ref[c] = _dot(v_t, (k * jnp.exp(last - cum)).astype(bf16)) * st_mask

        blocks = []
        for sb in range(c_len // sub):
            r0 = sb * sub
            q_i = q[r0:r0 + sub]
            cum_i = cum[r0:r0 + sub]
            ts = []
            for jj in range(sub):
                kj = z_ref[base + r0 + jj:base + r0 + jj + 1, 128:256]
                cj = cum_ref[base + r0 + jj:base + r0 + jj + 1, :]
                dec = jnp.exp(jnp.where(row >= jj, cum_i - cj, -1e30))
                ts.append((q_i * kj * dec).astype(bf16))
            r = _dot(jnp.concatenate(ts, axis=0), expand)
            o_sb = jnp.zeros((sub, GLA_V), f32)
            for jj in range(sub):
                o_sb = o_sb + r[jj * sub:(jj + 1) * sub] * z_ref[base + r0 + jj:base + r0 + jj + 1, 256:512]
            if sb > 0:
                c_prev = cum[r0 - 1:r0, :]
                kx = k[0:r0] * jnp.exp(c_prev - cum[0:r0])
                part = _dot(v_t[:, 0:r0], kx.astype(bf16)) * st_mask
                qx = q_i * jnp.exp(cum_i - c_prev)
                o_sb = o_sb + _dot_nt(qx.astype(bf16), part.astype(bf16))
            blocks.append(o_sb)
        oraw_ref[rows, :] = jnp.concatenate(blocks, axis=0)

    for c in range(n_chunks):
        intra(c)

    def recur(c, st):
        stc_ref[c] = st.astype(bf16)
        return dl_ref[c][0:1, :] * st + upd_ref[c]

    st_ref[...] = lax.fori_loop(0, n_chunks, recur, st_ref[...])
    for c in range(n_chunks):
        rows = slice(c * c_len, (c + 1) * c_len)
        oraw_ref[rows, :] += _dot_nt(qd_ref[rows, :], stc_ref[c])

    o = oraw_ref[...]
    ms = _group_stat(o * o, _group_mean_matrix(GLA_V, GLA_DV))
    og = z_ref[:, 512:768]
    o_ref[...] = (o * lax.rsqrt(ms + NORM_EPS) * no_ref[...] * (og * _sigmoid(og))).astype(bf16)


def _gla(z_gla, wg, bg, no, batch, seq):
    tc = SCAN_TILE
    nt = seq // tc
    n_chunks = tc // GLA_CHUNK
    return pl.pallas_call(
        _gla_kernel,
        out_shape=jax.ShapeDtypeStruct((batch * seq, GLA_V), bf16),
        grid=(batch, nt),
        in_specs=[pl.BlockSpec((tc, W_GLA), lambda b, i: (b * nt + i, 0)),
                  _const_spec((128, GLA_QK)),
                  _const_spec((1, GLA_QK)),
                  _const_spec((1, GLA_V))],
        out_specs=pl.BlockSpec((tc, GLA_V), lambda b, i: (b * nt + i, 0)),
        scratch_shapes=[pltpu.VMEM((GLA_V, GLA_QK), f32),
                        pltpu.VMEM((tc, GLA_QK), f32),
                        pltpu.VMEM((tc, GLA_QK), bf16),
                        pltpu.VMEM((n_chunks, GLA_V, GLA_QK), f32),
                        pltpu.VMEM((n_chunks, 8, GLA_QK), f32),
                        pltpu.VMEM((n_chunks, GLA_V, GLA_QK), bf16),
                        pltpu.VMEM((tc, GLA_V), f32)],
        compiler_params=_cparams(2),
        name="gla",
    )(z_gla, wg, bg, no)


DN_W = DN_HEADS * DN_DK
DN_PAIRS = DN_HEADS // 2


def _stack_heads(x):
    lane = _iota(x.shape, 1)
    zero = jnp.zeros_like(x)
    return jnp.concatenate([jnp.where(lane < DN_DK, x, zero), jnp.where(lane >= DN_DK, x, zero)], axis=0)


def _dot_3x(a, b):
    a_hi = a.astype(bf16)
    b_hi = b.astype(bf16)
    a_lo = (a - a_hi.astype(f32)).astype(bf16)
    b_lo = (b - b_hi.astype(f32)).astype(bf16)
    return _dot(a_hi, b_hi) + (_dot(a_hi, b_lo) + _dot(a_lo, b_hi))


def _dn_pre_kernel(y_ref, ab_ref, alog_ref, dtb_ref,
                   a_ref, attn_ref, rhs_ref, qd_ref, kd_ref, egl_ref,
                   q_s, k_s, beta_s, g_s, gam_s, gamr_s):
    tc = y_ref.shape[0]
    c_len = DN_CHUNK

    sum_mat = _group_mean_matrix(DN_W, DN_DK) * float(DN_DK)
    q = y_ref[:, 0:DN_W]
    k = y_ref[:, DN_W:2 * DN_W]
    q_s[...] = q * lax.rsqrt(_group_stat(q * q, sum_mat) + NORM_EPS) * (DN_DK ** -0.5)
    k_s[...] = k * lax.rsqrt(_group_stat(k * k, sum_mat) + NORM_EPS)

    ab = ab_ref[...]
    lane_head = _iota((tc, DN_W), 1) // DN_DK
    a_rep = jnp.zeros((tc, DN_W), f32)
    b_rep = jnp.zeros((tc, DN_W), f32)
    for h in range(DN_HEADS):
        a_rep = jnp.where(lane_head == h, jnp.broadcast_to(ab[:, h:h + 1], (tc, DN_W)), a_rep)
        b_rep = jnp.where(lane_head == h,
                          jnp.broadcast_to(ab[:, DN_HEADS + h:DN_HEADS + h + 1], (tc, DN_W)), b_rep)
    beta_s[...] = _sigmoid(b_rep)
    g_s[...] = -jnp.exp(alog_ref[...]) * _softplus(a_rep + dtb_ref[...])

    r64 = _iota((c_len, c_len), 0)
    c64 = _iota((c_len, c_len), 1)
    tri = (r64 >= c64).astype(bf16)
    ones = jnp.ones((c_len, c_len), bf16)
    row = _iota((c_len, DN_W), 0)
    colj = _iota((c_len, DN_W), 1) % c_len
    incl = row >= colj
    strict = row > colj

    n_chunks = tc // c_len
    g_parts = _split_bf16(g_s[...], 3)
    zero_b = jnp.zeros((c_len, DN_W), bf16)
    total = lambda parts: functools.reduce(lambda a, b: a + b, reversed(parts))
    for c in range(n_chunks):
        rows = slice(c * c_len, (c + 1) * c_len)
        gam_s[rows, :] = total([_dot(tri, part[rows]) for part in g_parts])
    for c in range(n_chunks):
        rows = slice(c * c_len, (c + 1) * c_len)
        gamr_s[rows, :] = total([_dot(ones, jnp.where(row <= colj, part[rows], zero_b)) for part in g_parts])

    def chunk(c):
        rows = slice(c * c_len, (c + 1) * c_len)
        gam = gam_s[rows, :]
        gam_row = gamr_s[rows, :]
        dec_incl = jnp.exp(jnp.where(incl, gam - gam_row, -1e30))
        dec_strict = jnp.where(strict, dec_incl, 0.0)
        eg = jnp.exp(gam)
        g_last = gam[c_len - 1:c_len, :]
        q = q_s[rows, :]
        k = k_s[rows, :]
        beta = beta_s[rows, :]
        bv = beta * y_ref[rows, 2 * DN_W:3 * DN_W]
        bk = beta * eg * k
        kdec = k * jnp.exp(g_last - gam)
        qd_ref[rows, :] = (q * eg).astype(bf16)
        egl_ref[c] = jnp.broadcast_to(jnp.exp(g_last), (8, DN_W))
        qb = q.astype(bf16)
        kb = k.astype(bf16)
        for p in range(DN_PAIRS):
            ps = slice(p * 128, (p + 1) * 128)
            kbp = kb[:, ps]
            k_bd = _stack_heads(kbp)
            a_ref[c * DN_PAIRS + p] = beta[:, ps] * _dot_nt(kbp, k_bd) * dec_strict[:, ps]
            attn_ref[c * DN_PAIRS + p] = (_dot_nt(qb[:, ps], k_bd) * dec_incl[:, ps]).astype(bf16)
            kd_ref[c * DN_PAIRS + p] = _stack_heads(kdec[:, ps]).T.astype(bf16)
            rhs_ref[c * DN_PAIRS + p] = jnp.concatenate(
                [_stack_heads(bv[:, ps]), _stack_heads(bk[:, ps])], axis=1)

    for c in range(n_chunks):
        chunk(c)


def _dn_pre(z_dn, alog, dtb, batch, seq):
    tc = SCAN_TILE
    nt = seq // tc
    t = batch * seq
    npt = (tc // DN_CHUNK) * DN_PAIRS
    n_pair = (t // DN_CHUNK) * DN_PAIRS
    scr = lambda: pltpu.VMEM((tc, DN_W), f32)
    tok = lambda: pl.BlockSpec((tc, DN_W), lambda b, i: (b * nt + i, 0))
    mat = lambda r, w: pl.BlockSpec((npt, r, w), lambda b, i: (b * nt + i, 0, 0))
    return pl.pallas_call(
        _dn_pre_kernel,
        out_shape=(jax.ShapeDtypeStruct((n_pair, DN_CHUNK, 128), f32),
                   jax.ShapeDtypeStruct((n_pair, DN_CHUNK, 128), bf16),
                   jax.ShapeDtypeStruct((n_pair, 128, 256), f32),
                   jax.ShapeDtypeStruct((t, DN_W), bf16),
                   jax.ShapeDtypeStruct((n_pair, 128, 128), bf16),
                   jax.ShapeDtypeStruct((t // DN_CHUNK, 8, DN_W), f32)),
        grid=(batch, nt),
        in_specs=[pl.BlockSpec((tc, 3 * DN_W), lambda b, i: (b * nt + i, 0)),
                  pl.BlockSpec((tc, 128), lambda b, i: (b * nt + i, 4 * DN_W // 128)),
                  _const_spec((1, DN_W)),
                  _const_spec((1, DN_W))],
        out_specs=(mat(DN_CHUNK, 128), mat(DN_CHUNK, 128), mat(128, 256), tok(), mat(128, 128),
                   pl.BlockSpec((tc // DN_CHUNK, 8, DN_W), lambda b, i: (b * nt + i, 0, 0))),
        scratch_shapes=[scr(), scr(), scr(), scr(), scr(), scr()],
        compiler_params=_cparams(2),
        name="dn_pre",
    )(z_dn, z_dn, alog, dtb)


def _dn_solve_kernel(a_ref, t_ref):
    c_len = DN_CHUNK
    blk = DN_INV_BLOCK
    vshape = a_ref.shape[1:]

    def row_i(i, carry):
        for mb in range(c_len // blk):
            m0 = mb * blk

            def col_j(j, acc):
                a_ij = a_ref[i * c_len + j]
                return tuple(acc[m] - a_ij * t_ref[j * c_len + m0 + m] for m in range(blk))

            acc = lax.fori_loop(m0, i, col_j, tuple(jnp.zeros(vshape, f32) for _ in range(blk)))
            for m in range(blk):
                t_ref[i * c_len + m0 + m] = acc[m] + jnp.where(i == m0 + m, 1.0, 0.0)
        return carry

    lax.fori_loop(0, c_len, row_i, 0)


def _dn_solve(a_t):
    n_ent, n_grp, lanes = a_t.shape
    g = min(n_grp, 8)
    spec = lambda: pl.BlockSpec((n_ent, g, lanes), lambda i: (0, i, 0), pipeline_mode=pl.Buffered(1))
    return pl.pallas_call(
        _dn_solve_kernel,
        out_shape=jax.ShapeDtypeStruct(a_t.shape, f32),
        grid=(n_grp // g,),
        in_specs=[spec()],
        out_specs=spec(),
        compiler_params=_cparams(1),
        name="dn_solve",
    )(a_t)


def _dn_scan_kernel(t_ref, attn_ref, rhs_ref, qd_ref, kd_ref, egl_ref, zg_ref, no_ref,
                    o_ref, st_ref, oraw_ref):
    tc = qd_ref.shape[0]
    c_len = DN_CHUNK
    pairs = range(DN_PAIRS)
    psl = [slice(p * 128, (p + 1) * 128) for p in pairs]

    @pl.when(pl.program_id(1) == 0)
    def _():
        st_ref[...] = jnp.zeros_like(st_ref)

    def solve(c):
        return [_dot_3x(t_ref[c * DN_PAIRS + p], rhs_ref[c * DN_PAIRS + p]) for p in pairs]

    n_chunks = tc // c_len
    sols = solve(0)
    for c in range(n_chunks):
        rows = slice(c * c_len, (c + 1) * c_len)
        qd = qd_ref[rows, :]
        eg_last = egl_ref[c][0:1, :]
        sts = [st_ref[p] for p in pairs]
        xq = [_dot(jnp.concatenate([sols[p][:, 128:].astype(bf16), qd[:, psl[p]]], axis=0),
                   sts[p].astype(bf16)) for p in pairs]
        nxt = solve(c + 1) if c + 1 < n_chunks else None
        u_bd = [_stack_heads((sols[p][:, :128] - xq[p][:c_len]).astype(bf16)) for p in pairs]
        au = [_dot(jnp.concatenate([attn_ref[c * DN_PAIRS + p], kd_ref[c * DN_PAIRS + p]], axis=0), u_bd[p])
              for p in pairs]
        for p in pairs:
            st_ref[p] = eg_last[:, psl[p]] * sts[p] + au[p][c_len:]
        oraw_ref[rows, :] = jnp.concatenate([xq[p][c_len:] + au[p][:c_len] for p in pairs], axis=-1)
        sols = nxt

    o = oraw_ref[...]
    ms = _group_stat(o * o, _group_mean_matrix(DN_W, DN_DV))
    zg = zg_ref[...]
    o_ref[...] = (o * lax.rsqrt(ms + NORM_EPS) * no_ref[...] * (zg * _sigmoid(zg))).astype(bf16)


def _dn_scan(t_pairs, attn, rhs, qd, kd, egl, z_dn, no, batch, seq):
    tc = SCAN_TILE
    nt = seq // tc
    npt = (tc // DN_CHUNK) * DN_PAIRS
    tok = lambda: pl.BlockSpec((tc, DN_W), lambda b, i: (b * nt + i, 0))
    mat = lambda r, w: pl.BlockSpec((npt, r, w), lambda b, i: (b * nt + i, 0, 0))
    return pl.pallas_call(
        _dn_scan_kernel,
        out_shape=jax.ShapeDtypeStruct((batch * seq, DN_W), bf16),
        grid=(batch, nt),
        in_specs=[mat(DN_CHUNK, 128), mat(DN_CHUNK, 128), mat(128, 256), tok(), mat(128, 128),
                  pl.BlockSpec((tc // DN_CHUNK, 8, DN_W), lambda b, i: (b * nt + i, 0, 0)),
                  pl.BlockSpec((tc, DN_W), lambda b, i: (b * nt + i, 3)),
                  _const_spec((1, DN_W))],
        out_specs=tok(),
        scratch_shapes=[pltpu.VMEM((DN_PAIRS, 128, 128), f32),
                        pltpu.VMEM((tc, DN_W), f32)],
        compiler_params=_cparams(2),
        name="dn_scan",
    )(t_pairs, attn, rhs, qd, kd, egl, z_dn, no)


def _dn(z_dn, alog, dtb, no, batch, seq):
    a_pairs, attn, rhs, qd, kd, egl = _dn_pre(z_dn, alog, dtb, batch, seq)
    n_pair = a_pairs.shape[0]
    n_prob = n_pair * 2
    c = DN_CHUNK
    a_t = a_pairs.reshape(n_pair, c, 2, c).transpose(1, 3, 0, 2).reshape(c * c, n_prob // 128, 128)
    t_t = _dn_solve(a_t)
    t_pairs = t_t.reshape(c, c, n_pair, 2).transpose(2, 0, 3, 1).reshape(n_pair, c, 2 * c)
    return _dn_scan(t_pairs, attn, rhs, qd, kd, egl, z_dn, no, batch, seq)


def _merge_kernel(x_ref, g_ref, oa_ref, ob_ref, oc_ref, od_ref, wg_ref, wb_ref, wo_ref, out_ref):
    x = x_ref[...]
    h = _rms(x, g_ref[...]).astype(bf16)
    merged = jnp.zeros(x.shape, f32)
    for n, o_ref in enumerate((oa_ref, ob_ref, oc_ref, od_ref)):
        gate = _sigmoid(_dot(h, wg_ref[:, n * D_MODEL:(n + 1) * D_MODEL]))
        merged = merged + gate * _dot(o_ref[...], wb_ref[n])
    out_ref[...] = x + _dot(merged.astype(bf16), wo_ref[...])


def _merge(x2, g, oa, ob, oc, od, wg, wb, wo):
    t = x2.shape[0]
    tm = TOKEN_TILE
    tok = lambda w: pl.BlockSpec((tm, w), lambda i: (i, 0))
    return pl.pallas_call(
        _merge_kernel,
        out_shape=jax.ShapeDtypeStruct((t, D_MODEL), f32),
        grid=(t // tm,),
        in_specs=[tok(D_MODEL), _const_spec((1, D_MODEL)),
                  tok(BRANCH_WIDTH), tok(BRANCH_WIDTH), tok(BRANCH_WIDTH), tok(BRANCH_WIDTH),
                  _const_spec((D_MODEL, N_BRANCH * D_MODEL)),
                  _const_spec((N_BRANCH, BRANCH_WIDTH, D_MODEL)),
                  _const_spec((D_MODEL, D_MODEL))],
        out_specs=tok(D_MODEL),
        compiler_params=_cparams(1),
        name="merge",
    )(x2, g, oa, ob, oc, od, wg, wb, wo)


def _memkv_kernel(mem_ref, g_ref, wk_ref, wv_ref, k_ref, v_ref):
    mn = _rms(mem_ref[...], g_ref[...]).astype(bf16)
    k_ref[...] = _dot(mn, wk_ref[...]).astype(bf16)
    v_ref[...] = _dot(mn, wv_ref[...]).astype(bf16)


def _memkv(mem2, g, wk, wv, mem_len):
    t = mem2.shape[0]
    blk = lambda: pl.BlockSpec((mem_len, D_MODEL), lambda i: (i, 0))
    return pl.pallas_call(
        _memkv_kernel,
        out_shape=(jax.ShapeDtypeStruct((t, D_MODEL), bf16), jax.ShapeDtypeStruct((t, D_MODEL), bf16)),
        grid=(t // mem_len,),
        in_specs=[blk(), _const_spec((1, D_MODEL)),
                  _const_spec((D_MODEL, D_MODEL)), _const_spec((D_MODEL, D_MODEL))],
        out_specs=(blk(), blk()),
        compiler_params=_cparams(1),
        name="mem_kv",
    )(mem2, g, wk, wv)


def _xattn_kernel(x_ref, g_ref, k_ref, v_ref, wq_ref, wo_ref, out_ref):
    x = x_ref[...]
    h = _rms(x, g_ref[...]).astype(bf16)
    q = (_dot(h, wq_ref[...]) * (XATTN_DIM ** -0.5)).astype(bf16)
    hsl = [slice(hh * XATTN_DIM, (hh + 1) * XATTN_DIM) for hh in range(XATTN_HEADS)]
    scores = [_dot_nt(q[:, sl], k_ref[:, sl]) for sl in hsl]
    probs = []
    for s in scores:
        p = jnp.exp(s - jnp.max(s, axis=-1, keepdims=True))
        probs.append((p / jnp.sum(p, axis=-1, keepdims=True)).astype(bf16))
    o = jnp.concatenate([_dot(p, v_ref[:, sl]).astype(bf16) for p, sl in zip(probs, hsl)], axis=-1)
    out_ref[...] = x + _dot(o, wo_ref[...])


def _xattn(x2, g, kmem, vmem, wq, wo, batch, seq, mem_len):
    tm = TOKEN_TILE
    nt = seq // tm
    tok = pl.BlockSpec((tm, D_MODEL), lambda b, i: (b * nt + i, 0))
    memb = lambda: pl.BlockSpec((mem_len, D_MODEL), lambda b, i: (b, 0))
    return pl.pallas_call(
        _xattn_kernel,
        out_shape=jax.ShapeDtypeStruct((batch * seq, D_MODEL), f32),
        grid=(batch, nt),
        in_specs=[tok, _const_spec((1, D_MODEL)), memb(), memb(),
                  _const_spec((D_MODEL, D_MODEL)), _const_spec((D_MODEL, D_MODEL))],
        out_specs=tok,
        compiler_params=_cparams(2),
        name="xattn",
    )(x2, g, kmem, vmem, wq, wo)


def _mlp_kernel(x_ref, g_ref, wu_ref, wd_ref, gf_ref, out_ref, *, final):
    x = x_ref[...]
    h = _rms(x, g_ref[...]).astype(bf16)
    acc = x
    for c in range(MLP_HIDDEN // MLP_CHUNK):
        sl = slice(c * MLP_CHUNK, (c + 1) * MLP_CHUNK)
        a = jnp.maximum(_dot(h, wu_ref[:, sl]), 0.0)
        acc = acc + _dot((a * a).astype(bf16), wd_ref[sl, :])
    if final:
        acc = _rms(acc, gf_ref[...])
    out_ref[...] = acc


def _mlp(x2, g, wu, wd, gf, final):
    t = x2.shape[0]
    tm = TOKEN_TILE
    tok = pl.BlockSpec((tm, D_MODEL), lambda i: (i, 0))
    return pl.pallas_call(
        functools.partial(_mlp_kernel, final=final),
        out_shape=jax.ShapeDtypeStruct((t, D_MODEL), f32),
        grid=(t // tm,),
        in_specs=[tok, _const_spec((1, D_MODEL)),
                  _const_spec((D_MODEL, MLP_HIDDEN)), _const_spec((MLP_HIDDEN, D_MODEL)),
                  _const_spec((1, D_MODEL))],
        out_specs=tok,
        compiler_params=_cparams(1),
        name="mlp",
    )(x2, g, wu, wd, gf)


def _pad_cols(w, width):
    return jnp.pad(w, ((0, 0), (0, width - w.shape[1])))


def _prep_w_in(w_in):
    col = lambda i: w_in[:, IN_OFFS[i]:IN_OFFS[i + 1]]
    zeros = lambda n: jnp.zeros((D_MODEL, n), w_in.dtype)
    (cq, ckv, kpe, sg_uv, gq, gk, gv, g_lr, g_og, dq, dk, dv, da, db, dz) = (col(i) for i in range(15))
    small = jnp.concatenate([
        cq, ckv, zeros(MLA_NOPE), kpe, zeros(MLA_HEAD_PAD - MLA_NOPE - MLA_ROPE),
        sg_uv,
        gq, gk, gv, g_og, _pad_cols(g_lr, 128),
        dq, dk, dv, dz, _pad_cols(jnp.concatenate([da, db], axis=1), 128),
    ], axis=1)
    assert small.shape[1] == W_SMALL
    return small.astype(bf16), col(15).astype(bf16)


def _rope_inv_freq():
    inv_freq = ROPE_THETA ** (-jnp.arange(0, MLA_ROPE, 2, dtype=f32) / MLA_ROPE)
    return inv_freq[:, None]


def kernel(x, mem, positions, norm_mix, w_in, mla_norm_q, mla_norm_kv, mla_w_uq, mla_w_ukv, sg_ln_g, sg_ln_b, sg_w, sg_b, gla_w_gate, gla_b_gate, gla_norm, dn_conv, dn_a_log, dn_dt_bias, dn_norm, w_branch, w_out, norm_xattn, norm_mem, xattn_wq, xattn_wk, xattn_wv, xattn_wo, norm_mlp, w_up, w_down, norm_final):
    batch, seq, _ = x.shape
    mem_len = mem.shape[1]
    depth = w_in.shape[0]
    assert seq % TOKEN_TILE == 0 and seq % SCAN_TILE == 0 and seq % ATTN_TILE == 0
    x2 = x.reshape(batch * seq, D_MODEL)
    mem2 = mem.reshape(batch * mem_len, D_MODEL)
    pos2 = positions.reshape(batch * seq // TOKEN_TILE, 1, TOKEN_TILE)
    invf = _rope_inv_freq()
    row = lambda v: v[None, :]

    for l in range(depth):
        w_small, w_gates = _prep_w_in(w_in[l])
        z_mla, z_sg, z_gla, z_dn = _inproj(x2, row(norm_mix[l]), w_small, dn_conv[l], seq)

        wuq = jnp.pad(mla_w_uq[l].reshape(MLA_Q_RANK, MLA_HEADS, MLA_NOPE + MLA_ROPE),
                      ((0, 0), (0, 0), (0, MLA_HEAD_PAD - MLA_NOPE - MLA_ROPE)))
        wuq = wuq.reshape(MLA_Q_RANK, MLA_HEADS * MLA_HEAD_PAD).astype(bf16)
        wukv = mla_w_ukv[l].reshape(MLA_KV_RANK, MLA_HEADS, MLA_NOPE + MLA_V)
        wuk = jnp.pad(wukv[:, :, :MLA_NOPE], ((0, 0), (0, 0), (0, MLA_HEAD_PAD - MLA_NOPE)))
        wuk = wuk.reshape(MLA_KV_RANK, MLA_HEADS * MLA_HEAD_PAD).astype(bf16)
        wuv = wukv[:, :, MLA_NOPE:].reshape(MLA_KV_RANK, MLA_HEADS * MLA_V).astype(bf16)
        q_a, k_a, v_a = _mla_pre(z_mla, pos2, invf, row(mla_norm_q[l]), row(mla_norm_kv[l]), wuq, wuk, wuv,
                                 batch, seq)
        o_a = _mla_attn(q_a, k_a, v_a, batch, seq)

        o_b = _sg(z_sg, row(sg_ln_g[l]), row(sg_ln_b[l]), sg_w[l], sg_b[l].T)

        wg = jnp.pad(gla_w_gate[l], ((0, 128 - GLA_GATE_RANK), (0, 0))).astype(bf16)
        o_c = _gla(z_gla, wg, row(gla_b_gate[l]), row(jnp.tile(gla_norm[l], GLA_HEADS)), batch, seq)

        o_d = _dn(z_dn, row(jnp.repeat(dn_a_log[l], DN_DK)), row(jnp.repeat(dn_dt_bias[l], DN_DK)),
                  row(jnp.tile(dn_norm[l], DN_HEADS)), batch, seq)

        x2 = _merge(x2, row(norm_mix[l]), o_a, o_b, o_c, o_d, w_gates,
                    w_branch[l].astype(bf16), w_out[l].astype(bf16))

        k_m, v_m = _memkv(mem2, row(norm_mem[l]), xattn_wk[l].astype(bf16), xattn_wv[l].astype(bf16), mem_len)
        x2 = _xattn(x2, row(norm_xattn[l]), k_m, v_m, xattn_wq[l].astype(bf16), xattn_wo[l].astype(bf16),
                    batch, seq, mem_len)
        x2 = _mlp(x2, row(norm_mlp[l]), w_up[l].astype(bf16), w_down[l].astype(bf16), row(norm_final),
                  final=(l == depth - 1))
    return x2.reshape(batch, seq, D_MODEL)
```

```python
import functools

import jax
import jax.numpy as jnp
import numpy as np
from jax import lax
from jax.experimental import pallas as pl
from jax.experimental.pallas import tpu as pltpu

f32 = jnp.float32
bf16 = jnp.bfloat16
HIGHEST = lax.Precision.HIGHEST

D_MODEL = 1024
NORM_EPS = 1e-6
N_BRANCH = 4
BRANCH_WIDTH = 256

MLA_HEADS = 4
MLA_Q_RANK = 256
MLA_KV_RANK = 128
MLA_NOPE = 64
MLA_ROPE = 32
MLA_V = 64
ROPE_THETA = 10000.0
MLA_HEAD_PAD = 128
MLA_VT_ROWS = 80

SG_GROUPS = 4
SG_WIDTH = 256
SG_CHUNK = 128

GLA_HEADS = 4
GLA_DK = 32
GLA_DV = 64
GLA_GATE_RANK = 16
GLA_GATE_TEMP = 16.0
GLA_CHUNK = 64

DN_HEADS = 4
DN_DK = 64
DN_DV = 64
DN_CONV = 4
DN_CHUNK = 64
DN_INV_BLOCK = 16

XATTN_HEADS = 4
XATTN_DIM = D_MODEL // XATTN_HEADS
MLP_HIDDEN = 4 * D_MODEL

IN_SIZES = (
    MLA_Q_RANK, MLA_KV_RANK, MLA_ROPE,
    2 * SG_WIDTH,
    GLA_HEADS * GLA_DK, GLA_HEADS * GLA_DK, GLA_HEADS * GLA_DV,
    GLA_GATE_RANK, GLA_HEADS * GLA_DV,
    DN_HEADS * DN_DK, DN_HEADS * DN_DK, DN_HEADS * DN_DV,
    DN_HEADS, DN_HEADS, DN_HEADS * DN_DV,
    N_BRANCH * D_MODEL,
)
IN_OFFS = tuple(int(v) for v in np.cumsum((0,) + IN_SIZES))

W_MLA = 512
W_SG = 512
W_GLA = 896
W_DN = 1152
W_SMALL = W_MLA + W_SG + W_GLA + W_DN

TOKEN_TILE = 512
ATTN_TILE = 256
ATTN_KEY_TILE = 128
SCAN_TILE = 512
MLP_CHUNK = 1024
VMEM_LIMIT = 56 * 1024 * 1024


def _cparams(n_axes):
    return pltpu.CompilerParams(dimension_semantics=("arbitrary",) * n_axes,
                                vmem_limit_bytes=VMEM_LIMIT)


def _const_spec(shape):
    nd = len(shape)
    return pl.BlockSpec(shape, lambda *_: (0,) * nd, pipeline_mode=pl.Buffered(1))


def _rms(x, g):
    return x * lax.rsqrt(jnp.mean(x * x, axis=-1, keepdims=True) + NORM_EPS) * g


def _sigmoid(x):
    return 1.0 / (1.0 + jnp.exp(-x))


def _softplus(x):
    return jnp.maximum(x, 0.0) + jnp.log1p(jnp.exp(-jnp.abs(x)))


def _dot(a, b):
    return jnp.dot(a, b, preferred_element_type=f32)


def _split_bf16(x, terms):
    parts = []
    for _ in range(terms):
        p = x.astype(bf16)
        parts.append(p)
        x = x - p.astype(f32)
    return parts


def _dot_sel(sel, x, terms=3):
    sel = sel.astype(bf16)
    parts = [_dot(sel, p) for p in _split_bf16(x, terms)]
    return functools.reduce(lambda a, b: a + b, reversed(parts))


def _group_stat(x, mat, terms=2):
    mat = mat.astype(bf16)
    parts = [_dot(p, mat) for p in _split_bf16(x, terms)]
    return functools.reduce(lambda a, b: a + b, reversed(parts))


def _dot_nt(a, b):
    return lax.dot_general(a, b, (((1,), (1,)), ((), ())), preferred_element_type=f32)


def _dot_tn(a, b):
    return lax.dot_general(a, b, (((0,), (0,)), ((), ())), preferred_element_type=f32)


def _iota(shape, dim):
    return lax.broadcasted_iota(jnp.int32, shape, dim)


def _group_mean_matrix(width, group):
    r = _iota((width, width), 0) // group
    c = _iota((width, width), 1) // group
    return jnp.where(r == c, 1.0 / group, 0.0).astype(f32)


DN_CONV_TAPS = 4
DN_CONV_COLS = 768
DN_CONV_PAD = 8


def _inproj_kernel(x_ref, g_ref, w_ref, cw_ref, mla_ref, sg_ref, gla_ref, dn_ref, xbuf_ref, *,
                   tiles_per_seq):
    @pl.when(pl.program_id(0) % tiles_per_seq == 0)
    def _():
        xbuf_ref[0:DN_CONV_PAD, :] = jnp.zeros((DN_CONV_PAD, DN_CONV_COLS), f32)

    h = _rms(x_ref[...], g_ref[...]).astype(bf16)
    tm = h.shape[0]
    z_dn = _dot(h, w_ref[:, W_SMALL - W_DN:])
    dn_ref[:, DN_CONV_COLS:] = z_dn[:, DN_CONV_COLS:]
    xbuf_ref[DN_CONV_PAD:DN_CONV_PAD + tm, :] = z_dn[:, :DN_CONV_COLS]
    y = jnp.zeros((tm, DN_CONV_COLS), f32)
    for kk in range(DN_CONV_TAPS):
        y = y + cw_ref[kk:kk + 1, :] * xbuf_ref[pl.ds(DN_CONV_PAD - (DN_CONV_TAPS - 1) + kk, tm), :]
    xbuf_ref[0:DN_CONV_PAD, :] = xbuf_ref[tm:tm + DN_CONV_PAD, :]
    dn_ref[:, :DN_CONV_COLS] = y * _sigmoid(y)
    lo = 0
    for ref, width in ((mla_ref, W_MLA), (sg_ref, W_SG), (gla_ref, W_GLA)):
        ref[...] = _dot(h, w_ref[:, lo:lo + width])
        lo += width


def _inproj(x2, g, w_small, conv_w, seq):
    t = x2.shape[0]
    tm = TOKEN_TILE
    outs = tuple(jax.ShapeDtypeStruct((t, w), f32) for w in (W_MLA, W_SG, W_GLA, W_DN))
    return pl.pallas_call(
        functools.partial(_inproj_kernel, tiles_per_seq=seq // tm),
        out_shape=outs,
        grid=(t // tm,),
        in_specs=[pl.BlockSpec((tm, D_MODEL), lambda i: (i, 0)),
                  _const_spec((1, D_MODEL)),
                  _const_spec((D_MODEL, W_SMALL)),
                  _const_spec((DN_CONV_TAPS, DN_CONV_COLS))],
        out_specs=tuple(pl.BlockSpec((tm, w), lambda i: (i, 0)) for w in (W_MLA, W_SG, W_GLA, W_DN)),
        scratch_shapes=[pltpu.VMEM((tm + 2 * DN_CONV_PAD, DN_CONV_COLS), f32)],
        compiler_params=_cparams(1),
        name="inproj",
    )(x2, g, w_small, conv_w)


def _mla_pre_kernel(z_ref, pos_ref, invf_ref, nq_ref, nkv_ref, wuq_ref, wuk_ref, wuv_ref,
                    qt_ref, k_ref, vt_ref):
    z = z_ref[...]
    tm = z.shape[0]
    cq = z[:, :MLA_Q_RANK]
    ckv = z[:, MLA_Q_RANK:MLA_Q_RANK + MLA_KV_RANK]
    kpe = z[:, MLA_Q_RANK + MLA_KV_RANK:]

    half = MLA_ROPE // 2
    ang_t = invf_ref[...] * pos_ref[...].astype(f32)
    f_row = _iota((half, MLA_HEAD_PAD), 0)
    f_lane = _iota((half, MLA_HEAD_PAD), 1)
    place = ((f_lane == MLA_NOPE + f_row) | (f_lane == MLA_NOPE + half + f_row)).astype(bf16)

    def to_lanes(x_t):
        parts = [_dot_tn(p, place) for p in _split_bf16(x_t, 3)]
        return functools.reduce(lambda a, b: a + b, reversed(parts))

    cos = to_lanes(jnp.cos(ang_t))
    sin = to_lanes(jnp.sin(ang_t))
    lane = _iota((tm, MLA_HEAD_PAD), 1)
    in_x1 = (lane >= MLA_NOPE) & (lane < MLA_NOPE + half)
    in_x2 = (lane >= MLA_NOPE + half) & (lane < MLA_NOPE + MLA_ROPE)
    c_tab = jnp.where(in_x1 | in_x2, cos, 1.0)
    s1_tab = jnp.where(in_x1, -sin, 0.0)
    s2_tab = jnp.where(in_x2, sin, 0.0)

    def rope(t):
        return (t * c_tab + pltpu.roll(t, MLA_HEAD_PAD - half, 1) * s1_tab
                + pltpu.roll(t, half, 1) * s2_tab)

    scale = (MLA_NOPE + MLA_ROPE) ** -0.5
    q = _dot(_rms(cq, nq_ref[...]).astype(bf16), wuq_ref[...])
    kvn = _rms(ckv, nkv_ref[...]).astype(bf16)
    kn = _dot(kvn, wuk_ref[...])
    v_t = _dot(kvn, wuv_ref[...]).T
    kpe_rot = rope(kpe)
    q_rot = []
    for h in range(MLA_HEADS):
        sl = slice(h * MLA_HEAD_PAD, (h + 1) * MLA_HEAD_PAD)
        q_rot.append(rope(q[:, sl]) * scale)
        k_ref[:, sl] = (kn[:, sl] + kpe_rot).astype(bf16)
    qt_ref[...] = jnp.concatenate(q_rot, axis=-1).T.astype(bf16)
    pad = jnp.concatenate([jnp.ones((1, tm), f32), jnp.zeros((MLA_VT_ROWS - MLA_V - 1, tm), f32)], axis=0)
    rows = []
    for h in range(MLA_HEADS):
        rows += [v_t[h * MLA_V:(h + 1) * MLA_V], pad]
    vt_ref[...] = jnp.concatenate(rows, axis=0).astype(bf16)


def _mla_pre(z_mla, pos2, invf, nq, nkv, wuq, wuk, wuv, batch, seq):
    t = z_mla.shape[0]
    tm = TOKEN_TILE
    nt = seq // tm
    hp = MLA_HEADS * MLA_HEAD_PAD
    hv = MLA_HEADS * MLA_V
    hvt = MLA_HEADS * MLA_VT_ROWS
    return pl.pallas_call(
        _mla_pre_kernel,
        out_shape=(jax.ShapeDtypeStruct((batch, hp, seq), bf16),
                   jax.ShapeDtypeStruct((t, hp), bf16),
                   jax.ShapeDtypeStruct((batch, hvt, seq), bf16)),
        grid=(t // tm,),
        in_specs=[pl.BlockSpec((tm, W_MLA), lambda i: (i, 0)),
                  pl.BlockSpec((None, 1, tm), lambda i: (i, 0, 0)),
                  _const_spec((MLA_ROPE // 2, 1)),
                  _const_spec((1, MLA_Q_RANK)),
                  _const_spec((1, MLA_KV_RANK)),
                  _const_spec((MLA_Q_RANK, hp)),
                  _const_spec((MLA_KV_RANK, hp)),
                  _const_spec((MLA_KV_RANK, hv))],
        out_specs=(pl.BlockSpec((None, hp, tm), lambda i: (i // nt, 0, i % nt)),
                   pl.BlockSpec((tm, hp), lambda i: (i, 0)),
                   pl.BlockSpec((None, hvt, tm), lambda i: (i // nt, 0, i % nt))),
        compiler_params=_cparams(1),
        name="mla_pre",
    )(z_mla, pos2, invf, nq, nkv, wuq, wuk, wuv)


def _mla_attn_kernel(qt_ref, k_ref, vt_ref, o_ref, sa_ref, sb_ref):
    i = pl.program_id(1)
    tq = qt_ref.shape[1]
    tk = sa_ref.shape[1]
    per = tq // tk
    key = _iota((tk, tq), 0)
    qry = _iota((tk, tq), 1)

    def issue_scores(b, buf):
        start = pl.multiple_of(b * tk, tk)
        cms = []
        for h in range(MLA_HEADS):
            hsl = slice(h * MLA_HEAD_PAD, (h + 1) * MLA_HEAD_PAD)
            s = _dot(k_ref[pl.ds(start, tk), hsl], qt_ref[hsl, :])
            buf[h] = s
            cms.append(jnp.max(s, axis=0, keepdims=True))
        return tuple(cms)

    def consume(b, buf, cms, state, diag_offset=None):
        start = pl.multiple_of(b * tk, tk)
        probs = []
        for h in range(MLA_HEADS):
            s = buf[h]
            if diag_offset is None:
                cm = cms[h]
            else:
                s = jnp.where(key + diag_offset <= qry, s, -jnp.inf)
                cm = jnp.max(s, axis=0, keepdims=True)
            m_new = jnp.maximum(state[h][0], cm)
            probs.append((m_new, jnp.exp(s - m_new).astype(bf16)))
        new = []
        for h in range(MLA_HEADS):
            vsl = slice(h * MLA_VT_ROWS, (h + 1) * MLA_VT_ROWS)
            m, acc = state[h]
            m_new, p = probs[h]
            acc = jnp.exp(m - m_new) * acc + _dot(vt_ref[vsl, pl.ds(start, tk)], p)
            new.append((m_new, acc))
        return tuple(new)

    def body(t, carry):
        state, cm_a = carry
        cm_b = issue_scores(per * t + 1, sb_ref)
        state = consume(per * t, sa_ref, cm_a, state)
        cm_a = issue_scores(per * t + 2, sa_ref)
        state = consume(per * t + 1, sb_ref, cm_b, state)
        return state, cm_a

    assert per == 2
    init = tuple((jnp.full((1, tq), -jnp.inf, f32), jnp.zeros((MLA_VT_ROWS, tq), f32))
                 for _ in range(MLA_HEADS))
    carry = (init, issue_scores(0, sa_ref))
    carry = lax.fori_loop(0, i // 2, lambda u, c: body(2 * u + 1, body(2 * u, c)), carry)
    state, _ = lax.fori_loop(0, i % 2, lambda _, c: body(i - 1, c), carry)
    issue_scores(per * i + 1, sb_ref)
    state = consume(per * i, sa_ref, None, state, diag_offset=0)
    state = consume(per * i + 1, sb_ref, None, state, diag_offset=tk)
    o_t = jnp.concatenate([acc[:MLA_V] / acc[MLA_V:MLA_V + 1] for _, acc in state], axis=0)
    o_ref[...] = o_t.T.astype(bf16)


def _mla_attn(q_t, k, v_t, batch, seq):
    tq = ATTN_TILE
    nq = seq // tq
    hp = MLA_HEADS * MLA_HEAD_PAD
    hv = MLA_HEADS * MLA_V
    hvt = MLA_HEADS * MLA_VT_ROWS
    return pl.pallas_call(
        _mla_attn_kernel,
        out_shape=jax.ShapeDtypeStruct((batch * seq, hv), bf16),
        grid=(batch, nq),
        in_specs=[pl.BlockSpec((None, hp, tq), lambda b, i: (b, 0, i)),
                  pl.BlockSpec((seq, hp), lambda b, i: (b, 0)),
                  pl.BlockSpec((None, hvt, seq), lambda b, i: (b, 0, 0))],
        out_specs=pl.BlockSpec((tq, hv), lambda b, i: (b * nq + i, 0)),
        scratch_shapes=[pltpu.VMEM((MLA_HEADS, ATTN_KEY_TILE, tq), f32),
                        pltpu.VMEM((MLA_HEADS, ATTN_KEY_TILE, tq), f32)],
        compiler_params=_cparams(2),
        name="mla_attn",
    )(q_t, k, v_t)


def _sg_kernel(z_ref, lng_ref, lnb_ref, ws_ref, bst_ref, o_ref):
    z = z_ref[...]
    tm = z.shape[0]
    uv = 0.5 * z * (1.0 + jnp.tanh(np.sqrt(2.0 / np.pi) * (z + 0.044715 * (z * z * z))))
    u = uv[:, :SG_WIDTH]
    v = uv[:, SG_WIDTH:]
    mu = jnp.mean(v, axis=-1, keepdims=True)
    vc = v - mu
    vn = vc * lax.rsqrt(jnp.mean(vc * vc, axis=-1, keepdims=True) + NORM_EPS)
    vn = (vn * lng_ref[...] + lnb_ref[...]).astype(bf16)

    gw = SG_WIDTH // SG_GROUPS
    lane_group = _iota((SG_CHUNK, SG_WIDTH), 1) // gw
    tril = _iota((SG_CHUNK, SG_CHUNK), 0) >= _iota((SG_CHUNK, SG_CHUNK), 1)
    bias = jnp.zeros((SG_CHUNK, SG_WIDTH), f32)
    wm = []
    for g in range(SG_GROUPS):
        wm.append(jnp.where(tril, ws_ref[g], 0.0).astype(bf16))
        bias = jnp.where(lane_group == g,
                         jnp.broadcast_to(bst_ref[:, g:g + 1], (SG_CHUNK, SG_WIDTH)), bias)
    for c in range(tm // SG_CHUNK):
        rows = slice(c * SG_CHUNK, (c + 1) * SG_CHUNK)
        vch = vn[rows]
        mixed = bias
        for g in range(SG_GROUPS):
            mixed = mixed + jnp.where(lane_group == g, _dot(wm[g], vch), 0.0)
        o_ref[rows, :] = (u[rows] * mixed).astype(bf16)


def _sg(z_sg, lng, lnb, ws, bst):
    t = z_sg.shape[0]
    tm = TOKEN_TILE
    return pl.pallas_call(
        _sg_kernel,
        out_shape=jax.ShapeDtypeStruct((t, SG_WIDTH), bf16),
        grid=(t // tm,),
        in_specs=[pl.BlockSpec((tm, W_SG), lambda i: (i, 0)),
                  _const_spec((1, SG_WIDTH)),
                  _const_spec((1, SG_WIDTH)),
                  _const_spec((SG_GROUPS, SG_CHUNK, SG_CHUNK)),
                  _const_spec((SG_CHUNK, SG_GROUPS))],
        out_specs=pl.BlockSpec((tm, SG_WIDTH), lambda i: (i, 0)),
        compiler_params=_cparams(1),
        name="spatial_gating",
    )(z_sg, lng, lnb, ws, bst)


GLA_QK = GLA_HEADS * GLA_DK
GLA_V = GLA_HEADS * GLA_DV
GLA_SUB = 16


def _gla_kernel(z_ref, wg_ref, bg_ref, no_ref, o_ref,
                st_ref, cum_ref, qd_ref, upd_ref, dl_ref, stc_ref, oraw_ref):
    tc = z_ref.shape[0]
    c_len = GLA_CHUNK
    n_chunks = tc // c_len

    @pl.when(pl.program_id(1) == 0)
    def _():
        st_ref[...] = jnp.zeros_like(st_ref)

    zg = _dot(z_ref[:, 768:896].astype(bf16), wg_ref[...]) + bg_ref[...]
    log_a = (jnp.minimum(zg, 0.0) - jnp.log1p(jnp.exp(-jnp.abs(zg)))) * (1.0 / GLA_GATE_TEMP)

    tri = (_iota((c_len, c_len), 0) >= _iota((c_len, c_len), 1)).astype(f32)
    for c in range(n_chunks):
        cum_ref[c * c_len:(c + 1) * c_len, :] = _dot_sel(tri, log_a[c * c_len:(c + 1) * c_len])

    expand = (_iota((GLA_QK, GLA_V), 0) // GLA_DK == _iota((GLA_QK, GLA_V), 1) // GLA_DV).astype(bf16)
    st_mask = (_iota((GLA_V, GLA_QK), 0) // GLA_DV == _iota((GLA_V, GLA_QK), 1) // GLA_DK).astype(f32)
    sub = GLA_SUB
    row = _iota((sub, GLA_QK), 0)

    def intra(c):
        base = c * c_len
        rows = slice(base, base + c_len)
        cum = cum_ref[rows, :]
        q = z_ref[rows, 0:128] * (GLA_DK ** -0.5)
        k = z_ref[rows, 128:256]
        v = z_ref[rows, 256:512]
        v_t = v.T.astype(bf16)
        qd_ref[rows, :] = (q * jnp.exp(cum)).astype(bf16)
        last = cum[c_len - 1:c_len, :]
        dl_ref[c] = jnp.broadcast_to(jnp.exp(last), (8, GLA_QK))
        upd_ref[c] = _dot(v_t, (k * jnp.exp(last - cum)).astype(bf16)) * st_mask

        blocks = []
        for sb in range(c_len // sub):
            r0 = sb * sub
            q_i = q[r0:r0 + sub]
            cum_i = cum[r0:r0 + sub]
            ts = []
            for jj in range(sub):
                kj = z_ref[base + r0 + jj:base + r0 + jj + 1, 128:256]
                cj = cum_ref[base + r0 + jj:base + r0 + jj + 1, :]
                dec = jnp.exp(jnp.where(row >= jj, cum_i - cj, -1e30))
                ts.append((q_i * kj * dec).astype(bf16))
            r = _dot(jnp.concatenate(ts, axis=0), expand)
            o_sb = jnp.zeros((sub, GLA_V), f32)
            for jj in range(sub):
                o_sb = o_sb + r[jj * sub:(jj + 1) * sub] * z_ref[base + r0 + jj:base + r0 + jj + 1, 256:512]
            if sb > 0:
                c_prev = cum[r0 - 1:r0, :]
                kx = k[0:r0] * jnp.exp(c_prev - cum[0:r0])
                part = _dot(v_t[:, 0:r0], kx.astype(bf16)) * st_mask
                qx = q_i * jnp.exp(cum_i - c_prev)
                o_sb = o_sb + _dot_nt(qx.astype(bf16), part.astype(bf16))
            blocks.append(o_sb)
        oraw_ref[rows, :] = jnp.concatenate(blocks, axis=0)

    for c in range(n_chunks):
        intra(c)

    def recur(c, st):
        stc_ref[c] = st.astype(bf16)
        return dl_ref[c][0:1, :] * st + upd_ref[c]

    st_ref[...] = lax.fori_loop(0, n_chunks, recur, st_ref[...])
    for c in range(n_chunks):
        rows = slice(c * c_len, (c + 1) * c_len)
        oraw_ref[rows, :] += _dot_nt(qd_ref[rows, :], stc_ref[c])

    o = oraw_ref[...]
    ms = _group_stat(o * o, _group_mean_matrix(GLA_V, GLA_DV))
    og = z_ref[:, 512:768]
    o_ref[...] = (o * lax.rsqrt(ms + NORM_EPS) * no_ref[...] * (og * _sigmoid(og))).astype(bf16)


def _gla(z_gla, wg, bg, no, batch, seq):
    tc = SCAN_TILE
    nt = seq // tc
    n_chunks = tc // GLA_CHUNK
    return pl.pallas_call(
        _gla_kernel,
        out_shape=jax.ShapeDtypeStruct((batch * seq, GLA_V), bf16),
        grid=(batch, nt),
        in_specs=[pl.BlockSpec((tc, W_GLA), lambda b, i: (b * nt + i, 0)),
                  _const_spec((128, GLA_QK)),
                  _const_spec((1, GLA_QK)),
                  _const_spec((1, GLA_V))],
        out_specs=pl.BlockSpec((tc, GLA_V), lambda b, i: (b * nt + i, 0)),
        scratch_shapes=[pltpu.VMEM((GLA_V, GLA_QK), f32),
                        pltpu.VMEM((tc, GLA_QK), f32),
                        pltpu.VMEM((tc, GLA_QK), bf16),
                        pltpu.VMEM((n_chunks, GLA_V, GLA_QK), f32),
                        pltpu.VMEM((n_chunks, 8, GLA_QK), f32),
                        pltpu.VMEM((n_chunks, GLA_V, GLA_QK), bf16),
                        pltpu.VMEM((tc, GLA_V), f32)],
        compiler_params=_cparams(2),
        name="gla",
    )(z_gla, wg, bg, no)


DN_W = DN_HEADS * DN_DK
DN_PAIRS = DN_HEADS // 2


def _stack_heads(x):
    lane = _iota(x.shape, 1)
    zero = jnp.zeros_like(x)
    return jnp.concatenate([jnp.where(lane < DN_DK, x, zero), jnp.where(lane >= DN_DK, x, zero)], axis=0)


def _dot_3x(a, b):
    a_hi = a.astype(bf16)
    b_hi = b.astype(bf16)
    a_lo = (a - a_hi.astype(f32)).astype(bf16)
    b_lo = (b - b_hi.astype(f32)).astype(bf16)
    return _dot(a_hi, b_hi) + (_dot(a_hi, b_lo) + _dot(a_lo, b_hi))


def _dn_pre_kernel(y_ref, ab_ref, alog_ref, dtb_ref,
                   a_ref, attn_ref, rhs_ref, qd_ref, kd_ref, egl_ref,
                   q_s, k_s, beta_s, g_s, gam_s, gamr_s):
    tc = y_ref.shape[0]
    c_len = DN_CHUNK

    sum_mat = _group_mean_matrix(DN_W, DN_DK) * float(DN_DK)
    q = y_ref[:, 0:DN_W]
    k = y_ref[:, DN_W:2 * DN_W]
    q_s[...] = q * lax.rsqrt(_group_stat(q * q, sum_mat) + NORM_EPS) * (DN_DK ** -0.5)
    k_s[...] = k * lax.rsqrt(_group_stat(k * k, sum_mat) + NORM_EPS)

    ab = ab_ref[...]
    lane_head = _iota((tc, DN_W), 1) // DN_DK
    a_rep = jnp.zeros((tc, DN_W), f32)
    b_rep = jnp.zeros((tc, DN_W), f32)
    for h in range(DN_HEADS):
        a_rep = jnp.where(lane_head == h, jnp.broadcast_to(ab[:, h:h + 1], (tc, DN_W)), a_rep)
        b_rep = jnp.where(lane_head == h,
                          jnp.broadcast_to(ab[:, DN_HEADS + h:DN_HEADS + h + 1], (tc, DN_W)), b_rep)
    beta_s[...] = _sigmoid(b_rep)
    g_s[...] = -jnp.exp(alog_ref[...]) * _softplus(a_rep + dtb_ref[...])

    r64 = _iota((c_len, c_len), 0)
    c64 = _iota((c_len, c_len), 1)
    tri = (r64 >= c64).astype(bf16)
    ones = jnp.ones((c_len, c_len), bf16)
    row = _iota((c_len, DN_W), 0)
    colj = _iota((c_len, DN_W), 1) % c_len
    incl = row >= colj
    strict = row > colj

    n_chunks = tc // c_len
    g_parts = _split_bf16(g_s[...], 3)
    zero_b = jnp.zeros((c_len, DN_W), bf16)
    total = lambda parts: functools.reduce(lambda a, b: a + b, reversed(parts))
    for c in range(n_chunks):
        rows = slice(c * c_len, (c + 1) * c_len)
        gam_s[rows, :] = total([_dot(tri, part[rows]) for part in g_parts])
    for c in range(n_chunks):
        rows = slice(c * c_len, (c + 1) * c_len)
        gamr_s[rows, :] = total([_dot(ones, jnp.where(row <= colj, part[rows], zero_b)) for part in g_parts])

    def chunk(c):
        rows = slice(c * c_len, (c + 1) * c_len)
        gam = gam_s[rows, :]
        gam_row = gamr_s[rows, :]
        dec_incl = jnp.exp(jnp.where(incl, gam - gam_row, -1e30))
        dec_strict = jnp.where(strict, dec_incl, 0.0)
        eg = jnp.exp(gam)
        g_last = gam[c_len - 1:c_len, :]
        q = q_s[rows, :]
        k = k_s[rows, :]
        beta = beta_s[rows, :]
        bv = beta * y_ref[rows, 2 * DN_W:3 * DN_W]
        bk = beta * eg * k
        kdec = k * jnp.exp(g_last - gam)
        qd_ref[rows, :] = (q * eg).astype(bf16)
        egl_ref[c] = jnp.broadcast_to(jnp.exp(g_last), (8, DN_W))
        qb = q.astype(bf16)
        kb = k.astype(bf16)
        for p in range(DN_PAIRS):
            ps = slice(p * 128, (p + 1) * 128)
            kbp = kb[:, ps]
            k_bd = _stack_heads(kbp)
            a_pair = beta[:, ps] * _dot_nt(kbp, k_bd) * dec_strict[:, ps]
            a_ref[c * DN_HEADS + 2 * p] = a_pair[:, :DN_CHUNK]
            a_ref[c * DN_HEADS + 2 * p + 1] = a_pair[:, DN_CHUNK:]
            attn_ref[c * DN_PAIRS + p] = (_dot_nt(qb[:, ps], k_bd) * dec_incl[:, ps]).astype(bf16)
            kd_ref[c * DN_PAIRS + p] = _stack_heads(kdec[:, ps]).T.astype(bf16)
            rhs_ref[c * DN_PAIRS + p] = jnp.concatenate(
                [_stack_heads(bv[:, ps]), _stack_heads(bk[:, ps])], axis=1)

    for c in range(n_chunks):
        chunk(c)


def _dn_pre(z_dn, alog, dtb, batch, seq):
    tc = SCAN_TILE
    nt = seq // tc
    t = batch * seq
    npt = (tc // DN_CHUNK) * DN_PAIRS
    n_pair = (t // DN_CHUNK) * DN_PAIRS
    scr = lambda: pltpu.VMEM((tc, DN_W), f32)
    tok = lambda: pl.BlockSpec((tc, DN_W), lambda b, i: (b * nt + i, 0))
    mat = lambda r, w: pl.BlockSpec((npt, r, w), lambda b, i: (b * nt + i, 0, 0))
    return pl.pallas_call(
        _dn_pre_kernel,
        out_shape=(jax.ShapeDtypeStruct((2 * n_pair, DN_CHUNK, DN_CHUNK), f32),
                   jax.ShapeDtypeStruct((n_pair, DN_CHUNK, 128), bf16),
                   jax.ShapeDtypeStruct((n_pair, 128, 256), f32),
                   jax.ShapeDtypeStruct((t, DN_W), bf16),
                   jax.ShapeDtypeStruct((n_pair, 128, 128), bf16),
                   jax.ShapeDtypeStruct((t // DN_CHUNK, 8, DN_W), f32)),
        grid=(batch, nt),
        in_specs=[pl.BlockSpec((tc, 3 * DN_W), lambda b, i: (b * nt + i, 0)),
                  pl.BlockSpec((tc, 128), lambda b, i: (b * nt + i, 4 * DN_W // 128)),
                  _const_spec((1, DN_W)),
                  _const_spec((1, DN_W))],
        out_specs=(pl.BlockSpec((2 * npt, DN_CHUNK, DN_CHUNK), lambda b, i: (b * nt + i, 0, 0)),
                   mat(DN_CHUNK, 128), mat(128, 256), tok(), mat(128, 128),
                   pl.BlockSpec((tc // DN_CHUNK, 8, DN_W), lambda b, i: (b * nt + i, 0, 0))),
        scratch_shapes=[scr(), scr(), scr(), scr(), scr(), scr()],
        compiler_params=_cparams(2),
        name="dn_pre",
    )(z_dn, z_dn, alog, dtb)


def _dn_solve_kernel(a_ref, t_ref):
    c_len = DN_CHUNK
    blk = DN_INV_BLOCK
    vshape = a_ref.shape[1:]

    def row_pair(ip, carry):
        i0 = 2 * ip
        i1 = i0 + 1
        for mb in range(c_len // blk):
            m0 = mb * blk

            def col_j(j, accs):
                a0 = a_ref[i0 * c_len + j]
                a1 = a_ref[i1 * c_len + j]
                ts = [t_ref[j * c_len + m0 + m] for m in range(blk)]
                return (tuple(accs[0][m] - a0 * ts[m] for m in range(blk)),
                        tuple(accs[1][m] - a1 * ts[m] for m in range(blk)))

            zeros = tuple(jnp.zeros(vshape, f32) for _ in range(blk))
            acc0, acc1 = lax.fori_loop(m0, i0, col_j, (zeros, zeros))
            a10 = a_ref[i1 * c_len + i0]
            for m in range(blk):
                t0 = acc0[m] + jnp.where(i0 == m0 + m, 1.0, 0.0)
                t_ref[i0 * c_len + m0 + m] = t0
                t_ref[i1 * c_len + m0 + m] = acc1[m] - a10 * t0 + jnp.where(i1 == m0 + m, 1.0, 0.0)
        return carry

    lax.fori_loop(0, c_len // 2, row_pair, 0)


def _dn_solve(a_t):
    n_ent, n_grp, lanes = a_t.shape
    g = min(n_grp, 8)
    spec = lambda: pl.BlockSpec((n_ent, g, lanes), lambda i: (0, i, 0), pipeline_mode=pl.Buffered(1))
    return pl.pallas_call(
        _dn_solve_kernel,
        out_shape=jax.ShapeDtypeStruct(a_t.shape, f32),
        grid=(n_grp // g,),
        in_specs=[spec()],
        out_specs=spec(),
        compiler_params=_cparams(1),
        name="dn_solve",
    )(a_t)


def _dn_scan_kernel(t_ref, attn_ref, rhs_ref, qd_ref, kd_ref, egl_ref, zg_ref, no_ref,
                    o_ref, st_ref, oraw_ref):
    tc = qd_ref.shape[0]
    c_len = DN_CHUNK
    pairs = range(DN_PAIRS)
    psl = [slice(p * 128, (p + 1) * 128) for p in pairs]

    @pl.when(pl.program_id(1) == 0)
    def _():
        st_ref[...] = jnp.zeros_like(st_ref)

    def solve(c):
        t_pair = [jnp.concatenate([t_ref[c * DN_HEADS + 2 * p], t_ref[c * DN_HEADS + 2 * p + 1]], axis=-1)
                  for p in pairs]
        return [_dot_3x(t_pair[p], rhs_ref[c * DN_PAIRS + p]) for p in pairs]

    n_chunks = tc // c_len
    sols = solve(0)
    for c in range(n_chunks):
        rows = slice(c * c_len, (c + 1) * c_len)
        qd = qd_ref[rows, :]
        eg_last = egl_ref[c][0:1, :]
        sts = [st_ref[p] for p in pairs]
        xq = [_dot(jnp.concatenate([sols[p][:, 128:].astype(bf16), qd[:, psl[p]]], axis=0),
                   sts[p].astype(bf16)) for p in pairs]
        nxt = solve(c + 1) if c + 1 < n_chunks else None
        u_bd = [_stack_heads((sols[p][:, :128] - xq[p][:c_len]).astype(bf16)) for p in pairs]
        au = [_dot(jnp.concatenate([attn_ref[c * DN_PAIRS + p], kd_ref[c * DN_PAIRS + p]], axis=0), u_bd[p])
              for p in pairs]
        for p in pairs:
            st_ref[p] = eg_last[:, psl[p]] * sts[p] + au[p][c_len:]
        oraw_ref[rows, :] = jnp.concatenate([xq[p][c_len:] + au[p][:c_len] for p in pairs], axis=-1)
        sols = nxt

    o = oraw_ref[...]
    ms = _group_stat(o * o, _group_mean_matrix(DN_W, DN_DV))
    zg = zg_ref[...]
    o_ref[...] = (o * lax.rsqrt(ms + NORM_EPS) * no_ref[...] * (zg * _sigmoid(zg))).astype(bf16)


def _dn_scan(t_all, attn, rhs, qd, kd, egl, z_dn, no, batch, seq):
    tc = SCAN_TILE
    nt = seq // tc
    npt = (tc // DN_CHUNK) * DN_PAIRS
    tok = lambda: pl.BlockSpec((tc, DN_W), lambda b, i: (b * nt + i, 0))
    mat = lambda r, w: pl.BlockSpec((npt, r, w), lambda b, i: (b * nt + i, 0, 0))
    return pl.pallas_call(
        _dn_scan_kernel,
        out_shape=jax.ShapeDtypeStruct((batch * seq, DN_W), bf16),
        grid=(batch, nt),
        in_specs=[pl.BlockSpec((2 * npt, DN_CHUNK, DN_CHUNK), lambda b, i: (b * nt + i, 0, 0)),
                  mat(DN_CHUNK, 128), mat(128, 256), tok(), mat(128, 128),
                  pl.BlockSpec((tc // DN_CHUNK, 8, DN_W), lambda b, i: (b * nt + i, 0, 0)),
                  pl.BlockSpec((tc, DN_W), lambda b, i: (b * nt + i, 3)),
                  _const_spec((1, DN_W))],
        out_specs=tok(),
        scratch_shapes=[pltpu.VMEM((DN_PAIRS, 128, 128), f32),
                        pltpu.VMEM((tc, DN_W), f32)],
        compiler_params=_cparams(2),
        name="dn_scan",
    )(t_all, attn, rhs, qd, kd, egl, z_dn, no)


def _dn(z_dn, alog, dtb, no, batch, seq):
    a_all, attn, rhs, qd, kd, egl = _dn_pre(z_dn, alog, dtb, batch, seq)
    n_prob = a_all.shape[0]
    n_ent = DN_CHUNK * DN_CHUNK
    a_t = a_all.reshape(n_prob, n_ent).T.reshape(n_ent, n_prob // 128, 128)
    t_all = _dn_solve(a_t).reshape(n_ent, n_prob).T.reshape(n_prob, DN_CHUNK, DN_CHUNK)
    return _dn_scan(t_all, attn, rhs, qd, kd, egl, z_dn, no, batch, seq)


def _merge_kernel(x_ref, g_ref, oa_ref, ob_ref, oc_ref, od_ref, wg_ref, wb_ref, wo_ref, out_ref):
    x = x_ref[...]
    h = _rms(x, g_ref[...]).astype(bf16)
    merged = jnp.zeros(x.shape, f32)
    for n, o_ref in enumerate((oa_ref, ob_ref, oc_ref, od_ref)):
        gate = _sigmoid(_dot(h, wg_ref[:, n * D_MODEL:(n + 1) * D_MODEL]))
        merged = merged + gate * _dot(o_ref[...], wb_ref[n])
    out_ref[...] = x + _dot(merged.astype(bf16), wo_ref[...])


def _merge(x2, g, oa, ob, oc, od, wg, wb, wo):
    t = x2.shape[0]
    tm = TOKEN_TILE
    tok = lambda w: pl.BlockSpec((tm, w), lambda i: (i, 0))
    return pl.pallas_call(
        _merge_kernel,
        out_shape=jax.ShapeDtypeStruct((t, D_MODEL), f32),
        grid=(t // tm,),
        in_specs=[tok(D_MODEL), _const_spec((1, D_MODEL)),
                  tok(BRANCH_WIDTH), tok(BRANCH_WIDTH), tok(BRANCH_WIDTH), tok(BRANCH_WIDTH),
                  _const_spec((D_MODEL, N_BRANCH * D_MODEL)),
                  _const_spec((N_BRANCH, BRANCH_WIDTH, D_MODEL)),
                  _const_spec((D_MODEL, D_MODEL))],
        out_specs=tok(D_MODEL),
        compiler_params=_cparams(1),
        name="merge",
    )(x2, g, oa, ob, oc, od, wg, wb, wo)


def _memkv_kernel(mem_ref, g_ref, wk_ref, wv_ref, k_ref, v_ref):
    mn = _rms(mem_ref[...], g_ref[...]).astype(bf16)
    k_ref[...] = _dot(mn, wk_ref[...]).astype(bf16)
    v_ref[...] = _dot(mn, wv_ref[...]).astype(bf16)


def _memkv(mem2, g, wk, wv, mem_len):
    t = mem2.shape[0]
    blk = lambda: pl.BlockSpec((mem_len, D_MODEL), lambda i: (i, 0))
    return pl.pallas_call(
        _memkv_kernel,
        out_shape=(jax.ShapeDtypeStruct((t, D_MODEL), bf16), jax.ShapeDtypeStruct((t, D_MODEL), bf16)),
        grid=(t // mem_len,),
        in_specs=[blk(), _const_spec((1, D_MODEL)),
                  _const_spec((D_MODEL, D_MODEL)), _const_spec((D_MODEL, D_MODEL))],
        out_specs=(blk(), blk()),
        compiler_params=_cparams(1),
        name="mem_kv",
    )(mem2, g, wk, wv)


def _xattn_kernel(x_ref, g_ref, k_ref, v_ref, wq_ref, wo_ref, out_ref):
    x = x_ref[...]
    h = _rms(x, g_ref[...]).astype(bf16)
    q = (_dot(h, wq_ref[...]) * (XATTN_DIM ** -0.5)).astype(bf16)
    hsl = [slice(hh * XATTN_DIM, (hh + 1) * XATTN_DIM) for hh in range(XATTN_HEADS)]
    scores = [_dot_nt(q[:, sl], k_ref[:, sl]) for sl in hsl]
    probs = []
    for s in scores:
        p = jnp.exp(s - jnp.max(s, axis=-1, keepdims=True))
        probs.append((p / jnp.sum(p, axis=-1, keepdims=True)).astype(bf16))
    o = jnp.concatenate([_dot(p, v_ref[:, sl]).astype(bf16) for p, sl in zip(probs, hsl)], axis=-1)
    out_ref[...] = x + _dot(o, wo_ref[...])


def _xattn(x2, g, kmem, vmem, wq, wo, batch, seq, mem_len):
    tm = TOKEN_TILE
    nt = seq // tm
    tok = pl.BlockSpec((tm, D_MODEL), lambda b, i: (b * nt + i, 0))
    memb = lambda: pl.BlockSpec((mem_len, D_MODEL), lambda b, i: (b, 0))
    return pl.pallas_call(
        _xattn_kernel,
        out_shape=jax.ShapeDtypeStruct((batch * seq, D_MODEL), f32),
        grid=(batch, nt),
        in_specs=[tok, _const_spec((1, D_MODEL)), memb(), memb(),
                  _const_spec((D_MODEL, D_MODEL)), _const_spec((D_MODEL, D_MODEL))],
        out_specs=tok,
        compiler_params=_cparams(2),
        name="xattn",
    )(x2, g, kmem, vmem, wq, wo)


def _mlp_kernel(x_ref, g_ref, wu_ref, wd_ref, gf_ref, out_ref, *, final):
    x = x_ref[...]
    h = _rms(x, g_ref[...]).astype(bf16)
    acc = x
    for c in range(MLP_HIDDEN // MLP_CHUNK):
        sl = slice(c * MLP_CHUNK, (c + 1) * MLP_CHUNK)
        a = jnp.maximum(_dot(h, wu_ref[:, sl]), 0.0)
        acc = acc + _dot((a * a).astype(bf16), wd_ref[sl, :])
    if final:
        acc = _rms(acc, gf_ref[...])
    out_ref[...] = acc


def _mlp(x2, g, wu, wd, gf, final):
    t = x2.shape[0]
    tm = TOKEN_TILE
    tok = pl.BlockSpec((tm, D_MODEL), lambda i: (i, 0))
    return pl.pallas_call(
        functools.partial(_mlp_kernel, final=final),
        out_shape=jax.ShapeDtypeStruct((t, D_MODEL), f32),
        grid=(t // tm,),
        in_specs=[tok, _const_spec((1, D_MODEL)),
                  _const_spec((D_MODEL, MLP_HIDDEN)), _const_spec((MLP_HIDDEN, D_MODEL)),
                  _const_spec((1, D_MODEL))],
        out_specs=tok,
        compiler_params=_cparams(1),
        name="mlp",
    )(x2, g, wu, wd, gf)


def _pad_cols(w, width):
    return jnp.pad(w, ((0, 0), (0, width - w.shape[1])))


def _prep_w_in(w_in):
    col = lambda i: w_in[:, IN_OFFS[i]:IN_OFFS[i + 1]]
    zeros = lambda n: jnp.zeros((D_MODEL, n), w_in.dtype)
    (cq, ckv, kpe, sg_uv, gq, gk, gv, g_lr, g_og, dq, dk, dv, da, db, dz) = (col(i) for i in range(15))
    small = jnp.concatenate([
        cq, ckv, zeros(MLA_NOPE), kpe, zeros(MLA_HEAD_PAD - MLA_NOPE - MLA_ROPE),
        sg_uv,
        gq, gk, gv, g_og, _pad_cols(g_lr, 128),
        dq, dk, dv, dz, _pad_cols(jnp.concatenate([da, db], axis=1), 128),
    ], axis=1)
    assert small.shape[1] == W_SMALL
    return small.astype(bf16), col(15).astype(bf16)


def _rope_inv_freq():
    inv_freq = ROPE_THETA ** (-jnp.arange(0, MLA_ROPE, 2, dtype=f32) / MLA_ROPE)
    return inv_freq[:, None]


def kernel(x, mem, positions, norm_mix, w_in, mla_norm_q, mla_norm_kv, mla_w_uq, mla_w_ukv, sg_ln_g, sg_ln_b, sg_w, sg_b, gla_w_gate, gla_b_gate, gla_norm, dn_conv, dn_a_log, dn_dt_bias, dn_norm, w_branch, w_out, norm_xattn, norm_mem, xattn_wq, xattn_wk, xattn_wv, xattn_wo, norm_mlp, w_up, w_down, norm_final):
    batch, seq, _ = x.shape
    mem_len = mem.shape[1]
    depth = w_in.shape[0]
    assert seq % TOKEN_TILE == 0 and seq % SCAN_TILE == 0 and seq % ATTN_TILE == 0
    x2 = x.reshape(batch * seq, D_MODEL)
    mem2 = mem.reshape(batch * mem_len, D_MODEL)
    pos2 = positions.reshape(batch * seq // TOKEN_TILE, 1, TOKEN_TILE)
    invf = _rope_inv_freq()
    row = lambda v: v[None, :]

    for l in range(depth):
        w_small, w_gates = _prep_w_in(w_in[l])
        z_mla, z_sg, z_gla, z_dn = _inproj(x2, row(norm_mix[l]), w_small, dn_conv[l], seq)

        wuq = jnp.pad(mla_w_uq[l].reshape(MLA_Q_RANK, MLA_HEADS, MLA_NOPE + MLA_ROPE),
                      ((0, 0), (0, 0), (0, MLA_HEAD_PAD - MLA_NOPE - MLA_ROPE)))
        wuq = wuq.reshape(MLA_Q_RANK, MLA_HEADS * MLA_HEAD_PAD).astype(bf16)
        wukv = mla_w_ukv[l].reshape(MLA_KV_RANK, MLA_HEADS, MLA_NOPE + MLA_V)
        wuk = jnp.pad(wukv[:, :, :MLA_NOPE], ((0, 0), (0, 0), (0, MLA_HEAD_PAD - MLA_NOPE)))
        wuk = wuk.reshape(MLA_KV_RANK, MLA_HEADS * MLA_HEAD_PAD).astype(bf16)
        wuv = wukv[:, :, MLA_NOPE:].reshape(MLA_KV_RANK, MLA_HEADS * MLA_V).astype(bf16)
        q_a, k_a, v_a = _mla_pre(z_mla, pos2, invf, row(mla_norm_q[l]), row(mla_norm_kv[l]), wuq, wuk, wuv,
                                 batch, seq)
        o_a = _mla_attn(q_a, k_a, v_a, batch, seq)

        o_b = _sg(z_sg, row(sg_ln_g[l]), row(sg_ln_b[l]), sg_w[l], sg_b[l].T)

        wg = jnp.pad(gla_w_gate[l], ((0, 128 - GLA_GATE_RANK), (0, 0))).astype(bf16)
        o_c = _gla(z_gla, wg, row(gla_b_gate[l]), row(jnp.tile(gla_norm[l], GLA_HEADS)), batch, seq)

        o_d = _dn(z_dn, row(jnp.repeat(dn_a_log[l], DN_DK)), row(jnp.repeat(dn_dt_bias[l], DN_DK)),
                  row(jnp.tile(dn_norm[l], DN_HEADS)), batch, seq)

        x2 = _merge(x2, row(norm_mix[l]), o_a, o_b, o_c, o_d, w_gates,
                    w_branch[l].astype(bf16), w_out[l].astype(bf16))

        k_m, v_m = _memkv(mem2, row(norm_mem[l]), xattn_wk[l].astype(bf16), xattn_wv[l].astype(bf16), mem_len)
        x2 = _xattn(x2, row(norm_xattn[l]), k_m, v_m, xattn_wq[l].astype(bf16), xattn_wo[l].astype(bf16),
                    batch, seq, mem_len)
        x2 = _mlp(x2, row(norm_mlp[l]), w_up[l].astype(bf16), w_down[l].astype(bf16), row(norm_final),
                  final=(l == depth - 1))
    return x2.reshape(batch, seq, D_MODEL)
```

```python
import functools

import jax
import jax.numpy as jnp
import numpy as np
from jax import lax
from jax.experimental import pallas as pl
from jax.experimental.pallas import tpu as pltpu

f32 = jnp.float32
bf16 = jnp.bfloat16
HIGHEST = lax.Precision.HIGHEST

D_MODEL = 1024
NORM_EPS = 1e-6
N_BRANCH = 4
BRANCH_WIDTH = 256

MLA_HEADS = 4
MLA_Q_RANK = 256
MLA_KV_RANK = 128
MLA_NOPE = 64
MLA_ROPE = 32
MLA_V = 64
ROPE_THETA = 10000.0
MLA_HEAD_PAD = 128
MLA_VT_ROWS = 80

SG_GROUPS = 4
SG_WIDTH = 256
SG_CHUNK = 128

GLA_HEADS = 4
GLA_DK = 32
GLA_DV = 64
GLA_GATE_RANK = 16
GLA_GATE_TEMP = 16.0
GLA_CHUNK = 64

DN_HEADS = 4
DN_DK = 64
DN_DV = 64
DN_CONV = 4
DN_CHUNK = 64
DN_INV_BLOCK = 16

XATTN_HEADS = 4
XATTN_DIM = D_MODEL // XATTN_HEADS
MLP_HIDDEN = 4 * D_MODEL

IN_SIZES = (
    MLA_Q_RANK, MLA_KV_RANK, MLA_ROPE,
    2 * SG_WIDTH,
    GLA_HEADS * GLA_DK, GLA_HEADS * GLA_DK, GLA_HEADS * GLA_DV,
    GLA_GATE_RANK, GLA_HEADS * GLA_DV,
    DN_HEADS * DN_DK, DN_HEADS * DN_DK, DN_HEADS * DN_DV,
    DN_HEADS, DN_HEADS, DN_HEADS * DN_DV,
    N_BRANCH * D_MODEL,
)
IN_OFFS = tuple(int(v) for v in np.cumsum((0,) + IN_SIZES))

W_MLA = 512
W_SG = 512
W_GLA = 896
W_DN = 1152
W_SMALL = W_MLA + W_SG + W_GLA + W_DN

TOKEN_TILE = 512
ATTN_TILE = 256
ATTN_KEY_TILE = 128
SCAN_TILE = 512
MLP_CHUNK = 1024
VMEM_LIMIT = 56 * 1024 * 1024


def _cparams(n_axes):
    return pltpu.CompilerParams(dimension_semantics=("arbitrary",) * n_axes,
                                vmem_limit_bytes=VMEM_LIMIT)


def _const_spec(shape):
    nd = len(shape)
    return pl.BlockSpec(shape, lambda *_: (0,) * nd, pipeline_mode=pl.Buffered(1))


def _rms(x, g):
    return x * lax.rsqrt(jnp.mean(x * x, axis=-1, keepdims=True) + NORM_EPS) * g


def _sigmoid(x):
    return 1.0 / (1.0 + jnp.exp(-x))


def _softplus(x):
    return jnp.maximum(x, 0.0) + jnp.log1p(jnp.exp(-jnp.abs(x)))


def _dot(a, b):
    return jnp.dot(a, b, preferred_element_type=f32)


def _split_bf16(x, terms):
    parts = []
    for _ in range(terms):
        p = x.astype(bf16)
        parts.append(p)
        x = x - p.astype(f32)
    return parts


def _dot_sel(sel, x, terms=3):
    sel = sel.astype(bf16)
    parts = [_dot(sel, p) for p in _split_bf16(x, terms)]
    return functools.reduce(lambda a, b: a + b, reversed(parts))


def _group_stat(x, mat, terms=2):
    mat = mat.astype(bf16)
    parts = [_dot(p, mat) for p in _split_bf16(x, terms)]
    return functools.reduce(lambda a, b: a + b, reversed(parts))


def _dot_nt(a, b):
    return lax.dot_general(a, b, (((1,), (1,)), ((), ())), preferred_element_type=f32)


def _dot_tn(a, b):
    return lax.dot_general(a, b, (((0,), (0,)), ((), ())), preferred_element_type=f32)


def _iota(shape, dim):
    return lax.broadcasted_iota(jnp.int32, shape, dim)


def _group_mean_matrix(width, group):
    r = _iota((width, width), 0) // group
    c = _iota((width, width), 1) // group
    return jnp.where(r == c, 1.0 / group, 0.0).astype(f32)


DN_CONV_TAPS = 4
DN_CONV_COLS = 768
DN_CONV_PAD = 8


def _inproj_kernel(x_ref, g_ref, w_ref, cw_ref, mla_ref, sg_ref, gla_ref, dn_ref, xbuf_ref, *,
                   tiles_per_seq):
    @pl.when(pl.program_id(0) % tiles_per_seq == 0)
    def _():
        xbuf_ref[0:DN_CONV_PAD, :] = jnp.zeros((DN_CONV_PAD, DN_CONV_COLS), f32)

    h = _rms(x_ref[...], g_ref[...]).astype(bf16)
    tm = h.shape[0]
    z_dn = _dot(h, w_ref[:, W_SMALL - W_DN:])
    dn_ref[:, DN_CONV_COLS:] = z_dn[:, DN_CONV_COLS:]
    xbuf_ref[DN_CONV_PAD:DN_CONV_PAD + tm, :] = z_dn[:, :DN_CONV_COLS]
    y = jnp.zeros((tm, DN_CONV_COLS), f32)
    for kk in range(DN_CONV_TAPS):
        y = y + cw_ref[kk:kk + 1, :] * xbuf_ref[pl.ds(DN_CONV_PAD - (DN_CONV_TAPS - 1) + kk, tm), :]
    xbuf_ref[0:DN_CONV_PAD, :] = xbuf_ref[tm:tm + DN_CONV_PAD, :]
    dn_ref[:, :DN_CONV_COLS] = y * _sigmoid(y)
    lo = 0
    for ref, width in ((mla_ref, W_MLA), (sg_ref, W_SG), (gla_ref, W_GLA)):
        ref[...] = _dot(h, w_ref[:, lo:lo + width])
        lo += width


def _inproj(x2, g, w_small, conv_w, seq):
    t = x2.shape[0]
    tm = TOKEN_TILE
    outs = tuple(jax.ShapeDtypeStruct((t, w), f32) for w in (W_MLA, W_SG, W_GLA, W_DN))
    return pl.pallas_call(
        functools.partial(_inproj_kernel, tiles_per_seq=seq // tm),
        out_shape=outs,
        grid=(t // tm,),
        in_specs=[pl.BlockSpec((tm, D_MODEL), lambda i: (i, 0)),
                  _const_spec((1, D_MODEL)),
                  _const_spec((D_MODEL, W_SMALL)),
                  _const_spec((DN_CONV_TAPS, DN_CONV_COLS))],
        out_specs=tuple(pl.BlockSpec((tm, w), lambda i: (i, 0)) for w in (W_MLA, W_SG, W_GLA, W_DN)),
        scratch_shapes=[pltpu.VMEM((tm + 2 * DN_CONV_PAD, DN_CONV_COLS), f32)],
        compiler_params=_cparams(1),
        name="inproj",
    )(x2, g, w_small, conv_w)


def _mla_pre_kernel(z_ref, pos_ref, invf_ref, nq_ref, nkv_ref, wuq_ref, wuk_ref, wuv_ref,
                    qt_ref, k_ref, vt_ref):
    z = z_ref[...]
    tm = z.shape[0]
    cq = z[:, :MLA_Q_RANK]
    ckv = z[:, MLA_Q_RANK:MLA_Q_RANK + MLA_KV_RANK]
    kpe = z[:, MLA_Q_RANK + MLA_KV_RANK:]

    half = MLA_ROPE // 2
    ang_t = invf_ref[...] * pos_ref[...].astype(f32)
    f_row = _iota((half, MLA_HEAD_PAD), 0)
    f_lane = _iota((half, MLA_HEAD_PAD), 1)
    place = ((f_lane == MLA_NOPE + f_row) | (f_lane == MLA_NOPE + half + f_row)).astype(bf16)

    def to_lanes(x_t):
        parts = [_dot_tn(p, place) for p in _split_bf16(x_t, 3)]
        return functools.reduce(lambda a, b: a + b, reversed(parts))

    cos = to_lanes(jnp.cos(ang_t))
    sin = to_lanes(jnp.sin(ang_t))
    lane = _iota((tm, MLA_HEAD_PAD), 1)
    in_x1 = (lane >= MLA_NOPE) & (lane < MLA_NOPE + half)
    in_x2 = (lane >= MLA_NOPE + half) & (lane < MLA_NOPE + MLA_ROPE)
    c_tab = jnp.where(in_x1 | in_x2, cos, 1.0)
    s1_tab = jnp.where(in_x1, -sin, 0.0)
    s2_tab = jnp.where(in_x2, sin, 0.0)

    def rope(t):
        return (t * c_tab + pltpu.roll(t, MLA_HEAD_PAD - half, 1) * s1_tab
                + pltpu.roll(t, half, 1) * s2_tab)

    scale = (MLA_NOPE + MLA_ROPE) ** -0.5
    q = _dot(_rms(cq, nq_ref[...]).astype(bf16), wuq_ref[...])
    kvn = _rms(ckv, nkv_ref[...]).astype(bf16)
    kn = _dot(kvn, wuk_ref[...])
    v_t = _dot(kvn, wuv_ref[...]).T
    kpe_rot = rope(kpe)
    q_rot = []
    for h in range(MLA_HEADS):
        sl = slice(h * MLA_HEAD_PAD, (h + 1) * MLA_HEAD_PAD)
        q_rot.append(rope(q[:, sl]) * scale)
        k_ref[:, sl] = (kn[:, sl] + kpe_rot).astype(bf16)
    qt_ref[...] = jnp.concatenate(q_rot, axis=-1).T.astype(bf16)
    pad = jnp.concatenate([jnp.ones((1, tm), f32), jnp.zeros((MLA_VT_ROWS - MLA_V - 1, tm), f32)], axis=0)
    rows = []
    for h in range(MLA_HEADS):
        rows += [v_t[h * MLA_V:(h + 1) * MLA_V], pad]
    vt_ref[...] = jnp.concatenate(rows, axis=0).astype(bf16)


def _mla_pre(z_mla, pos2, invf, nq, nkv, wuq, wuk, wuv, batch, seq):
    t = z_mla.shape[0]
    tm = TOKEN_TILE
    nt = seq // tm
    hp = MLA_HEADS * MLA_HEAD_PAD
    hv = MLA_HEADS * MLA_V
    hvt = MLA_HEADS * MLA_VT_ROWS
    return pl.pallas_call(
        _mla_pre_kernel,
        out_shape=(jax.ShapeDtypeStruct((batch, hp, seq), bf16),
                   jax.ShapeDtypeStruct((t, hp), bf16),
                   jax.ShapeDtypeStruct((batch, hvt, seq), bf16)),
        grid=(t // tm,),
        in_specs=[pl.BlockSpec((tm, W_MLA), lambda i: (i, 0)),
                  pl.BlockSpec((None, 1, tm), lambda i: (i, 0, 0)),
                  _const_spec((MLA_ROPE // 2, 1)),
                  _const_spec((1, MLA_Q_RANK)),
                  _const_spec((1, MLA_KV_RANK)),
                  _const_spec((MLA_Q_RANK, hp)),
                  _const_spec((MLA_KV_RANK, hp)),
                  _const_spec((MLA_KV_RANK, hv))],
        out_specs=(pl.BlockSpec((None, hp, tm), lambda i: (i // nt, 0, i % nt)),
                   pl.BlockSpec((tm, hp), lambda i: (i, 0)),
                   pl.BlockSpec((None, hvt, tm), lambda i: (i // nt, 0, i % nt))),
        compiler_params=_cparams(1),
        name="mla_pre",
    )(z_mla, pos2, invf, nq, nkv, wuq, wuk, wuv)


def _mla_attn_kernel(qt_ref, k_ref, vt_ref, o_ref, sa_ref, sb_ref):
    i = pl.program_id(1)
    tq = qt_ref.shape[1]
    tk = sa_ref.shape[1]
    per = tq // tk
    key = _iota((tk, tq), 0)
    qry = _iota((tk, tq), 1)

    def issue_scores(b, buf):
        start = pl.multiple_of(b * tk, tk)
        cms = []
        for h in range(MLA_HEADS):
            hsl = slice(h * MLA_HEAD_PAD, (h + 1) * MLA_HEAD_PAD)
            s = _dot(k_ref[pl.ds(start, tk), hsl], qt_ref[hsl, :])
            buf[h] = s
            cms.append(jnp.max(s, axis=0, keepdims=True))
        return tuple(cms)

    def consume(b, buf, cms, state, diag_offset=None):
        start = pl.multiple_of(b * tk, tk)
        probs = []
        for h in range(MLA_HEADS):
            s = buf[h]
            if diag_offset is None:
                cm = cms[h]
            else:
                s = jnp.where(key + diag_offset <= qry, s, -jnp.inf)
                cm = jnp.max(s, axis=0, keepdims=True)
            m_new = jnp.maximum(state[h][0], cm)
            probs.append((m_new, jnp.exp(s - m_new).astype(bf16)))
        new = []
        for h in range(MLA_HEADS):
            vsl = slice(h * MLA_VT_ROWS, (h + 1) * MLA_VT_ROWS)
            m, acc = state[h]
            m_new, p = probs[h]
            acc = jnp.exp(m - m_new) * acc + _dot(vt_ref[vsl, pl.ds(start, tk)], p)
            new.append((m_new, acc))
        return tuple(new)

    def body(t, carry):
        state, cm_a = carry
        cm_b = issue_scores(per * t + 1, sb_ref)
        state = consume(per * t, sa_ref, cm_a, state)
        cm_a = issue_scores(per * t + 2, sa_ref)
        state = consume(per * t + 1, sb_ref, cm_b, state)
        return state, cm_a

    assert per == 2
    init = tuple((jnp.full((1, tq), -jnp.inf, f32), jnp.zeros((MLA_VT_ROWS, tq), f32))
                 for _ in range(MLA_HEADS))
    carry = (init, issue_scores(0, sa_ref))
    carry = lax.fori_loop(0, i // 4, lambda u, c: body(4 * u + 3, body(4 * u + 2, body(4 * u + 1, body(4 * u, c)))),
                          carry)
    t4 = 4 * (i // 4)
    carry = lax.fori_loop(0, (i % 4) // 2, lambda _, c: body(t4 + 1, body(t4, c)), carry)
    state, _ = lax.fori_loop(0, i % 2, lambda _, c: body(i - 1, c), carry)
    issue_scores(per * i + 1, sb_ref)
    state = consume(per * i, sa_ref, None, state, diag_offset=0)
    state = consume(per * i + 1, sb_ref, None, state, diag_offset=tk)
    o_t = jnp.concatenate([acc[:MLA_V] / acc[MLA_V:MLA_V + 1] for _, acc in state], axis=0)
    o_ref[...] = o_t.T.astype(bf16)


def _mla_attn(q_t, k, v_t, batch, seq):
    tq = ATTN_TILE
    nq = seq // tq
    hp = MLA_HEADS * MLA_HEAD_PAD
    hv = MLA_HEADS * MLA_V
    hvt = MLA_HEADS * MLA_VT_ROWS
    return pl.pallas_call(
        _mla_attn_kernel,
        out_shape=jax.ShapeDtypeStruct((batch * seq, hv), bf16),
        grid=(batch, nq),
        in_specs=[pl.BlockSpec((None, hp, tq), lambda b, i: (b, 0, i)),
                  pl.BlockSpec((seq, hp), lambda b, i: (b, 0)),
                  pl.BlockSpec((None, hvt, seq), lambda b, i: (b, 0, 0))],
        out_specs=pl.BlockSpec((tq, hv), lambda b, i: (b * nq + i, 0)),
        scratch_shapes=[pltpu.VMEM((MLA_HEADS, ATTN_KEY_TILE, tq), f32),
                        pltpu.VMEM((MLA_HEADS, ATTN_KEY_TILE, tq), f32)],
        compiler_params=_cparams(2),
        name="mla_attn",
    )(q_t, k, v_t)


def _sg_kernel(z_ref, lng_ref, lnb_ref, ws_ref, bst_ref, o_ref):
    z = z_ref[...]
    tm = z.shape[0]
    uv = 0.5 * z * (1.0 + jnp.tanh(np.sqrt(2.0 / np.pi) * (z + 0.044715 * (z * z * z))))
    u = uv[:, :SG_WIDTH]
    v = uv[:, SG_WIDTH:]
    mu = jnp.mean(v, axis=-1, keepdims=True)
    vc = v - mu
    vn = vc * lax.rsqrt(jnp.mean(vc * vc, axis=-1, keepdims=True) + NORM_EPS)
    vn = (vn * lng_ref[...] + lnb_ref[...]).astype(bf16)

    gw = SG_WIDTH // SG_GROUPS
    lane_group = _iota((SG_CHUNK, SG_WIDTH), 1) // gw
    tril = _iota((SG_CHUNK, SG_CHUNK), 0) >= _iota((SG_CHUNK, SG_CHUNK), 1)
    bias = jnp.zeros((SG_CHUNK, SG_WIDTH), f32)
    wm = []
    for g in range(SG_GROUPS):
        wm.append(jnp.where(tril, ws_ref[g], 0.0).astype(bf16))
        bias = jnp.where(lane_group == g,
                         jnp.broadcast_to(bst_ref[:, g:g + 1], (SG_CHUNK, SG_WIDTH)), bias)
    for c in range(tm // SG_CHUNK):
        rows = slice(c * SG_CHUNK, (c + 1) * SG_CHUNK)
        vch = vn[rows]
        mixed = bias
        for g in range(SG_GROUPS):
            mixed = mixed + jnp.where(lane_group == g, _dot(wm[g], vch), 0.0)
        o_ref[rows, :] = (u[rows] * mixed).astype(bf16)


def _sg(z_sg, lng, lnb, ws, bst):
    t = z_sg.shape[0]
    tm = TOKEN_TILE
    return pl.pallas_call(
        _sg_kernel,
        out_shape=jax.ShapeDtypeStruct((t, SG_WIDTH), bf16),
        grid=(t // tm,),
        in_specs=[pl.BlockSpec((tm, W_SG), lambda i: (i, 0)),
                  _const_spec((1, SG_WIDTH)),
                  _const_spec((1, SG_WIDTH)),
                  _const_spec((SG_GROUPS, SG_CHUNK, SG_CHUNK)),
                  _const_spec((SG_CHUNK, SG_GROUPS))],
        out_specs=pl.BlockSpec((tm, SG_WIDTH), lambda i: (i, 0)),
        compiler_params=_cparams(1),
        name="spatial_gating",
    )(z_sg, lng, lnb, ws, bst)


GLA_QK = GLA_HEADS * GLA_DK
GLA_V = GLA_HEADS * GLA_DV
GLA_SUB = 16


def _gla_kernel(z_ref, wg_ref, bg_ref, no_ref, o_ref,
                st_ref, cum_ref, qd_ref, upd_ref, dl_ref, stc_ref, oraw_ref):
    tc = z_ref.shape[0]
    c_len = GLA_CHUNK
    n_chunks = tc // c_len

    @pl.when(pl.program_id(1) == 0)
    def _():
        st_ref[...] = jnp.zeros_like(st_ref)

    zg = _dot(z_ref[:, 768:896].astype(bf16), wg_ref[...]) + bg_ref[...]
    log_a = (jnp.minimum(zg, 0.0) - jnp.log1p(jnp.exp(-jnp.abs(zg)))) * (1.0 / GLA_GATE_TEMP)

    tri = (_iota((c_len, c_len), 0) >= _iota((c_len, c_len), 1)).astype(f32)
    for c in range(n_chunks):
        cum_ref[c * c_len:(c + 1) * c_len, :] = _dot_sel(tri, log_a[c * c_len:(c + 1) * c_len])

    expand = (_iota((GLA_QK, GLA_V), 0) // GLA_DK == _iota((GLA_QK, GLA_V), 1) // GLA_DV).astype(bf16)
    st_mask = (_iota((GLA_V, GLA_QK), 0) // GLA_DV == _iota((GLA_V, GLA_QK), 1) // GLA_DK).astype(f32)
    sub = GLA_SUB
    row = _iota((sub, GLA_QK), 0)

    def intra(c):
        base = c * c_len
        rows = slice(base, base + c_len)
        cum = cum_ref[rows, :]
        q = z_ref[rows, 0:128] * (GLA_DK ** -0.5)
        k = z_ref[rows, 128:256]
        v = z_ref[rows, 256:512]
        v_t = v.T.astype(bf16)
        qd_ref[rows, :] = (q * jnp.exp(cum)).astype(bf16)
        last = cum[c_len - 1:c_len, :]
        dl_ref[c] = jnp.broadcast_to(jnp.exp(last), (8, GLA_QK))
        upd_ref[c] = _dot(v_t, (k * jnp.exp(last - cum)).astype(bf16)) * st_mask

        n_sub = c_len // sub
        parts = [None]
        for sb in range(1, n_sub):
            r0 = sb * sub
            kx = k[0:r0] * jnp.exp(cum[r0 - 1:r0, :] - cum[0:r0])
            parts.append((_dot(v_t[:, 0:r0], kx.astype(bf16)) * st_mask).astype(bf16))
        rs = []
        for sb in range(n_sub):
            r0 = sb * sub
            q_i = q[r0:r0 + sub]
            cum_i = cum[r0:r0 + sub]
            ts = []
            for jj in range(sub):
                kj = z_ref[base + r0 + jj:base + r0 + jj + 1, 128:256]
                cj = cum_ref[base + r0 + jj:base + r0 + jj + 1, :]
                dec = jnp.exp(jnp.where(row >= jj, cum_i - cj, -1e30))
                ts.append((q_i * kj * dec).astype(bf16))
            rs.append(_dot(jnp.concatenate(ts, axis=0), expand))
        blocks = []
        for sb in range(n_sub):
            r0 = sb * sub
            o_sb = jnp.zeros((sub, GLA_V), f32)
            for jj in range(sub):
                o_sb = o_sb + rs[sb][jj * sub:(jj + 1) * sub] * z_ref[base + r0 + jj:base + r0 + jj + 1, 256:512]
            if sb > 0:
                qx = q[r0:r0 + sub] * jnp.exp(cum[r0:r0 + sub] - cum[r0 - 1:r0, :])
                o_sb = o_sb + _dot_nt(qx.astype(bf16), parts[sb])
            blocks.append(o_sb)
        oraw_ref[rows, :] = jnp.concatenate(blocks, axis=0)

    for c in range(n_chunks):
        intra(c)

    def recur(c, st):
        stc_ref[c] = st.astype(bf16)
        return dl_ref[c][0:1, :] * st + upd_ref[c]

    st_ref[...] = lax.fori_loop(0, n_chunks, recur, st_ref[...])
    for c in range(n_chunks):
        rows = slice(c * c_len, (c + 1) * c_len)
        oraw_ref[rows, :] += _dot_nt(qd_ref[rows, :], stc_ref[c])

    o = oraw_ref[...]
    ms = _group_stat(o * o, _group_mean_matrix(GLA_V, GLA_DV))
    og = z_ref[:, 512:768]
    o_ref[...] = (o * lax.rsqrt(ms + NORM_EPS) * no_ref[...] * (og * _sigmoid(og))).astype(bf16)


def _gla(z_gla, wg, bg, no, batch, seq):
    tc = SCAN_TILE
    nt = seq // tc
    n_chunks = tc // GLA_CHUNK
    return pl.pallas_call(
        _gla_kernel,
        out_shape=jax.ShapeDtypeStruct((batch * seq, GLA_V), bf16),
        grid=(batch, nt),
        in_specs=[pl.BlockSpec((tc, W_GLA), lambda b, i: (b * nt + i, 0)),
                  _const_spec((128, GLA_QK)),
                  _const_spec((1, GLA_QK)),
                  _const_spec((1, GLA_V))],
        out_specs=pl.BlockSpec((tc, GLA_V), lambda b, i: (b * nt + i, 0)),
        scratch_shapes=[pltpu.VMEM((GLA_V, GLA_QK), f32),
                        pltpu.VMEM((tc, GLA_QK), f32),
                        pltpu.VMEM((tc, GLA_QK), bf16),
                        pltpu.VMEM((n_chunks, GLA_V, GLA_QK), f32),
                        pltpu.VMEM((n_chunks, 8, GLA_QK), f32),
                        pltpu.VMEM((n_chunks, GLA_V, GLA_QK), bf16),
                        pltpu.VMEM((tc, GLA_V), f32)],
        compiler_params=_cparams(2),
        name="gla",
    )(z_gla, wg, bg, no)


DN_W = DN_HEADS * DN_DK
DN_PAIRS = DN_HEADS // 2


def _stack_heads(x):
    lane = _iota(x.shape, 1)
    zero = jnp.zeros_like(x)
    return jnp.concatenate([jnp.where(lane < DN_DK, x, zero), jnp.where(lane >= DN_DK, x, zero)], axis=0)


def _dot_3x(a, b):
    a_hi = a.astype(bf16)
    b_hi = b.astype(bf16)
    a_lo = (a - a_hi.astype(f32)).astype(bf16)
    b_lo = (b - b_hi.astype(f32)).astype(bf16)
    return _dot(a_hi, b_hi) + (_dot(a_hi, b_lo) + _dot(a_lo, b_hi))


def _dn_pre_kernel(y_ref, ab_ref, alog_ref, dtb_ref,
                   a_ref, attn_ref, rhs_ref, qd_ref, kd_ref, egl_ref,
                   q_s, k_s, beta_s, g_s, gam_s, gamr_s):
    tc = y_ref.shape[0]
    c_len = DN_CHUNK

    sum_mat = _group_mean_matrix(DN_W, DN_DK) * float(DN_DK)
    q = y_ref[:, 0:DN_W]
    k = y_ref[:, DN_W:2 * DN_W]
    q_s[...] = q * lax.rsqrt(_group_stat(q * q, sum_mat) + NORM_EPS) * (DN_DK ** -0.5)
    k_s[...] = k * lax.rsqrt(_group_stat(k * k, sum_mat) + NORM_EPS)

    ab = ab_ref[...]
    lane_head = _iota((tc, DN_W), 1) // DN_DK
    a_rep = jnp.zeros((tc, DN_W), f32)
    b_rep = jnp.zeros((tc, DN_W), f32)
    for h in range(DN_HEADS):
        a_rep = jnp.where(lane_head == h, jnp.broadcast_to(ab[:, h:h + 1], (tc, DN_W)), a_rep)
        b_rep = jnp.where(lane_head == h,
                          jnp.broadcast_to(ab[:, DN_HEADS + h:DN_HEADS + h + 1], (tc, DN_W)), b_rep)
    beta_s[...] = _sigmoid(b_rep)
    g_s[...] = -jnp.exp(alog_ref[...]) * _softplus(a_rep + dtb_ref[...])

    r64 = _iota((c_len, c_len), 0)
    c64 = _iota((c_len, c_len), 1)
    tri = (r64 >= c64).astype(bf16)
    ones = jnp.ones((c_len, c_len), bf16)
    row = _iota((c_len, DN_W), 0)
    colj = _iota((c_len, DN_W), 1) % c_len
    incl = row >= colj
    strict = row > colj

    n_chunks = tc // c_len
    g_parts = _split_bf16(g_s[...], 3)
    zero_b = jnp.zeros((c_len, DN_W), bf16)
    total = lambda parts: functools.reduce(lambda a, b: a + b, reversed(parts))
    for c in range(n_chunks):
        rows = slice(c * c_len, (c + 1) * c_len)
        gam_s[rows, :] = total([_dot(tri, part[rows]) for part in g_parts])
    for c in range(n_chunks):
        rows = slice(c * c_len, (c + 1) * c_len)
        gamr_s[rows, :] = total([_dot(ones, jnp.where(row <= colj, part[rows], zero_b)) for part in g_parts])

    def chunk(c):
        rows = slice(c * c_len, (c + 1) * c_len)
        gam = gam_s[rows, :]
        gam_row = gamr_s[rows, :]
        dec_incl = jnp.exp(jnp.where(incl, gam - gam_row, -1e30))
        dec_strict = jnp.where(strict, dec_incl, 0.0)
        eg = jnp.exp(gam)
        g_last = gam[c_len - 1:c_len, :]
        q = q_s[rows, :]
        k = k_s[rows, :]
        beta = beta_s[rows, :]
        bv = beta * y_ref[rows, 2 * DN_W:3 * DN_W]
        bk = beta * eg * k
        kdec = k * jnp.exp(g_last - gam)
        qd_ref[rows, :] = (q * eg).astype(bf16)
        egl_ref[c] = jnp.broadcast_to(jnp.exp(g_last), (8, DN_W))
        qb = q.astype(bf16)
        kb = k.astype(bf16)
        for p in range(DN_PAIRS):
            ps = slice(p * 128, (p + 1) * 128)
            kbp = kb[:, ps]
            k_bd = _stack_heads(kbp)
            a_pair = beta[:, ps] * _dot_nt(kbp, k_bd) * dec_strict[:, ps]
            a_ref[c * DN_HEADS + 2 * p] = a_pair[:, :DN_CHUNK]
            a_ref[c * DN_HEADS + 2 * p + 1] = a_pair[:, DN_CHUNK:]
            attn_ref[c * DN_PAIRS + p] = (_dot_nt(qb[:, ps], k_bd) * dec_incl[:, ps]).astype(bf16)
            kd_ref[c * DN_PAIRS + p] = _stack_heads(kdec[:, ps]).T.astype(bf16)
            rhs_ref[c * DN_PAIRS + p] = jnp.concatenate(
                [_stack_heads(bv[:, ps]), _stack_heads(bk[:, ps])], axis=1)

    for c in range(n_chunks):
        chunk(c)


def _dn_pre(z_dn, alog, dtb, batch, seq):
    tc = SCAN_TILE
    nt = seq // tc
    t = batch * seq
    npt = (tc // DN_CHUNK) * DN_PAIRS
    n_pair = (t // DN_CHUNK) * DN_PAIRS
    scr = lambda: pltpu.VMEM((tc, DN_W), f32)
    tok = lambda: pl.BlockSpec((tc, DN_W), lambda b, i: (b * nt + i, 0))
    mat = lambda r, w: pl.BlockSpec((npt, r, w), lambda b, i: (b * nt + i, 0, 0))
    return pl.pallas_call(
        _dn_pre_kernel,
        out_shape=(jax.ShapeDtypeStruct((2 * n_pair, DN_CHUNK, DN_CHUNK), f32),
                   jax.ShapeDtypeStruct((n_pair, DN_CHUNK, 128), bf16),
                   jax.ShapeDtypeStruct((n_pair, 128, 256), f32),
                   jax.ShapeDtypeStruct((t, DN_W), bf16),
                   jax.ShapeDtypeStruct((n_pair, 128, 128), bf16),
                   jax.ShapeDtypeStruct((t // DN_CHUNK, 8, DN_W), f32)),
        grid=(batch, nt),
        in_specs=[pl.BlockSpec((tc, 3 * DN_W), lambda b, i: (b * nt + i, 0)),
                  pl.BlockSpec((tc, 128), lambda b, i: (b * nt + i, 4 * DN_W // 128)),
                  _const_spec((1, DN_W)),
                  _const_spec((1, DN_W))],
        out_specs=(pl.BlockSpec((2 * npt, DN_CHUNK, DN_CHUNK), lambda b, i: (b * nt + i, 0, 0)),
                   mat(DN_CHUNK, 128), mat(128, 256), tok(), mat(128, 128),
                   pl.BlockSpec((tc // DN_CHUNK, 8, DN_W), lambda b, i: (b * nt + i, 0, 0))),
        scratch_shapes=[scr(), scr(), scr(), scr(), scr(), scr()],
        compiler_params=_cparams(2),
        name="dn_pre",
    )(z_dn, z_dn, alog, dtb)


def _dn_solve_kernel(a_ref, t_ref):
    c_len = DN_CHUNK
    blk = DN_INV_BLOCK
    vshape = a_ref.shape[1:]

    def row_pair(ip, carry):
        i0 = 2 * ip
        i1 = i0 + 1
        for mb in range(c_len // blk):
            m0 = mb * blk

            def col_j(j, accs):
                a0 = a_ref[i0 * c_len + j]
                a1 = a_ref[i1 * c_len + j]
                ts = [t_ref[j * c_len + m0 + m] for m in range(blk)]
                return (tuple(accs[0][m] - a0 * ts[m] for m in range(blk)),
                        tuple(accs[1][m] - a1 * ts[m] for m in range(blk)))

            zeros = tuple(jnp.zeros(vshape, f32) for _ in range(blk))
            acc0, acc1 = lax.fori_loop(m0, i0, col_j, (zeros, zeros))
            a10 = a_ref[i1 * c_len + i0]
            for m in range(blk):
                t0 = acc0[m] + jnp.where(i0 == m0 + m, 1.0, 0.0)
                t_ref[i0 * c_len + m0 + m] = t0
                t_ref[i1 * c_len + m0 + m] = acc1[m] - a10 * t0 + jnp.where(i1 == m0 + m, 1.0, 0.0)
        return carry

    lax.fori_loop(0, c_len // 2, row_pair, 0)


def _dn_solve(a_t):
    n_ent, n_grp, lanes = a_t.shape
    g = min(n_grp, 8)
    spec = lambda: pl.BlockSpec((n_ent, g, lanes), lambda i: (0, i, 0), pipeline_mode=pl.Buffered(1))
    return pl.pallas_call(
        _dn_solve_kernel,
        out_shape=jax.ShapeDtypeStruct(a_t.shape, f32),
        grid=(n_grp // g,),
        in_specs=[spec()],
        out_specs=spec(),
        compiler_params=_cparams(1),
        name="dn_solve",
    )(a_t)


def _dn_scan_kernel(t_ref, attn_ref, rhs_ref, qd_ref, kd_ref, egl_ref, zg_ref, no_ref,
                    o_ref, st_ref, oraw_ref):
    tc = qd_ref.shape[0]
    c_len = DN_CHUNK
    pairs = range(DN_PAIRS)
    psl = [slice(p * 128, (p + 1) * 128) for p in pairs]

    @pl.when(pl.program_id(1) == 0)
    def _():
        st_ref[...] = jnp.zeros_like(st_ref)

    def solve(c):
        t_pair = [jnp.concatenate([t_ref[c * DN_HEADS + 2 * p], t_ref[c * DN_HEADS + 2 * p + 1]], axis=-1)
                  for p in pairs]
        return [_dot_3x(t_pair[p], rhs_ref[c * DN_PAIRS + p]) for p in pairs]

    n_chunks = tc // c_len
    sols = solve(0)
    for c in range(n_chunks):
        rows = slice(c * c_len, (c + 1) * c_len)
        qd = qd_ref[rows, :]
        eg_last = egl_ref[c][0:1, :]
        sts = [st_ref[p] for p in pairs]
        xq = [_dot(jnp.concatenate([sols[p][:, 128:].astype(bf16), qd[:, psl[p]]], axis=0),
                   sts[p].astype(bf16)) for p in pairs]
        nxt = solve(c + 1) if c + 1 < n_chunks else None
        u_bd = [_stack_heads((sols[p][:, :128] - xq[p][:c_len]).astype(bf16)) for p in pairs]
        au = [_dot(jnp.concatenate([attn_ref[c * DN_PAIRS + p], kd_ref[c * DN_PAIRS + p]], axis=0), u_bd[p])
              for p in pairs]
        for p in pairs:
            st_ref[p] = eg_last[:, psl[p]] * sts[p] + au[p][c_len:]
        oraw_ref[rows, :] = jnp.concatenate([xq[p][c_len:] + au[p][:c_len] for p in pairs], axis=-1)
        sols = nxt

    o = oraw_ref[...]
    ms = _group_stat(o * o, _group_mean_matrix(DN_W, DN_DV))
    zg = zg_ref[...]
    o_ref[...] = (o * lax.rsqrt(ms + NORM_EPS) * no_ref[...] * (zg * _sigmoid(zg))).astype(bf16)


def _dn_scan(t_all, attn, rhs, qd, kd, egl, z_dn, no, batch, seq):
    tc = SCAN_TILE
    nt = seq // tc
    npt = (tc // DN_CHUNK) * DN_PAIRS
    tok = lambda: pl.BlockSpec((tc, DN_W), lambda b, i: (b * nt + i, 0))
    mat = lambda r, w: pl.BlockSpec((npt, r, w), lambda b, i: (b * nt + i, 0, 0))
    return pl.pallas_call(
        _dn_scan_kernel,
        out_shape=jax.ShapeDtypeStruct((batch * seq, DN_W), bf16),
        grid=(batch, nt),
        in_specs=[pl.BlockSpec((2 * npt, DN_CHUNK, DN_CHUNK), lambda b, i: (b * nt + i, 0, 0)),
                  mat(DN_CHUNK, 128), mat(128, 256), tok(), mat(128, 128),
                  pl.BlockSpec((tc // DN_CHUNK, 8, DN_W), lambda b, i: (b * nt + i, 0, 0)),
                  pl.BlockSpec((tc, DN_W), lambda b, i: (b * nt + i, 3)),
                  _const_spec((1, DN_W))],
        out_specs=tok(),
        scratch_shapes=[pltpu.VMEM((DN_PAIRS, 128, 128), f32),
                        pltpu.VMEM((tc, DN_W), f32)],
        compiler_params=_cparams(2),
        name="dn_scan",
    )(t_all, attn, rhs, qd, kd, egl, z_dn, no)


def _dn(z_dn, alog, dtb, no, batch, seq):
    a_all, attn, rhs, qd, kd, egl = _dn_pre(z_dn, alog, dtb, batch, seq)
    n_prob = a_all.shape[0]
    n_ent = DN_CHUNK * DN_CHUNK
    a_t = a_all.reshape(n_prob, n_ent).T.reshape(n_ent, n_prob // 128, 128)
    t_all = _dn_solve(a_t).reshape(n_ent, n_prob).T.reshape(n_prob, DN_CHUNK, DN_CHUNK)
    return _dn_scan(t_all, attn, rhs, qd, kd, egl, z_dn, no, batch, seq)


def _merge_kernel(x_ref, g_ref, oa_ref, ob_ref, oc_ref, od_ref, wg_ref, wb_ref, wo_ref, out_ref):
    x = x_ref[...]
    h = _rms(x, g_ref[...]).astype(bf16)
    merged = jnp.zeros(x.shape, f32)
    for n, o_ref in enumerate((oa_ref, ob_ref, oc_ref, od_ref)):
        gate = _sigmoid(_dot(h, wg_ref[:, n * D_MODEL:(n + 1) * D_MODEL]))
        merged = merged + gate * _dot(o_ref[...], wb_ref[n])
    out_ref[...] = x + _dot(merged.astype(bf16), wo_ref[...])


def _merge(x2, g, oa, ob, oc, od, wg, wb, wo):
    t = x2.shape[0]
    tm = TOKEN_TILE
    tok = lambda w: pl.BlockSpec((tm, w), lambda i: (i, 0))
    return pl.pallas_call(
        _merge_kernel,
        out_shape=jax.ShapeDtypeStruct((t, D_MODEL), f32),
        grid=(t // tm,),
        in_specs=[tok(D_MODEL), _const_spec((1, D_MODEL)),
                  tok(BRANCH_WIDTH), tok(BRANCH_WIDTH), tok(BRANCH_WIDTH), tok(BRANCH_WIDTH),
                  _const_spec((D_MODEL, N_BRANCH * D_MODEL)),
                  _const_spec((N_BRANCH, BRANCH_WIDTH, D_MODEL)),
                  _const_spec((D_MODEL, D_MODEL))],
        out_specs=tok(D_MODEL),
        compiler_params=_cparams(1),
        name="merge",
    )(x2, g, oa, ob, oc, od, wg, wb, wo)


def _memkv_kernel(mem_ref, g_ref, wk_ref, wv_ref, k_ref, v_ref):
    mn = _rms(mem_ref[...], g_ref[...]).astype(bf16)
    k_ref[...] = _dot(mn, wk_ref[...]).astype(bf16)
    v_ref[...] = _dot(mn, wv_ref[...]).astype(bf16)


def _memkv(mem2, g, wk, wv, mem_len):
    t = mem2.shape[0]
    blk = lambda: pl.BlockSpec((mem_len, D_MODEL), lambda i: (i, 0))
    return pl.pallas_call(
        _memkv_kernel,
        out_shape=(jax.ShapeDtypeStruct((t, D_MODEL), bf16), jax.ShapeDtypeStruct((t, D_MODEL), bf16)),
        grid=(t // mem_len,),
        in_specs=[blk(), _const_spec((1, D_MODEL)),
                  _const_spec((D_MODEL, D_MODEL)), _const_spec((D_MODEL, D_MODEL))],
        out_specs=(blk(), blk()),
        compiler_params=_cparams(1),
        name="mem_kv",
    )(mem2, g, wk, wv)


def _xattn_kernel(x_ref, g_ref, k_ref, v_ref, wq_ref, wo_ref, out_ref):
    x = x_ref[...]
    h = _rms(x, g_ref[...]).astype(bf16)
    q = (_dot(h, wq_ref[...]) * (XATTN_DIM ** -0.5)).astype(bf16)
    hsl = [slice(hh * XATTN_DIM, (hh + 1) * XATTN_DIM) for hh in range(XATTN_HEADS)]
    scores = [_dot_nt(q[:, sl], k_ref[:, sl]) for sl in hsl]
    probs = []
    for s in scores:
        p = jnp.exp(s - jnp.max(s, axis=-1, keepdims=True))
        probs.append((p / jnp.sum(p, axis=-1, keepdims=True)).astype(bf16))
    o = jnp.concatenate([_dot(p, v_ref[:, sl]).astype(bf16) for p, sl in zip(probs, hsl)], axis=-1)
    out_ref[...] = x + _dot(o, wo_ref[...])


def _xattn(x2, g, kmem, vmem, wq, wo, batch, seq, mem_len):
    tm = TOKEN_TILE
    nt = seq // tm
    tok = pl.BlockSpec((tm, D_MODEL), lambda b, i: (b * nt + i, 0))
    memb = lambda: pl.BlockSpec((mem_len, D_MODEL), lambda b, i: (b, 0))
    return pl.pallas_call(
        _xattn_kernel,
        out_shape=jax.ShapeDtypeStruct((batch * seq, D_MODEL), f32),
        grid=(batch, nt),
        in_specs=[tok, _const_spec((1, D_MODEL)), memb(), memb(),
                  _const_spec((D_MODEL, D_MODEL)), _const_spec((D_MODEL, D_MODEL))],
        out_specs=tok,
        compiler_params=_cparams(2),
        name="xattn",
    )(x2, g, kmem, vmem, wq, wo)


def _mlp_kernel(x_ref, g_ref, wu_ref, wd_ref, gf_ref, out_ref, *, final):
    x = x_ref[...]
    h = _rms(x, g_ref[...]).astype(bf16)
    acc = x
    for c in range(MLP_HIDDEN // MLP_CHUNK):
        sl = slice(c * MLP_CHUNK, (c + 1) * MLP_CHUNK)
        a = jnp.maximum(_dot(h, wu_ref[:, sl]), 0.0)
        acc = acc + _dot((a * a).astype(bf16), wd_ref[sl, :])
    if final:
        acc = _rms(acc, gf_ref[...])
    out_ref[...] = acc


def _mlp(x2, g, wu, wd, gf, final):
    t = x2.shape[0]
    tm = TOKEN_TILE
    tok = pl.BlockSpec((tm, D_MODEL), lambda i: (i, 0))
    return pl.pallas_call(
        functools.partial(_mlp_kernel, final=final),
        out_shape=jax.ShapeDtypeStruct((t, D_MODEL), f32),
        grid=(t // tm,),
        in_specs=[tok, _const_spec((1, D_MODEL)),
                  _const_spec((D_MODEL, MLP_HIDDEN)), _const_spec((MLP_HIDDEN, D_MODEL)),
                  _const_spec((1, D_MODEL))],
        out_specs=tok,
        compiler_params=_cparams(1),
        name="mlp",
    )(x2, g, wu, wd, gf)


def _pad_cols(w, width):
    return jnp.pad(w, ((0, 0), (0, width - w.shape[1])))


def _prep_w_in(w_in):
    col = lambda i: w_in[:, IN_OFFS[i]:IN_OFFS[i + 1]]
    zeros = lambda n: jnp.zeros((D_MODEL, n), w_in.dtype)
    (cq, ckv, kpe, sg_uv, gq, gk, gv, g_lr, g_og, dq, dk, dv, da, db, dz) = (col(i) for i in range(15))
    small = jnp.concatenate([
        cq, ckv, zeros(MLA_NOPE), kpe, zeros(MLA_HEAD_PAD - MLA_NOPE - MLA_ROPE),
        sg_uv,
        gq, gk, gv, g_og, _pad_cols(g_lr, 128),
        dq, dk, dv, dz, _pad_cols(jnp.concatenate([da, db], axis=1), 128),
    ], axis=1)
    assert small.shape[1] == W_SMALL
    return small.astype(bf16), col(15).astype(bf16)


def _rope_inv_freq():
    inv_freq = ROPE_THETA ** (-jnp.arange(0, MLA_ROPE, 2, dtype=f32) / MLA_ROPE)
    return inv_freq[:, None]


def kernel(x, mem, positions, norm_mix, w_in, mla_norm_q, mla_norm_kv, mla_w_uq, mla_w_ukv, sg_ln_g, sg_ln_b, sg_w, sg_b, gla_w_gate, gla_b_gate, gla_norm, dn_conv, dn_a_log, dn_dt_bias, dn_norm, w_branch, w_out, norm_xattn, norm_mem, xattn_wq, xattn_wk, xattn_wv, xattn_wo, norm_mlp, w_up, w_down, norm_final):
    batch, seq, _ = x.shape
    mem_len = mem.shape[1]
    depth = w_in.shape[0]
    assert seq % TOKEN_TILE == 0 and seq % SCAN_TILE == 0 and seq % ATTN_TILE == 0
    x2 = x.reshape(batch * seq, D_MODEL)
    mem2 = mem.reshape(batch * mem_len, D_MODEL)
    pos2 = positions.reshape(batch * seq // TOKEN_TILE, 1, TOKEN_TILE)
    invf = _rope_inv_freq()
    row = lambda v: v[None, :]

    for l in range(depth):
        w_small, w_gates = _prep_w_in(w_in[l])
        z_mla, z_sg, z_gla, z_dn = _inproj(x2, row(norm_mix[l]), w_small, dn_conv[l], seq)

        wuq = jnp.pad(mla_w_uq[l].reshape(MLA_Q_RANK, MLA_HEADS, MLA_NOPE + MLA_ROPE),
                      ((0, 0), (0, 0), (0, MLA_HEAD_PAD - MLA_NOPE - MLA_ROPE)))
        wuq = wuq.reshape(MLA_Q_RANK, MLA_HEADS * MLA_HEAD_PAD).astype(bf16)
        wukv = mla_w_ukv[l].reshape(MLA_KV_RANK, MLA_HEADS, MLA_NOPE + MLA_V)
        wuk = jnp.pad(wukv[:, :, :MLA_NOPE], ((0, 0), (0, 0), (0, MLA_HEAD_PAD - MLA_NOPE)))
        wuk = wuk.reshape(MLA_KV_RANK, MLA_HEADS * MLA_HEAD_PAD).astype(bf16)
        wuv = wukv[:, :, MLA_NOPE:].reshape(MLA_KV_RANK, MLA_HEADS * MLA_V).astype(bf16)
        q_a, k_a, v_a = _mla_pre(z_mla, pos2, invf, row(mla_norm_q[l]), row(mla_norm_kv[l]), wuq, wuk, wuv,
                                 batch, seq)
        o_a = _mla_attn(q_a, k_a, v_a, batch, seq)

        o_b = _sg(z_sg, row(sg_ln_g[l]), row(sg_ln_b[l]), sg_w[l], sg_b[l].T)

        wg = jnp.pad(gla_w_gate[l], ((0, 128 - GLA_GATE_RANK), (0, 0))).astype(bf16)
        o_c = _gla(z_gla, wg, row(gla_b_gate[l]), row(jnp.tile(gla_norm[l], GLA_HEADS)), batch, seq)

        o_d = _dn(z_dn, row(jnp.repeat(dn_a_log[l], DN_DK)), row(jnp.repeat(dn_dt_bias[l], DN_DK)),
                  row(jnp.tile(dn_norm[l], DN_HEADS)), batch, seq)

        x2 = _merge(x2, row(norm_mix[l]), o_a, o_b, o_c, o_d, w_gates,
                    w_branch[l].astype(bf16), w_out[l].astype(bf16))

        k_m, v_m = _memkv(mem2, row(norm_mem[l]), xattn_wk[l].astype(bf16), xattn_wv[l].astype(bf16), mem_len)
        x2 = _xattn(x2, row(norm_xattn[l]), k_m, v_m, xattn_wq[l].astype(bf16), xattn_wo[l].astype(bf16),
                    batch, seq, mem_len)
        x2 = _mlp(x2, row(norm_mlp[l]), w_up[l].astype(bf16), w_down[l].astype(bf16), row(norm_final),
                  final=(l == depth - 1))
    return x2.reshape(batch, seq, D_MODEL)
```

```python
import functools

import jax
import jax.numpy as jnp
import numpy as np
from jax import lax
from jax.experimental import pallas as pl
from jax.experimental.pallas import tpu as pltpu

f32 = jnp.float32
bf16 = jnp.bfloat16
HIGHEST = lax.Precision.HIGHEST

D_MODEL = 1024
NORM_EPS = 1e-6
N_BRANCH = 4
BRANCH_WIDTH = 256

MLA_HEADS = 4
MLA_Q_RANK = 256
MLA_KV_RANK = 128
MLA_NOPE = 64
MLA_ROPE = 32
MLA_V = 64
ROPE_THETA = 10000.0
MLA_HEAD_PAD = 128
MLA_VT_ROWS = 80

SG_GROUPS = 4
SG_WIDTH = 256
SG_CHUNK = 128

GLA_HEADS = 4
GLA_DK = 32
GLA_DV = 64
GLA_GATE_RANK = 16
GLA_GATE_TEMP = 16.0
GLA_CHUNK = 64

DN_HEADS = 4
DN_DK = 64
DN_DV = 64
DN_CONV = 4
DN_CHUNK = 64
DN_INV_BLOCK = 16

XATTN_HEADS = 4
XATTN_DIM = D_MODEL // XATTN_HEADS
MLP_HIDDEN = 4 * D_MODEL

IN_SIZES = (
    MLA_Q_RANK, MLA_KV_RANK, MLA_ROPE,
    2 * SG_WIDTH,
    GLA_HEADS * GLA_DK, GLA_HEADS * GLA_DK, GLA_HEADS * GLA_DV,
    GLA_GATE_RANK, GLA_HEADS * GLA_DV,
    DN_HEADS * DN_DK, DN_HEADS * DN_DK, DN_HEADS * DN_DV,
    DN_HEADS, DN_HEADS, DN_HEADS * DN_DV,
    N_BRANCH * D_MODEL,
)
IN_OFFS = tuple(int(v) for v in np.cumsum((0,) + IN_SIZES))

W_MLA = 512
W_SG = 512
W_GLA = 896
W_DN = 1152
W_SMALL = W_MLA + W_SG + W_GLA + W_DN

TOKEN_TILE = 512
ATTN_TILE = 256
ATTN_KEY_TILE = 128
SCAN_TILE = 512
MLP_CHUNK = 1024
VMEM_LIMIT = 56 * 1024 * 1024


def _cparams(n_axes):
    return pltpu.CompilerParams(dimension_semantics=("arbitrary",) * n_axes,
                                vmem_limit_bytes=VMEM_LIMIT)


def _const_spec(shape):
    nd = len(shape)
    return pl.BlockSpec(shape, lambda *_: (0,) * nd, pipeline_mode=pl.Buffered(1))


def _rms(x, g):
    return x * lax.rsqrt(jnp.mean(x * x, axis=-1, keepdims=True) + NORM_EPS) * g


def _sigmoid(x):
    return 1.0 / (1.0 + jnp.exp(-x))


def _softplus(x):
    return jnp.maximum(x, 0.0) + jnp.log1p(jnp.exp(-jnp.abs(x)))


def _dot(a, b):
    return jnp.dot(a, b, preferred_element_type=f32)


def _split_bf16(x, terms):
    parts = []
    for _ in range(terms):
        p = x.astype(bf16)
        parts.append(p)
        x = x - p.astype(f32)
    return parts


def _dot_sel(sel, x, terms=3):
    sel = sel.astype(bf16)
    parts = [_dot(sel, p) for p in _split_bf16(x, terms)]
    return functools.reduce(lambda a, b: a + b, reversed(parts))


def _group_stat(x, mat, terms=2):
    mat = mat.astype(bf16)
    parts = [_dot(p, mat) for p in _split_bf16(x, terms)]
    return functools.reduce(lambda a, b: a + b, reversed(parts))


def _dot_nt(a, b):
    return lax.dot_general(a, b, (((1,), (1,)), ((), ())), preferred_element_type=f32)


def _dot_tn(a, b):
    return lax.dot_general(a, b, (((0,), (0,)), ((), ())), preferred_element_type=f32)


def _iota(shape, dim):
    return lax.broadcasted_iota(jnp.int32, shape, dim)


def _group_mean_matrix(width, group):
    r = _iota((width, width), 0) // group
    c = _iota((width, width), 1) // group
    return jnp.where(r == c, 1.0 / group, 0.0).astype(f32)


DN_CONV_TAPS = 4
DN_CONV_COLS = 768
DN_CONV_PAD = 8


def _inproj_kernel(x_ref, g_ref, w_ref, cw_ref, mla_ref, sg_ref, gla_ref, dn_ref, xbuf_ref, *,
                   tiles_per_seq):
    @pl.when(pl.program_id(0) % tiles_per_seq == 0)
    def _():
        xbuf_ref[0:DN_CONV_PAD, :] = jnp.zeros((DN_CONV_PAD, DN_CONV_COLS), f32)

    h = _rms(x_ref[...], g_ref[...]).astype(bf16)
    tm = h.shape[0]
    z_dn = _dot(h, w_ref[:, W_SMALL - W_DN:])
    dn_ref[:, DN_CONV_COLS:] = z_dn[:, DN_CONV_COLS:]
    xbuf_ref[DN_CONV_PAD:DN_CONV_PAD + tm, :] = z_dn[:, :DN_CONV_COLS]
    y = jnp.zeros((tm, DN_CONV_COLS), f32)
    for kk in range(DN_CONV_TAPS):
        y = y + cw_ref[kk:kk + 1, :] * xbuf_ref[pl.ds(DN_CONV_PAD - (DN_CONV_TAPS - 1) + kk, tm), :]
    xbuf_ref[0:DN_CONV_PAD, :] = xbuf_ref[tm:tm + DN_CONV_PAD, :]
    dn_ref[:, :DN_CONV_COLS] = y * _sigmoid(y)
    lo = 0
    for ref, width in ((mla_ref, W_MLA), (sg_ref, W_SG), (gla_ref, W_GLA)):
        ref[...] = _dot(h, w_ref[:, lo:lo + width])
        lo += width


def _inproj(x2, g, w_small, conv_w, seq):
    t = x2.shape[0]
    tm = TOKEN_TILE
    outs = tuple(jax.ShapeDtypeStruct((t, w), f32) for w in (W_MLA, W_SG, W_GLA, W_DN))
    return pl.pallas_call(
        functools.partial(_inproj_kernel, tiles_per_seq=seq // tm),
        out_shape=outs,
        grid=(t // tm,),
        in_specs=[pl.BlockSpec((tm, D_MODEL), lambda i: (i, 0)),
                  _const_spec((1, D_MODEL)),
                  _const_spec((D_MODEL, W_SMALL)),
                  _const_spec((DN_CONV_TAPS, DN_CONV_COLS))],
        out_specs=tuple(pl.BlockSpec((tm, w), lambda i: (i, 0)) for w in (W_MLA, W_SG, W_GLA, W_DN)),
        scratch_shapes=[pltpu.VMEM((tm + 2 * DN_CONV_PAD, DN_CONV_COLS), f32)],
        compiler_params=_cparams(1),
        name="inproj",
    )(x2, g, w_small, conv_w)


def _mla_pre_kernel(z_ref, pos_ref, invf_ref, nq_ref, nkv_ref, wuq_ref, wuk_ref, wuv_ref,
                    qt_ref, k_ref, vt_ref):
    z = z_ref[...]
    tm = z.shape[0]
    cq = z[:, :MLA_Q_RANK]
    ckv = z[:, MLA_Q_RANK:MLA_Q_RANK + MLA_KV_RANK]
    kpe = z[:, MLA_Q_RANK + MLA_KV_RANK:]

    half = MLA_ROPE // 2
    ang_t = invf_ref[...] * pos_ref[...].astype(f32)
    f_row = _iota((half, MLA_HEAD_PAD), 0)
    f_lane = _iota((half, MLA_HEAD_PAD), 1)
    place = ((f_lane == MLA_NOPE + f_row) | (f_lane == MLA_NOPE + half + f_row)).astype(bf16)

    def to_lanes(x_t):
        parts = [_dot_tn(p, place) for p in _split_bf16(x_t, 3)]
        return functools.reduce(lambda a, b: a + b, reversed(parts))

    cos = to_lanes(jnp.cos(ang_t))
    sin = to_lanes(jnp.sin(ang_t))
    lane = _iota((tm, MLA_HEAD_PAD), 1)
    in_x1 = (lane >= MLA_NOPE) & (lane < MLA_NOPE + half)
    in_x2 = (lane >= MLA_NOPE + half) & (lane < MLA_NOPE + MLA_ROPE)
    c_tab = jnp.where(in_x1 | in_x2, cos, 1.0)
    s1_tab = jnp.where(in_x1, -sin, 0.0)
    s2_tab = jnp.where(in_x2, sin, 0.0)

    def rope(t):
        return (t * c_tab + pltpu.roll(t, MLA_HEAD_PAD - half, 1) * s1_tab
                + pltpu.roll(t, half, 1) * s2_tab)

    scale = (MLA_NOPE + MLA_ROPE) ** -0.5
    q = _dot(_rms(cq, nq_ref[...]).astype(bf16), wuq_ref[...])
    kvn = _rms(ckv, nkv_ref[...]).astype(bf16)
    kn = _dot(kvn, wuk_ref[...])
    v_t = _dot(kvn, wuv_ref[...]).T
    kpe_rot = rope(kpe)
    q_rot = []
    for h in range(MLA_HEADS):
        sl = slice(h * MLA_HEAD_PAD, (h + 1) * MLA_HEAD_PAD)
        q_rot.append(rope(q[:, sl]) * scale)
        k_ref[:, sl] = (kn[:, sl] + kpe_rot).astype(bf16)
    qt_ref[...] = jnp.concatenate(q_rot, axis=-1).T.astype(bf16)
    pad = jnp.concatenate([jnp.ones((1, tm), f32), jnp.zeros((MLA_VT_ROWS - MLA_V - 1, tm), f32)], axis=0)
    rows = []
    for h in range(MLA_HEADS):
        rows += [v_t[h * MLA_V:(h + 1) * MLA_V], pad]
    vt_ref[...] = jnp.concatenate(rows, axis=0).astype(bf16)


def _mla_pre(z_mla, pos2, invf, nq, nkv, wuq, wuk, wuv, batch, seq):
    t = z_mla.shape[0]
    tm = TOKEN_TILE
    nt = seq // tm
    hp = MLA_HEADS * MLA_HEAD_PAD
    hv = MLA_HEADS * MLA_V
    hvt = MLA_HEADS * MLA_VT_ROWS
    return pl.pallas_call(
        _mla_pre_kernel,
        out_shape=(jax.ShapeDtypeStruct((batch, hp, seq), bf16),
                   jax.ShapeDtypeStruct((t, hp), bf16),
                   jax.ShapeDtypeStruct((batch, hvt, seq), bf16)),
        grid=(t // tm,),
        in_specs=[pl.BlockSpec((tm, W_MLA), lambda i: (i, 0)),
                  pl.BlockSpec((None, 1, tm), lambda i: (i, 0, 0)),
                  _const_spec((MLA_ROPE // 2, 1)),
                  _const_spec((1, MLA_Q_RANK)),
                  _const_spec((1, MLA_KV_RANK)),
                  _const_spec((MLA_Q_RANK, hp)),
                  _const_spec((MLA_KV_RANK, hp)),
                  _const_spec((MLA_KV_RANK, hv))],
        out_specs=(pl.BlockSpec((None, hp, tm), lambda i: (i // nt, 0, i % nt)),
                   pl.BlockSpec((tm, hp), lambda i: (i, 0)),
                   pl.BlockSpec((None, hvt, tm), lambda i: (i // nt, 0, i % nt))),
        compiler_params=_cparams(1),
        name="mla_pre",
    )(z_mla, pos2, invf, nq, nkv, wuq, wuk, wuv)


def _mla_attn_kernel(qt_ref, k_ref, vt_ref, o_ref, sa_ref, sb_ref):
    i = pl.program_id(1)
    tq = qt_ref.shape[1]
    tk = sa_ref.shape[1]
    per = tq // tk
    key = _iota((tk, tq), 0)
    qry = _iota((tk, tq), 1)

    def issue_scores(b, buf):
        start = pl.multiple_of(b * tk, tk)
        cms = []
        for h in range(MLA_HEADS):
            hsl = slice(h * MLA_HEAD_PAD, (h + 1) * MLA_HEAD_PAD)
            s = _dot(k_ref[pl.ds(start, tk), hsl], qt_ref[hsl, :])
            buf[h] = s
            cms.append(jnp.max(s, axis=0, keepdims=True))
        return tuple(cms)

    def consume(b, buf, cms, state, diag_offset=None):
        start = pl.multiple_of(b * tk, tk)
        probs = []
        for h in range(MLA_HEADS):
            s = buf[h]
            if diag_offset is None:
                cm = cms[h]
            else:
                s = jnp.where(key + diag_offset <= qry, s, -jnp.inf)
                cm = jnp.max(s, axis=0, keepdims=True)
            m_new = jnp.maximum(state[h][0], cm)
            probs.append((m_new, jnp.exp(s - m_new).astype(bf16)))
        new = []
        for h in range(MLA_HEADS):
            vsl = slice(h * MLA_VT_ROWS, (h + 1) * MLA_VT_ROWS)
            m, acc = state[h]
            m_new, p = probs[h]
            acc = jnp.exp(m - m_new) * acc + _dot(vt_ref[vsl, pl.ds(start, tk)], p)
            new.append((m_new, acc))
        return tuple(new)

    def body(t, carry):
        state, cm_a = carry
        cm_b = issue_scores(per * t + 1, sb_ref)
        state = consume(per * t, sa_ref, cm_a, state)
        cm_a = issue_scores(per * t + 2, sa_ref)
        state = consume(per * t + 1, sb_ref, cm_b, state)
        return state, cm_a

    assert per == 2
    init = tuple((jnp.full((1, tq), -jnp.inf, f32), jnp.zeros((MLA_VT_ROWS, tq), f32))
                 for _ in range(MLA_HEADS))
    carry = (init, issue_scores(0, sa_ref))
    carry = lax.fori_loop(0, i // 4, lambda u, c: body(4 * u + 3, body(4 * u + 2, body(4 * u + 1, body(4 * u, c)))),
                          carry)
    t4 = 4 * (i // 4)
    carry = lax.fori_loop(0, (i % 4) // 2, lambda _, c: body(t4 + 1, body(t4, c)), carry)
    state, _ = lax.fori_loop(0, i % 2, lambda _, c: body(i - 1, c), carry)
    issue_scores(per * i + 1, sb_ref)
    state = consume(per * i, sa_ref, None, state, diag_offset=0)
    state = consume(per * i + 1, sb_ref, None, state, diag_offset=tk)
    o_t = jnp.concatenate([acc[:MLA_V] / acc[MLA_V:MLA_V + 1] for _, acc in state], axis=0)
    o_ref[...] = o_t.T.astype(bf16)


def _mla_attn(q_t, k, v_t, batch, seq):
    tq = ATTN_TILE
    nq = seq // tq
    hp = MLA_HEADS * MLA_HEAD_PAD
    hv = MLA_HEADS * MLA_V
    hvt = MLA_HEADS * MLA_VT_ROWS
    return pl.pallas_call(
        _mla_attn_kernel,
        out_shape=jax.ShapeDtypeStruct((batch * seq, hv), bf16),
        grid=(batch, nq),
        in_specs=[pl.BlockSpec((None, hp, tq), lambda b, i: (b, 0, i)),
                  pl.BlockSpec((seq, hp), lambda b, i: (b, 0)),
                  pl.BlockSpec((None, hvt, seq), lambda b, i: (b, 0, 0))],
        out_specs=pl.BlockSpec((tq, hv), lambda b, i: (b * nq + i, 0)),
        scratch_shapes=[pltpu.VMEM((MLA_HEADS, ATTN_KEY_TILE, tq), f32),
                        pltpu.VMEM((MLA_HEADS, ATTN_KEY_TILE, tq), f32)],
        compiler_params=_cparams(2),
        name="mla_attn",
    )(q_t, k, v_t)


def _sg_kernel(z_ref, lng_ref, lnb_ref, ws_ref, bst_ref, o_ref):
    z = z_ref[...]
    tm = z.shape[0]
    uv = 0.5 * z * (1.0 + jnp.tanh(np.sqrt(2.0 / np.pi) * (z + 0.044715 * (z * z * z))))
    u = uv[:, :SG_WIDTH]
    v = uv[:, SG_WIDTH:]
    mu = jnp.mean(v, axis=-1, keepdims=True)
    vc = v - mu
    vn = vc * lax.rsqrt(jnp.mean(vc * vc, axis=-1, keepdims=True) + NORM_EPS)
    vn = (vn * lng_ref[...] + lnb_ref[...]).astype(bf16)

    gw = SG_WIDTH // SG_GROUPS
    lane_group = _iota((SG_CHUNK, SG_WIDTH), 1) // gw
    tril = _iota((SG_CHUNK, SG_CHUNK), 0) >= _iota((SG_CHUNK, SG_CHUNK), 1)
    bias = jnp.zeros((SG_CHUNK, SG_WIDTH), f32)
    wm = []
    for g in range(SG_GROUPS):
        wm.append(jnp.where(tril, ws_ref[g], 0.0).astype(bf16))
        bias = jnp.where(lane_group == g,
                         jnp.broadcast_to(bst_ref[:, g:g + 1], (SG_CHUNK, SG_WIDTH)), bias)
    w_cat = jnp.concatenate(wm, axis=1)
    zero_b = jnp.zeros((SG_CHUNK, SG_WIDTH), bf16)
    for c in range(tm // SG_CHUNK):
        rows = slice(c * SG_CHUNK, (c + 1) * SG_CHUNK)
        vch = vn[rows]
        v_bd = jnp.concatenate([jnp.where(lane_group == g, vch, zero_b) for g in range(SG_GROUPS)], axis=0)
        o_ref[rows, :] = (u[rows] * (bias + _dot(w_cat, v_bd))).astype(bf16)


def _sg(z_sg, lng, lnb, ws, bst):
    t = z_sg.shape[0]
    tm = TOKEN_TILE
    return pl.pallas_call(
        _sg_kernel,
        out_shape=jax.ShapeDtypeStruct((t, SG_WIDTH), bf16),
        grid=(t // tm,),
        in_specs=[pl.BlockSpec((tm, W_SG), lambda i: (i, 0)),
                  _const_spec((1, SG_WIDTH)),
                  _const_spec((1, SG_WIDTH)),
                  _const_spec((SG_GROUPS, SG_CHUNK, SG_CHUNK)),
                  _const_spec((SG_CHUNK, SG_GROUPS))],
        out_specs=pl.BlockSpec((tm, SG_WIDTH), lambda i: (i, 0)),
        compiler_params=_cparams(1),
        name="spatial_gating",
    )(z_sg, lng, lnb, ws, bst)


GLA_QK = GLA_HEADS * GLA_DK
GLA_V = GLA_HEADS * GLA_DV
GLA_SUB = 16


def _gla_kernel(z_ref, wg_ref, bg_ref, no_ref, o_ref,
                st_ref, cum_ref, qd_ref, upd_ref, dl_ref, stc_ref, oraw_ref):
    tc = z_ref.shape[0]
    c_len = GLA_CHUNK
    n_chunks = tc // c_len

    @pl.when(pl.program_id(1) == 0)
    def _():
        st_ref[...] = jnp.zeros_like(st_ref)

    zg = _dot(z_ref[:, 768:896].astype(bf16), wg_ref[...]) + bg_ref[...]
    log_a = (jnp.minimum(zg, 0.0) - jnp.log1p(jnp.exp(-jnp.abs(zg)))) * (1.0 / GLA_GATE_TEMP)

    tri = (_iota((c_len, c_len), 0) >= _iota((c_len, c_len), 1)).astype(f32)
    for c in range(n_chunks):
        cum_ref[c * c_len:(c + 1) * c_len, :] = _dot_sel(tri, log_a[c * c_len:(c + 1) * c_len])

    expand = (_iota((GLA_QK, GLA_V), 0) // GLA_DK == _iota((GLA_QK, GLA_V), 1) // GLA_DV).astype(bf16)
    st_mask = (_iota((GLA_V, GLA_QK), 0) // GLA_DV == _iota((GLA_V, GLA_QK), 1) // GLA_DK).astype(f32)
    sub = GLA_SUB
    row = _iota((sub, GLA_QK), 0)

    def intra(c):
        base = c * c_len
        rows = slice(base, base + c_len)
        cum = cum_ref[rows, :]
        q = z_ref[rows, 0:128] * (GLA_DK ** -0.5)
        k = z_ref[rows, 128:256]
        v = z_ref[rows, 256:512]
        v_t = v.T.astype(bf16)
        qd_ref[rows, :] = (q * jnp.exp(cum)).astype(bf16)
        last = cum[c_len - 1:c_len, :]
        dl_ref[c] = jnp.broadcast_to(jnp.exp(last), (8, GLA_QK))
        upd_ref[c] = _dot(v_t, (k * jnp.exp(last - cum)).astype(bf16)) * st_mask

        n_sub = c_len // sub
        parts = [None]
        for sb in range(1, n_sub):
            r0 = sb * sub
            kx = k[0:r0] * jnp.exp(cum[r0 - 1:r0, :] - cum[0:r0])
            parts.append((_dot(v_t[:, 0:r0], kx.astype(bf16)) * st_mask).astype(bf16))
        rs = []
        for sb in range(n_sub):
            r0 = sb * sub
            q_i = q[r0:r0 + sub]
            cum_i = cum[r0:r0 + sub]
            ts = []
            for jj in range(sub):
                kj = z_ref[base + r0 + jj:base + r0 + jj + 1, 128:256]
                cj = cum_ref[base + r0 + jj:base + r0 + jj + 1, :]
                dec = jnp.exp(jnp.where(row >= jj, cum_i - cj, -1e30))
                ts.append((q_i * kj * dec).astype(bf16))
            rs.append(_dot(jnp.concatenate(ts, axis=0), expand))
        blocks = []
        for sb in range(n_sub):
            r0 = sb * sub
            o_sb = jnp.zeros((sub, GLA_V), f32)
            for jj in range(sub):
                o_sb = o_sb + rs[sb][jj * sub:(jj + 1) * sub] * z_ref[base + r0 + jj:base + r0 + jj + 1, 256:512]
            if sb > 0:
                qx = q[r0:r0 + sub] * jnp.exp(cum[r0:r0 + sub] - cum[r0 - 1:r0, :])
                o_sb = o_sb + _dot_nt(qx.astype(bf16), parts[sb])
            blocks.append(o_sb)
        oraw_ref[rows, :] = jnp.concatenate(blocks, axis=0)

    for c in range(n_chunks):
        intra(c)

    def recur(c, st):
        stc_ref[c] = st.astype(bf16)
        return dl_ref[c][0:1, :] * st + upd_ref[c]

    st_ref[...] = lax.fori_loop(0, n_chunks, recur, st_ref[...])
    for c in range(n_chunks):
        rows = slice(c * c_len, (c + 1) * c_len)
        oraw_ref[rows, :] += _dot_nt(qd_ref[rows, :], stc_ref[c])

    o = oraw_ref[...]
    ms = _group_stat(o * o, _group_mean_matrix(GLA_V, GLA_DV))
    og = z_ref[:, 512:768]
    o_ref[...] = (o * lax.rsqrt(ms + NORM_EPS) * no_ref[...] * (og * _sigmoid(og))).astype(bf16)


def _gla(z_gla, wg, bg, no, batch, seq):
    tc = SCAN_TILE
    nt = seq // tc
    n_chunks = tc // GLA_CHUNK
    return pl.pallas_call(
        _gla_kernel,
        out_shape=jax.ShapeDtypeStruct((batch * seq, GLA_V), bf16),
        grid=(batch, nt),
        in_specs=[pl.BlockSpec((tc, W_GLA), lambda b, i: (b * nt + i, 0)),
                  _const_spec((128, GLA_QK)),
                  _const_spec((1, GLA_QK)),
                  _const_spec((1, GLA_V))],
        out_specs=pl.BlockSpec((tc, GLA_V), lambda b, i: (b * nt + i, 0)),
        scratch_shapes=[pltpu.VMEM((GLA_V, GLA_QK), f32),
                        pltpu.VMEM((tc, GLA_QK), f32),
                        pltpu.VMEM((tc, GLA_QK), bf16),
                        pltpu.VMEM((n_chunks, GLA_V, GLA_QK), f32),
                        pltpu.VMEM((n_chunks, 8, GLA_QK), f32),
                        pltpu.VMEM((n_chunks, GLA_V, GLA_QK), bf16),
                        pltpu.VMEM((tc, GLA_V), f32)],
        compiler_params=_cparams(2),
        name="gla",
    )(z_gla, wg, bg, no)


DN_W = DN_HEADS * DN_DK
DN_PAIRS = DN_HEADS // 2


def _stack_heads(x):
    lane = _iota(x.shape, 1)
    zero = jnp.zeros_like(x)
    return jnp.concatenate([jnp.where(lane < DN_DK, x, zero), jnp.where(lane >= DN_DK, x, zero)], axis=0)


def _dot_3x(a, b):
    a_hi = a.astype(bf16)
    b_hi = b.astype(bf16)
    a_lo = (a - a_hi.astype(f32)).astype(bf16)
    b_lo = (b - b_hi.astype(f32)).astype(bf16)
    return _dot(a_hi, b_hi) + (_dot(a_hi, b_lo) + _dot(a_lo, b_hi))


def _dn_pre_kernel(y_ref, ab_ref, alog_ref, dtb_ref,
                   a_ref, attn_ref, rhs_ref, qd_ref, kd_ref, egl_ref,
                   q_s, k_s, beta_s, g_s, gam_s, gamr_s):
    tc = y_ref.shape[0]
    c_len = DN_CHUNK

    sum_mat = _group_mean_matrix(DN_W, DN_DK) * float(DN_DK)
    q = y_ref[:, 0:DN_W]
    k = y_ref[:, DN_W:2 * DN_W]
    q_s[...] = q * lax.rsqrt(_group_stat(q * q, sum_mat) + NORM_EPS) * (DN_DK ** -0.5)
    k_s[...] = k * lax.rsqrt(_group_stat(k * k, sum_mat) + NORM_EPS)

    ab = ab_ref[...]
    lane_head = _iota((tc, DN_W), 1) // DN_DK
    a_rep = jnp.zeros((tc, DN_W), f32)
    b_rep = jnp.zeros((tc, DN_W), f32)
    for h in range(DN_HEADS):
        a_rep = jnp.where(lane_head == h, jnp.broadcast_to(ab[:, h:h + 1], (tc, DN_W)), a_rep)
        b_rep = jnp.where(lane_head == h,
                          jnp.broadcast_to(ab[:, DN_HEADS + h:DN_HEADS + h + 1], (tc, DN_W)), b_rep)
    beta_s[...] = _sigmoid(b_rep)
    g_s[...] = -jnp.exp(alog_ref[...]) * _softplus(a_rep + dtb_ref[...])

    r64 = _iota((c_len, c_len), 0)
    c64 = _iota((c_len, c_len), 1)
    tri = (r64 >= c64).astype(bf16)
    ones = jnp.ones((c_len, c_len), bf16)
    row = _iota((c_len, DN_W), 0)
    colj = _iota((c_len, DN_W), 1) % c_len
    incl = row >= colj
    strict = row > colj

    n_chunks = tc // c_len
    g_parts = _split_bf16(g_s[...], 3)
    zero_b = jnp.zeros((c_len, DN_W), bf16)
    total = lambda parts: functools.reduce(lambda a, b: a + b, reversed(parts))
    for c in range(n_chunks):
        rows = slice(c * c_len, (c + 1) * c_len)
        gam_s[rows, :] = total([_dot(tri, part[rows]) for part in g_parts])
    for c in range(n_chunks):
        rows = slice(c * c_len, (c + 1) * c_len)
        gamr_s[rows, :] = total([_dot(ones, jnp.where(row <= colj, part[rows], zero_b)) for part in g_parts])

    def chunk(c):
        rows = slice(c * c_len, (c + 1) * c_len)
        gam = gam_s[rows, :]
        gam_row = gamr_s[rows, :]
        dec_incl = jnp.exp(jnp.where(incl, gam - gam_row, -1e30))
        dec_strict = jnp.where(strict, dec_incl, 0.0)
        eg = jnp.exp(gam)
        g_last = gam[c_len - 1:c_len, :]
        q = q_s[rows, :]
        k = k_s[rows, :]
        beta = beta_s[rows, :]
        bv = beta * y_ref[rows, 2 * DN_W:3 * DN_W]
        bk = beta * eg * k
        kdec = k * jnp.exp(g_last - gam)
        qd_ref[rows, :] = (q * eg).astype(bf16)
        egl_ref[c] = jnp.broadcast_to(jnp.exp(g_last), (8, DN_W))
        qb = q.astype(bf16)
        kb = k.astype(bf16)
        for p in range(DN_PAIRS):
            ps = slice(p * 128, (p + 1) * 128)
            kbp = kb[:, ps]
            k_bd = _stack_heads(kbp)
            a_pair = beta[:, ps] * _dot_nt(kbp, k_bd) * dec_strict[:, ps]
            a_ref[c * DN_HEADS + 2 * p] = a_pair[:, :DN_CHUNK]
            a_ref[c * DN_HEADS + 2 * p + 1] = a_pair[:, DN_CHUNK:]
            attn_ref[c * DN_PAIRS + p] = (_dot_nt(qb[:, ps], k_bd) * dec_incl[:, ps]).astype(bf16)
            kd_ref[c * DN_PAIRS + p] = _stack_heads(kdec[:, ps]).T.astype(bf16)
            rhs_ref[c * DN_PAIRS + p] = jnp.concatenate(
                [_stack_heads(bv[:, ps]), _stack_heads(bk[:, ps])], axis=1)

    for c in range(n_chunks):
        chunk(c)


def _dn_pre(z_dn, alog, dtb, batch, seq):
    tc = SCAN_TILE
    nt = seq // tc
    t = batch * seq
    npt = (tc // DN_CHUNK) * DN_PAIRS
    n_pair = (t // DN_CHUNK) * DN_PAIRS
    scr = lambda: pltpu.VMEM((tc, DN_W), f32)
    tok = lambda: pl.BlockSpec((tc, DN_W), lambda b, i: (b * nt + i, 0))
    mat = lambda r, w: pl.BlockSpec((npt, r, w), lambda b, i: (b * nt + i, 0, 0))
    return pl.pallas_call(
        _dn_pre_kernel,
        out_shape=(jax.ShapeDtypeStruct((2 * n_pair, DN_CHUNK, DN_CHUNK), f32),
                   jax.ShapeDtypeStruct((n_pair, DN_CHUNK, 128), bf16),
                   jax.ShapeDtypeStruct((n_pair, 128, 256), f32),
                   jax.ShapeDtypeStruct((t, DN_W), bf16),
                   jax.ShapeDtypeStruct((n_pair, 128, 128), bf16),
                   jax.ShapeDtypeStruct((t // DN_CHUNK, 8, DN_W), f32)),
        grid=(batch, nt),
        in_specs=[pl.BlockSpec((tc, 3 * DN_W), lambda b, i: (b * nt + i, 0)),
                  pl.BlockSpec((tc, 128), lambda b, i: (b * nt + i, 4 * DN_W // 128)),
                  _const_spec((1, DN_W)),
                  _const_spec((1, DN_W))],
        out_specs=(pl.BlockSpec((2 * npt, DN_CHUNK, DN_CHUNK), lambda b, i: (b * nt + i, 0, 0)),
                   mat(DN_CHUNK, 128), mat(128, 256), tok(), mat(128, 128),
                   pl.BlockSpec((tc // DN_CHUNK, 8, DN_W), lambda b, i: (b * nt + i, 0, 0))),
        scratch_shapes=[scr(), scr(), scr(), scr(), scr(), scr()],
        compiler_params=_cparams(2),
        name="dn_pre",
    )(z_dn, z_dn, alog, dtb)


def _dn_solve_kernel(a_ref, t_ref):
    c_len = DN_CHUNK
    blk = DN_INV_BLOCK
    vshape = a_ref.shape[1:]

    def row_pair(ip, carry):
        i0 = 2 * ip
        i1 = i0 + 1
        for mb in range(c_len // blk):
            m0 = mb * blk

            def col_j(j, accs):
                a0 = a_ref[i0 * c_len + j]
                a1 = a_ref[i1 * c_len + j]
                ts = [t_ref[j * c_len + m0 + m] for m in range(blk)]
                return (tuple(accs[0][m] - a0 * ts[m] for m in range(blk)),
                        tuple(accs[1][m] - a1 * ts[m] for m in range(blk)))

            zeros = tuple(jnp.zeros(vshape, f32) for _ in range(blk))
            acc0, acc1 = lax.fori_loop(m0, i0, col_j, (zeros, zeros))
            a10 = a_ref[i1 * c_len + i0]
            for m in range(blk):
                t0 = acc0[m] + jnp.where(i0 == m0 + m, 1.0, 0.0)
                t_ref[i0 * c_len + m0 + m] = t0
                t_ref[i1 * c_len + m0 + m] = acc1[m] - a10 * t0 + jnp.where(i1 == m0 + m, 1.0, 0.0)
        return carry

    lax.fori_loop(0, c_len // 2, row_pair, 0)


def _dn_solve(a_t):
    n_ent, n_grp, lanes = a_t.shape
    g = min(n_grp, 8)
    spec = lambda: pl.BlockSpec((n_ent, g, lanes), lambda i: (0, i, 0), pipeline_mode=pl.Buffered(1))
    return pl.pallas_call(
        _dn_solve_kernel,
        out_shape=jax.ShapeDtypeStruct(a_t.shape, f32),
        grid=(n_grp // g,),
        in_specs=[spec()],
        out_specs=spec(),
        compiler_params=_cparams(1),
        name="dn_solve",
    )(a_t)


DN_SCAN_SEQS = 2


def _dn_scan_kernel(t_ref, attn_ref, rhs_ref, qd_ref, kd_ref, egl_ref, zg_ref, no_ref,
                    o_ref, st_ref, oraw_ref):
    n_seq, tc = qd_ref.shape[0], qd_ref.shape[1]
    c_len = DN_CHUNK
    seqs = range(n_seq)
    pairs = range(DN_PAIRS)
    psl = [slice(p * 128, (p + 1) * 128) for p in pairs]

    @pl.when(pl.program_id(1) == 0)
    def _():
        st_ref[...] = jnp.zeros_like(st_ref)

    def solve(s, c):
        t_pair = [jnp.concatenate([t_ref[s, c * DN_HEADS + 2 * p], t_ref[s, c * DN_HEADS + 2 * p + 1]], axis=-1)
                  for p in pairs]
        return [_dot_3x(t_pair[p], rhs_ref[s, c * DN_PAIRS + p]) for p in pairs]

    n_chunks = tc // c_len
    sols = [solve(s, 0) for s in seqs]
    for c in range(n_chunks):
        rows = slice(c * c_len, (c + 1) * c_len)
        sts = [[st_ref[s, p] for p in pairs] for s in seqs]
        xq = [[_dot(jnp.concatenate([sols[s][p][:, 128:].astype(bf16), qd_ref[s, rows, psl[p]]], axis=0),
                    sts[s][p].astype(bf16)) for p in pairs] for s in seqs]
        nxt = [solve(s, c + 1) for s in seqs] if c + 1 < n_chunks else None
        u_bd = [[_stack_heads((sols[s][p][:, :128] - xq[s][p][:c_len]).astype(bf16)) for p in pairs]
                for s in seqs]
        au = [[_dot(jnp.concatenate([attn_ref[s, c * DN_PAIRS + p], kd_ref[s, c * DN_PAIRS + p]], axis=0),
                    u_bd[s][p]) for p in pairs] for s in seqs]
        for s in seqs:
            eg_last = egl_ref[s, c][0:1, :]
            for p in pairs:
                st_ref[s, p] = eg_last[:, psl[p]] * sts[s][p] + au[s][p][c_len:]
            oraw_ref[s, rows, :] = jnp.concatenate([xq[s][p][c_len:] + au[s][p][:c_len] for p in pairs], axis=-1)
        sols = nxt

    mean_mat = _group_mean_matrix(DN_W, DN_DV)
    for s in seqs:
        o = oraw_ref[s]
        ms = _group_stat(o * o, mean_mat)
        zg = zg_ref[s]
        o_ref[s] = (o * lax.rsqrt(ms + NORM_EPS) * no_ref[...] * (zg * _sigmoid(zg))).astype(bf16)


def _dn_scan(t_all, attn, rhs, qd, kd, egl, z_dn, no, batch, seq):
    tc = SCAN_TILE
    nt = seq // tc
    ns = DN_SCAN_SEQS if batch % DN_SCAN_SEQS == 0 else 1
    g = batch // ns
    npt = (tc // DN_CHUNK) * DN_PAIRS
    n_chunk_seq = seq // DN_CHUNK

    def grouped(x, per_seq):
        return x.reshape((g, ns, per_seq) + x.shape[1:])

    def spec(per_tile, tail, last_block=0):
        zeros = (0,) * (len(tail) - 1)
        return pl.BlockSpec((None, ns, per_tile) + tail, lambda b, i: (b, 0, i) + zeros + (last_block,))

    out = pl.pallas_call(
        _dn_scan_kernel,
        out_shape=jax.ShapeDtypeStruct((g, ns, seq, DN_W), bf16),
        grid=(g, nt),
        in_specs=[spec(2 * npt, (DN_CHUNK, DN_CHUNK)),
                  spec(npt, (DN_CHUNK, 128)), spec(npt, (128, 256)), spec(tc, (DN_W,)), spec(npt, (128, 128)),
                  spec(tc // DN_CHUNK, (8, DN_W)),
                  spec(tc, (DN_W,), last_block=3),
                  _const_spec((1, DN_W))],
        out_specs=spec(tc, (DN_W,)),
        scratch_shapes=[pltpu.VMEM((ns, DN_PAIRS, 128, 128), f32),
                        pltpu.VMEM((ns, tc, DN_W), f32)],
        compiler_params=_cparams(2),
        name="dn_scan",
    )(grouped(t_all, n_chunk_seq * DN_HEADS), grouped(attn, n_chunk_seq * DN_PAIRS),
      grouped(rhs, n_chunk_seq * DN_PAIRS), grouped(qd, seq), grouped(kd, n_chunk_seq * DN_PAIRS),
      grouped(egl, n_chunk_seq), grouped(z_dn, seq), no)
    return out.reshape(batch * seq, DN_W)


def _dn(z_dn, alog, dtb, no, batch, seq):
    a_all, attn, rhs, qd, kd, egl = _dn_pre(z_dn, alog, dtb, batch, seq)
    n_prob = a_all.shape[0]
    n_ent = DN_CHUNK * DN_CHUNK
    a_t = a_all.reshape(n_prob, n_ent).T.reshape(n_ent, n_prob // 128, 128)
    t_all = _dn_solve(a_t).reshape(n_ent, n_prob).T.reshape(n_prob, DN_CHUNK, DN_CHUNK)
    return _dn_scan(t_all, attn, rhs, qd, kd, egl, z_dn, no, batch, seq)


def _merge_kernel(x_ref, g_ref, oa_ref, ob_ref, oc_ref, od_ref, wg_ref, wb_ref, wo_ref, out_ref):
    x = x_ref[...]
    h = _rms(x, g_ref[...]).astype(bf16)
    merged = jnp.zeros(x.shape, f32)
    for n, o_ref in enumerate((oa_ref, ob_ref, oc_ref, od_ref)):
        gate = _sigmoid(_dot(h, wg_ref[:, n * D_MODEL:(n + 1) * D_MODEL]))
        merged = merged + gate * _dot(o_ref[...], wb_ref[n])
    out_ref[...] = x + _dot(merged.astype(bf16), wo_ref[...])


def _merge(x2, g, oa, ob, oc, od, wg, wb, wo):
    t = x2.shape[0]
    tm = TOKEN_TILE
    tok = lambda w: pl.BlockSpec((tm, w), lambda i: (i, 0))
    return pl.pallas_call(
        _merge_kernel,
        out_shape=jax.ShapeDtypeStruct((t, D_MODEL), f32),
        grid=(t // tm,),
        in_specs=[tok(D_MODEL), _const_spec((1, D_MODEL)),
                  tok(BRANCH_WIDTH), tok(BRANCH_WIDTH), tok(BRANCH_WIDTH), tok(BRANCH_WIDTH),
                  _const_spec((D_MODEL, N_BRANCH * D_MODEL)),
                  _const_spec((N_BRANCH, BRANCH_WIDTH, D_MODEL)),
                  _const_spec((D_MODEL, D_MODEL))],
        out_specs=tok(D_MODEL),
        compiler_params=_cparams(1),
        name="merge",
    )(x2, g, oa, ob, oc, od, wg, wb, wo)


def _memkv_kernel(mem_ref, g_ref, wk_ref, wv_ref, k_ref, v_ref):
    mn = _rms(mem_ref[...], g_ref[...]).astype(bf16)
    k_ref[...] = _dot(mn, wk_ref[...]).astype(bf16)
    v_ref[...] = _dot(mn, wv_ref[...]).astype(bf16)


def _memkv(mem2, g, wk, wv, mem_len):
    t = mem2.shape[0]
    blk = lambda: pl.BlockSpec((mem_len, D_MODEL), lambda i: (i, 0))
    return pl.pallas_call(
        _memkv_kernel,
        out_shape=(jax.ShapeDtypeStruct((t, D_MODEL), bf16), jax.ShapeDtypeStruct((t, D_MODEL), bf16)),
        grid=(t // mem_len,),
        in_specs=[blk(), _const_spec((1, D_MODEL)),
                  _const_spec((D_MODEL, D_MODEL)), _const_spec((D_MODEL, D_MODEL))],
        out_specs=(blk(), blk()),
        compiler_params=_cparams(1),
        name="mem_kv",
    )(mem2, g, wk, wv)


def _xattn_kernel(x_ref, g_ref, k_ref, v_ref, wq_ref, wo_ref, out_ref):
    x = x_ref[...]
    h = _rms(x, g_ref[...]).astype(bf16)
    q = (_dot(h, wq_ref[...]) * (XATTN_DIM ** -0.5)).astype(bf16)
    hsl = [slice(hh * XATTN_DIM, (hh + 1) * XATTN_DIM) for hh in range(XATTN_HEADS)]
    scores = [_dot_nt(q[:, sl], k_ref[:, sl]) for sl in hsl]
    probs = []
    for s in scores:
        p = jnp.exp(s - jnp.max(s, axis=-1, keepdims=True))
        probs.append((p / jnp.sum(p, axis=-1, keepdims=True)).astype(bf16))
    o = jnp.concatenate([_dot(p, v_ref[:, sl]).astype(bf16) for p, sl in zip(probs, hsl)], axis=-1)
    out_ref[...] = x + _dot(o, wo_ref[...])


def _xattn(x2, g, kmem, vmem, wq, wo, batch, seq, mem_len):
    tm = TOKEN_TILE
    nt = seq // tm
    tok = pl.BlockSpec((tm, D_MODEL), lambda b, i: (b * nt + i, 0))
    memb = lambda: pl.BlockSpec((mem_len, D_MODEL), lambda b, i: (b, 0))
    return pl.pallas_call(
        _xattn_kernel,
        out_shape=jax.ShapeDtypeStruct((batch * seq, D_MODEL), f32),
        grid=(batch, nt),
        in_specs=[tok, _const_spec((1, D_MODEL)), memb(), memb(),
                  _const_spec((D_MODEL, D_MODEL)), _const_spec((D_MODEL, D_MODEL))],
        out_specs=tok,
        compiler_params=_cparams(2),
        name="xattn",
    )(x2, g, kmem, vmem, wq, wo)


def _mlp_kernel(x_ref, g_ref, wu_ref, wd_ref, gf_ref, out_ref, *, final):
    x = x_ref[...]
    h = _rms(x, g_ref[...]).astype(bf16)
    acc = x
    for c in range(MLP_HIDDEN // MLP_CHUNK):
        sl = slice(c * MLP_CHUNK, (c + 1) * MLP_CHUNK)
        a = jnp.maximum(_dot(h, wu_ref[:, sl]), 0.0)
        acc = acc + _dot((a * a).astype(bf16), wd_ref[sl, :])
    if final:
        acc = _rms(acc, gf_ref[...])
    out_ref[...] = acc


def _mlp(x2, g, wu, wd, gf, final):
    t = x2.shape[0]
    tm = TOKEN_TILE
    tok = pl.BlockSpec((tm, D_MODEL), lambda i: (i, 0))
    return pl.pallas_call(
        functools.partial(_mlp_kernel, final=final),
        out_shape=jax.ShapeDtypeStruct((t, D_MODEL), f32),
        grid=(t // tm,),
        in_specs=[tok, _const_spec((1, D_MODEL)),
                  _const_spec((D_MODEL, MLP_HIDDEN)), _const_spec((MLP_HIDDEN, D_MODEL)),
                  _const_spec((1, D_MODEL))],
        out_specs=tok,
        compiler_params=_cparams(1),
        name="mlp",
    )(x2, g, wu, wd, gf)


def _pad_cols(w, width):
    return jnp.pad(w, ((0, 0), (0, width - w.shape[1])))


def _prep_w_in(w_in):
    col = lambda i: w_in[:, IN_OFFS[i]:IN_OFFS[i + 1]]
    zeros = lambda n: jnp.zeros((D_MODEL, n), w_in.dtype)
    (cq, ckv, kpe, sg_uv, gq, gk, gv, g_lr, g_og, dq, dk, dv, da, db, dz) = (col(i) for i in range(15))
    small = jnp.concatenate([
        cq, ckv, zeros(MLA_NOPE), kpe, zeros(MLA_HEAD_PAD - MLA_NOPE - MLA_ROPE),
        sg_uv,
        gq, gk, gv, g_og, _pad_cols(g_lr, 128),
        dq, dk, dv, dz, _pad_cols(jnp.concatenate([da, db], axis=1), 128),
    ], axis=1)
    assert small.shape[1] == W_SMALL
    return small.astype(bf16), col(15).astype(bf16)


def _rope_inv_freq():
    inv_freq = ROPE_THETA ** (-jnp.arange(0, MLA_ROPE, 2, dtype=f32) / MLA_ROPE)
    return inv_freq[:, None]


def kernel(x, mem, positions, norm_mix, w_in, mla_norm_q, mla_norm_kv, mla_w_uq, mla_w_ukv, sg_ln_g, sg_ln_b, sg_w, sg_b, gla_w_gate, gla_b_gate, gla_norm, dn_conv, dn_a_log, dn_dt_bias, dn_norm, w_branch, w_out, norm_xattn, norm_mem, xattn_wq, xattn_wk, xattn_wv, xattn_wo, norm_mlp, w_up, w_down, norm_final):
    batch, seq, _ = x.shape
    mem_len = mem.shape[1]
    depth = w_in.shape[0]
    assert seq % TOKEN_TILE == 0 and seq % SCAN_TILE == 0 and seq % ATTN_TILE == 0
    x2 = x.reshape(batch * seq, D_MODEL)
    mem2 = mem.reshape(batch * mem_len, D_MODEL)
    pos2 = positions.reshape(batch * seq // TOKEN_TILE, 1, TOKEN_TILE)
    invf = _rope_inv_freq()
    row = lambda v: v[None, :]

    for l in range(depth):
        w_small, w_gates = _prep_w_in(w_in[l])
        z_mla, z_sg, z_gla, z_dn = _inproj(x2, row(norm_mix[l]), w_small, dn_conv[l], seq)

        wuq = jnp.pad(mla_w_uq[l].reshape(MLA_Q_RANK, MLA_HEADS, MLA_NOPE + MLA_ROPE),
                      ((0, 0), (0, 0), (0, MLA_HEAD_PAD - MLA_NOPE - MLA_ROPE)))
        wuq = wuq.reshape(MLA_Q_RANK, MLA_HEADS * MLA_HEAD_PAD).astype(bf16)
        wukv = mla_w_ukv[l].reshape(MLA_KV_RANK, MLA_HEADS, MLA_NOPE + MLA_V)
        wuk = jnp.pad(wukv[:, :, :MLA_NOPE], ((0, 0), (0, 0), (0, MLA_HEAD_PAD - MLA_NOPE)))
        wuk = wuk.reshape(MLA_KV_RANK, MLA_HEADS * MLA_HEAD_PAD).astype(bf16)
        wuv = wukv[:, :, MLA_NOPE:].reshape(MLA_KV_RANK, MLA_HEADS * MLA_V).astype(bf16)
        q_a, k_a, v_a = _mla_pre(z_mla, pos2, invf, row(mla_norm_q[l]), row(mla_norm_kv[l]), wuq, wuk, wuv,
                                 batch, seq)
        o_a = _mla_attn(q_a, k_a, v_a, batch, seq)

        o_b = _sg(z_sg, row(sg_ln_g[l]), row(sg_ln_b[l]), sg_w[l], sg_b[l].T)

        wg = jnp.pad(gla_w_gate[l], ((0, 128 - GLA_GATE_RANK), (0, 0))).astype(bf16)
        o_c = _gla(z_gla, wg, row(gla_b_gate[l]), row(jnp.tile(gla_norm[l], GLA_HEADS)), batch, seq)

        o_d = _dn(z_dn, row(jnp.repeat(dn_a_log[l], DN_DK)), row(jnp.repeat(dn_dt_bias[l], DN_DK)),
                  row(jnp.tile(dn_norm[l], DN_HEADS)), batch, seq)

        x2 = _merge(x2, row(norm_mix[l]), o_a, o_b, o_c, o_d, w_gates,
                    w_branch[l].astype(bf16), w_out[l].astype(bf16))

        k_m, v_m = _memkv(mem2, row(norm_mem[l]), xattn_wk[l].astype(bf16), xattn_wv[l].astype(bf16), mem_len)
        x2 = _xattn(x2, row(norm_xattn[l]), k_m, v_m, xattn_wq[l].astype(bf16), xattn_wo[l].astype(bf16),
                    batch, seq, mem_len)
        x2 = _mlp(x2, row(norm_mlp[l]), w_up[l].astype(bf16), w_down[l].astype(bf16), row(norm_final),
                  final=(l == depth - 1))
    return x2.reshape(batch, seq, D_MODEL)
```

```python
import functools

import jax
import jax.numpy as jnp
import numpy as np
from jax import lax
from jax.experimental import pallas as pl
from jax.experimental.pallas import tpu as pltpu

f32 = jnp.float32
bf16 = jnp.bfloat16
HIGHEST = lax.Precision.HIGHEST

D_MODEL = 1024
NORM_EPS = 1e-6
N_BRANCH = 4
BRANCH_WIDTH = 256

MLA_HEADS = 4
MLA_Q_RANK = 256
MLA_KV_RANK = 128
MLA_NOPE = 64
MLA_ROPE = 32
MLA_V = 64
ROPE_THETA = 10000.0
MLA_HEAD_PAD = 128
MLA_VT_ROWS = 80

SG_GROUPS = 4
SG_WIDTH = 256
SG_CHUNK = 128

GLA_HEADS = 4
GLA_DK = 32
GLA_DV = 64
GLA_GATE_RANK = 16
GLA_GATE_TEMP = 16.0
GLA_CHUNK = 64

DN_HEADS = 4
DN_DK = 64
DN_DV = 64
DN_CONV = 4
DN_CHUNK = 64
DN_INV_BLOCK = 16

XATTN_HEADS = 4
XATTN_DIM = D_MODEL // XATTN_HEADS
MLP_HIDDEN = 4 * D_MODEL

IN_SIZES = (
    MLA_Q_RANK, MLA_KV_RANK, MLA_ROPE,
    2 * SG_WIDTH,
    GLA_HEADS * GLA_DK, GLA_HEADS * GLA_DK, GLA_HEADS * GLA_DV,
    GLA_GATE_RANK, GLA_HEADS * GLA_DV,
    DN_HEADS * DN_DK, DN_HEADS * DN_DK, DN_HEADS * DN_DV,
    DN_HEADS, DN_HEADS, DN_HEADS * DN_DV,
    N_BRANCH * D_MODEL,
)
IN_OFFS = tuple(int(v) for v in np.cumsum((0,) + IN_SIZES))

W_MLA = 512
W_SG = 512
W_GLA = 896
W_DN = 1152
W_SMALL = W_MLA + W_SG + W_GLA + W_DN

TOKEN_TILE = 512
ATTN_TILE = 256
ATTN_KEY_TILE = 128
SCAN_TILE = 512
MLP_CHUNK = 1024
VMEM_LIMIT = 56 * 1024 * 1024


def _cparams(n_axes):
    return pltpu.CompilerParams(dimension_semantics=("arbitrary",) * n_axes,
                                vmem_limit_bytes=VMEM_LIMIT)


def _const_spec(shape):
    nd = len(shape)
    return pl.BlockSpec(shape, lambda *_: (0,) * nd, pipeline_mode=pl.Buffered(1))


def _rms(x, g):
    return x * lax.rsqrt(jnp.mean(x * x, axis=-1, keepdims=True) + NORM_EPS) * g


def _sigmoid(x):
    return 1.0 / (1.0 + jnp.exp(-x))


def _softplus(x):
    return jnp.maximum(x, 0.0) + jnp.log1p(jnp.exp(-jnp.abs(x)))


def _dot(a, b):
    return jnp.dot(a, b, preferred_element_type=f32)


def _split_bf16(x, terms):
    parts = []
    for _ in range(terms):
        p = x.astype(bf16)
        parts.append(p)
        x = x - p.astype(f32)
    return parts


def _dot_sel(sel, x, terms=3):
    sel = sel.astype(bf16)
    parts = [_dot(sel, p) for p in _split_bf16(x, terms)]
    return functools.reduce(lambda a, b: a + b, reversed(parts))


def _group_stat(x, mat, terms=2):
    mat = mat.astype(bf16)
    parts = [_dot(p, mat) for p in _split_bf16(x, terms)]
    return functools.reduce(lambda a, b: a + b, reversed(parts))


def _dot_nt(a, b):
    return lax.dot_general(a, b, (((1,), (1,)), ((), ())), preferred_element_type=f32)


def _dot_tn(a, b):
    return lax.dot_general(a, b, (((0,), (0,)), ((), ())), preferred_element_type=f32)


def _iota(shape, dim):
    return lax.broadcasted_iota(jnp.int32, shape, dim)


def _group_mean_matrix(width, group):
    r = _iota((width, width), 0) // group
    c = _iota((width, width), 1) // group
    return jnp.where(r == c, 1.0 / group, 0.0).astype(f32)


DN_CONV_TAPS = 4
DN_CONV_COLS = 768
DN_CONV_PAD = 8


def _inproj_kernel(x_ref, g_ref, w_ref, cw_ref, mla_ref, sg_ref, gla_ref, dn_ref, xbuf_ref, *,
                   tiles_per_seq):
    @pl.when(pl.program_id(0) % tiles_per_seq == 0)
    def _():
        xbuf_ref[0:DN_CONV_PAD, :] = jnp.zeros((DN_CONV_PAD, DN_CONV_COLS), f32)

    h = _rms(x_ref[...], g_ref[...]).astype(bf16)
    tm = h.shape[0]
    z_dn = _dot(h, w_ref[:, W_SMALL - W_DN:])
    dn_ref[:, DN_CONV_COLS:] = z_dn[:, DN_CONV_COLS:]
    xbuf_ref[DN_CONV_PAD:DN_CONV_PAD + tm, :] = z_dn[:, :DN_CONV_COLS]
    y = jnp.zeros((tm, DN_CONV_COLS), f32)
    for kk in range(DN_CONV_TAPS):
        y = y + cw_ref[kk:kk + 1, :] * xbuf_ref[pl.ds(DN_CONV_PAD - (DN_CONV_TAPS - 1) + kk, tm), :]
    xbuf_ref[0:DN_CONV_PAD, :] = xbuf_ref[tm:tm + DN_CONV_PAD, :]
    dn_ref[:, :DN_CONV_COLS] = y * _sigmoid(y)
    lo = 0
    for ref, width in ((mla_ref, W_MLA), (sg_ref, W_SG), (gla_ref, W_GLA)):
        ref[...] = _dot(h, w_ref[:, lo:lo + width])
        lo += width


def _inproj(x2, g, w_small, conv_w, seq):
    t = x2.shape[0]
    tm = TOKEN_TILE
    outs = tuple(jax.ShapeDtypeStruct((t, w), f32) for w in (W_MLA, W_SG, W_GLA, W_DN))
    return pl.pallas_call(
        functools.partial(_inproj_kernel, tiles_per_seq=seq // tm),
        out_shape=outs,
        grid=(t // tm,),
        in_specs=[pl.BlockSpec((tm, D_MODEL), lambda i: (i, 0)),
                  _const_spec((1, D_MODEL)),
                  _const_spec((D_MODEL, W_SMALL)),
                  _const_spec((DN_CONV_TAPS, DN_CONV_COLS))],
        out_specs=tuple(pl.BlockSpec((tm, w), lambda i: (i, 0)) for w in (W_MLA, W_SG, W_GLA, W_DN)),
        scratch_shapes=[pltpu.VMEM((tm + 2 * DN_CONV_PAD, DN_CONV_COLS), f32)],
        compiler_params=_cparams(1),
        name="inproj",
    )(x2, g, w_small, conv_w)


def _mla_pre_kernel(z_ref, pos_ref, invf_ref, nq_ref, nkv_ref, wuq_ref, wuk_ref, wuv_ref,
                    qt_ref, k_ref, vt_ref):
    z = z_ref[...]
    tm = z.shape[0]
    cq = z[:, :MLA_Q_RANK]
    ckv = z[:, MLA_Q_RANK:MLA_Q_RANK + MLA_KV_RANK]
    kpe = z[:, MLA_Q_RANK + MLA_KV_RANK:]

    half = MLA_ROPE // 2
    ang_t = invf_ref[...] * pos_ref[...].astype(f32)
    f_row = _iota((half, MLA_HEAD_PAD), 0)
    f_lane = _iota((half, MLA_HEAD_PAD), 1)
    place = ((f_lane == MLA_NOPE + f_row) | (f_lane == MLA_NOPE + half + f_row)).astype(bf16)

    def to_lanes(x_t):
        parts = [_dot_tn(p, place) for p in _split_bf16(x_t, 3)]
        return functools.reduce(lambda a, b: a + b, reversed(parts))

    cos = to_lanes(jnp.cos(ang_t))
    sin = to_lanes(jnp.sin(ang_t))
    lane = _iota((tm, MLA_HEAD_PAD), 1)
    in_x1 = (lane >= MLA_NOPE) & (lane < MLA_NOPE + half)
    in_x2 = (lane >= MLA_NOPE + half) & (lane < MLA_NOPE + MLA_ROPE)
    c_tab = jnp.where(in_x1 | in_x2, cos, 1.0)
    s1_tab = jnp.where(in_x1, -sin, 0.0)
    s2_tab = jnp.where(in_x2, sin, 0.0)

    def rope(t):
        return (t * c_tab + pltpu.roll(t, MLA_HEAD_PAD - half, 1) * s1_tab
                + pltpu.roll(t, half, 1) * s2_tab)

    scale = (MLA_NOPE + MLA_ROPE) ** -0.5
    q = _dot(_rms(cq, nq_ref[...]).astype(bf16), wuq_ref[...])
    kvn = _rms(ckv, nkv_ref[...]).astype(bf16)
    kn = _dot(kvn, wuk_ref[...])
    v_t = _dot(kvn, wuv_ref[...]).T
    kpe_rot = rope(kpe)
    q_rot = []
    for h in range(MLA_HEADS):
        sl = slice(h * MLA_HEAD_PAD, (h + 1) * MLA_HEAD_PAD)
        q_rot.append(rope(q[:, sl]) * scale)
        k_ref[:, sl] = (kn[:, sl] + kpe_rot).astype(bf16)
    qt_ref[...] = jnp.concatenate(q_rot, axis=-1).T.astype(bf16)
    pad = jnp.concatenate([jnp.ones((1, tm), f32), jnp.zeros((MLA_VT_ROWS - MLA_V - 1, tm), f32)], axis=0)
    rows = []
    for h in range(MLA_HEADS):
        rows += [v_t[h * MLA_V:(h + 1) * MLA_V], pad]
    vt_ref[...] = jnp.concatenate(rows, axis=0).astype(bf16)


def _mla_pre(z_mla, pos2, invf, nq, nkv, wuq, wuk, wuv, batch, seq):
    t = z_mla.shape[0]
    tm = TOKEN_TILE
    nt = seq // tm
    hp = MLA_HEADS * MLA_HEAD_PAD
    hv = MLA_HEADS * MLA_V
    hvt = MLA_HEADS * MLA_VT_ROWS
    return pl.pallas_call(
        _mla_pre_kernel,
        out_shape=(jax.ShapeDtypeStruct((batch, hp, seq), bf16),
                   jax.ShapeDtypeStruct((t, hp), bf16),
                   jax.ShapeDtypeStruct((batch, hvt, seq), bf16)),
        grid=(t // tm,),
        in_specs=[pl.BlockSpec((tm, W_MLA), lambda i: (i, 0)),
                  pl.BlockSpec((None, 1, tm), lambda i: (i, 0, 0)),
                  _const_spec((MLA_ROPE // 2, 1)),
                  _const_spec((1, MLA_Q_RANK)),
                  _const_spec((1, MLA_KV_RANK)),
                  _const_spec((MLA_Q_RANK, hp)),
                  _const_spec((MLA_KV_RANK, hp)),
                  _const_spec((MLA_KV_RANK, hv))],
        out_specs=(pl.BlockSpec((None, hp, tm), lambda i: (i // nt, 0, i % nt)),
                   pl.BlockSpec((tm, hp), lambda i: (i, 0)),
                   pl.BlockSpec((None, hvt, tm), lambda i: (i // nt, 0, i % nt))),
        compiler_params=_cparams(1),
        name="mla_pre",
    )(z_mla, pos2, invf, nq, nkv, wuq, wuk, wuv)


def _mla_attn_kernel(qt_ref, k_ref, vt_ref, o_ref, sa_ref, sb_ref):
    i = pl.program_id(1)
    tq = qt_ref.shape[1]
    tk = sa_ref.shape[1]
    per = tq // tk
    key = _iota((tk, tq), 0)
    qry = _iota((tk, tq), 1)

    def issue_scores(b, buf):
        start = pl.multiple_of(b * tk, tk)
        cms = []
        for h in range(MLA_HEADS):
            hsl = slice(h * MLA_HEAD_PAD, (h + 1) * MLA_HEAD_PAD)
            s = _dot(k_ref[pl.ds(start, tk), hsl], qt_ref[hsl, :])
            buf[h] = s
            cms.append(jnp.max(s, axis=0, keepdims=True))
        return tuple(cms)

    def consume(b, buf, cms, state, diag_offset=None):
        start = pl.multiple_of(b * tk, tk)
        probs = []
        for h in range(MLA_HEADS):
            s = buf[h]
            if diag_offset is None:
                cm = cms[h]
            else:
                s = jnp.where(key + diag_offset <= qry, s, -jnp.inf)
                cm = jnp.max(s, axis=0, keepdims=True)
            m_new = jnp.maximum(state[h][0], cm)
            probs.append((m_new, jnp.exp(s - m_new).astype(bf16)))
        new = []
        for h in range(MLA_HEADS):
            vsl = slice(h * MLA_VT_ROWS, (h + 1) * MLA_VT_ROWS)
            m, acc = state[h]
            m_new, p = probs[h]
            acc = jnp.exp(m - m_new) * acc + _dot(vt_ref[vsl, pl.ds(start, tk)], p)
            new.append((m_new, acc))
        return tuple(new)

    def body(t, carry):
        state, cm_a = carry
        cm_b = issue_scores(per * t + 1, sb_ref)
        state = consume(per * t, sa_ref, cm_a, state)
        cm_a = issue_scores(per * t + 2, sa_ref)
        state = consume(per * t + 1, sb_ref, cm_b, state)
        return state, cm_a

    assert per == 2
    init = tuple((jnp.full((1, tq), -jnp.inf, f32), jnp.zeros((MLA_VT_ROWS, tq), f32))
                 for _ in range(MLA_HEADS))
    carry = (init, issue_scores(0, sa_ref))
    carry = lax.fori_loop(0, i // 4, lambda u, c: body(4 * u + 3, body(4 * u + 2, body(4 * u + 1, body(4 * u, c)))),
                          carry)
    t4 = 4 * (i // 4)
    carry = lax.fori_loop(0, (i % 4) // 2, lambda _, c: body(t4 + 1, body(t4, c)), carry)
    state, _ = lax.fori_loop(0, i % 2, lambda _, c: body(i - 1, c), carry)
    issue_scores(per * i + 1, sb_ref)
    state = consume(per * i, sa_ref, None, state, diag_offset=0)
    state = consume(per * i + 1, sb_ref, None, state, diag_offset=tk)
    o_t = jnp.concatenate([acc[:MLA_V] / acc[MLA_V:MLA_V + 1] for _, acc in state], axis=0)
    o_ref[...] = o_t.T.astype(bf16)


def _mla_attn(q_t, k, v_t, batch, seq):
    tq = ATTN_TILE
    nq = seq // tq
    hp = MLA_HEADS * MLA_HEAD_PAD
    hv = MLA_HEADS * MLA_V
    hvt = MLA_HEADS * MLA_VT_ROWS
    return pl.pallas_call(
        _mla_attn_kernel,
        out_shape=jax.ShapeDtypeStruct((batch * seq, hv), bf16),
        grid=(batch, nq),
        in_specs=[pl.BlockSpec((None, hp, tq), lambda b, i: (b, 0, i)),
                  pl.BlockSpec((seq, hp), lambda b, i: (b, 0)),
                  pl.BlockSpec((None, hvt, seq), lambda b, i: (b, 0, 0))],
        out_specs=pl.BlockSpec((tq, hv), lambda b, i: (b * nq + i, 0)),
        scratch_shapes=[pltpu.VMEM((MLA_HEADS, ATTN_KEY_TILE, tq), f32),
                        pltpu.VMEM((MLA_HEADS, ATTN_KEY_TILE, tq), f32)],
        compiler_params=_cparams(2),
        name="mla_attn",
    )(q_t, k, v_t)


def _sg_kernel(z_ref, lng_ref, lnb_ref, ws_ref, bst_ref, o_ref):
    z = z_ref[...]
    tm = z.shape[0]
    uv = 0.5 * z * (1.0 + jnp.tanh(np.sqrt(2.0 / np.pi) * (z + 0.044715 * (z * z * z))))
    u = uv[:, :SG_WIDTH]
    v = uv[:, SG_WIDTH:]
    mu = jnp.mean(v, axis=-1, keepdims=True)
    vc = v - mu
    vn = vc * lax.rsqrt(jnp.mean(vc * vc, axis=-1, keepdims=True) + NORM_EPS)
    vn = (vn * lng_ref[...] + lnb_ref[...]).astype(bf16)

    gw = SG_WIDTH // SG_GROUPS
    lane_group = _iota((SG_CHUNK, SG_WIDTH), 1) // gw
    tril = _iota((SG_CHUNK, SG_CHUNK), 0) >= _iota((SG_CHUNK, SG_CHUNK), 1)
    bias = jnp.zeros((SG_CHUNK, SG_WIDTH), f32)
    wm = []
    for g in range(SG_GROUPS):
        wm.append(jnp.where(tril, ws_ref[g], 0.0).astype(bf16))
        bias = jnp.where(lane_group == g,
                         jnp.broadcast_to(bst_ref[:, g:g + 1], (SG_CHUNK, SG_WIDTH)), bias)
    w_cat = jnp.concatenate(wm, axis=1)
    zero_b = jnp.zeros((SG_CHUNK, SG_WIDTH), bf16)
    for c in range(tm // SG_CHUNK):
        rows = slice(c * SG_CHUNK, (c + 1) * SG_CHUNK)
        vch = vn[rows]
        v_bd = jnp.concatenate([jnp.where(lane_group == g, vch, zero_b) for g in range(SG_GROUPS)], axis=0)
        o_ref[rows, :] = (u[rows] * (bias + _dot(w_cat, v_bd))).astype(bf16)


def _sg(z_sg, lng, lnb, ws, bst):
    t = z_sg.shape[0]
    tm = TOKEN_TILE
    return pl.pallas_call(
        _sg_kernel,
        out_shape=jax.ShapeDtypeStruct((t, SG_WIDTH), bf16),
        grid=(t // tm,),
        in_specs=[pl.BlockSpec((tm, W_SG), lambda i: (i, 0)),
                  _const_spec((1, SG_WIDTH)),
                  _const_spec((1, SG_WIDTH)),
                  _const_spec((SG_GROUPS, SG_CHUNK, SG_CHUNK)),
                  _const_spec((SG_CHUNK, SG_GROUPS))],
        out_specs=pl.BlockSpec((tm, SG_WIDTH), lambda i: (i, 0)),
        compiler_params=_cparams(1),
        name="spatial_gating",
    )(z_sg, lng, lnb, ws, bst)


GLA_QK = GLA_HEADS * GLA_DK
GLA_V = GLA_HEADS * GLA_DV
GLA_SUB = 16


def _gla_kernel(z_ref, wg_ref, bg_ref, no_ref, o_ref,
                st_ref, cum_ref, qd_ref, upd_ref, dl_ref, stc_ref, oraw_ref):
    tc = z_ref.shape[0]
    c_len = GLA_CHUNK
    n_chunks = tc // c_len

    @pl.when(pl.program_id(1) == 0)
    def _():
        st_ref[...] = jnp.zeros_like(st_ref)

    zg = _dot(z_ref[:, 768:896].astype(bf16), wg_ref[...]) + bg_ref[...]
    log_a = (jnp.minimum(zg, 0.0) - jnp.log1p(jnp.exp(-jnp.abs(zg)))) * (1.0 / GLA_GATE_TEMP)

    tri = (_iota((c_len, c_len), 0) >= _iota((c_len, c_len), 1)).astype(f32)
    for c in range(n_chunks):
        cum_ref[c * c_len:(c + 1) * c_len, :] = _dot_sel(tri, log_a[c * c_len:(c + 1) * c_len])

    expand = (_iota((GLA_QK, GLA_V), 0) // GLA_DK == _iota((GLA_QK, GLA_V), 1) // GLA_DV).astype(bf16)
    st_mask = (_iota((GLA_V, GLA_QK), 0) // GLA_DV == _iota((GLA_V, GLA_QK), 1) // GLA_DK).astype(f32)
    sub = GLA_SUB
    row = _iota((sub, GLA_QK), 0)

    def intra(c):
        base = c * c_len
        rows = slice(base, base + c_len)
        cum = cum_ref[rows, :]
        q = z_ref[rows, 0:128] * (GLA_DK ** -0.5)
        k = z_ref[rows, 128:256]
        v = z_ref[rows, 256:512]
        v_t = v.T.astype(bf16)
        qd_ref[rows, :] = (q * jnp.exp(cum)).astype(bf16)
        last = cum[c_len - 1:c_len, :]
        dl_ref[c] = jnp.broadcast_to(jnp.exp(last), (8, GLA_QK))
        upd_ref[c] = _dot(v_t, (k * jnp.exp(last - cum)).astype(bf16)) * st_mask

        n_sub = c_len // sub
        parts = [None]
        for sb in range(1, n_sub):
            r0 = sb * sub
            kx = k[0:r0] * jnp.exp(cum[r0 - 1:r0, :] - cum[0:r0])
            parts.append((_dot(v_t[:, 0:r0], kx.astype(bf16)) * st_mask).astype(bf16))
        rs = []
        for sb in range(n_sub):
            r0 = sb * sub
            q_i = q[r0:r0 + sub]
            cum_i = cum[r0:r0 + sub]
            ts = []
            for jj in range(sub):
                kj = z_ref[base + r0 + jj:base + r0 + jj + 1, 128:256]
                cj = cum_ref[base + r0 + jj:base + r0 + jj + 1, :]
                dec = jnp.exp(jnp.where(row >= jj, cum_i - cj, -1e30))
                ts.append((q_i * kj * dec).astype(bf16))
            rs.append(_dot(jnp.concatenate(ts, axis=0), expand))
        blocks = []
        for sb in range(n_sub):
            r0 = sb * sub
            o_sb = jnp.zeros((sub, GLA_V), f32)
            for jj in range(sub):
                o_sb = o_sb + rs[sb][jj * sub:(jj + 1) * sub] * z_ref[base + r0 + jj:base + r0 + jj + 1, 256:512]
            if sb > 0:
                qx = q[r0:r0 + sub] * jnp.exp(cum[r0:r0 + sub] - cum[r0 - 1:r0, :])
                o_sb = o_sb + _dot_nt(qx.astype(bf16), parts[sb])
            blocks.append(o_sb)
        oraw_ref[rows, :] = jnp.concatenate(blocks, axis=0)

    for c in range(n_chunks):
        intra(c)

    def recur(c, st):
        stc_ref[c] = st.astype(bf16)
        return dl_ref[c][0:1, :] * st + upd_ref[c]

    st_ref[...] = lax.fori_loop(0, n_chunks, recur, st_ref[...])
    for c in range(n_chunks):
        rows = slice(c * c_len, (c + 1) * c_len)
        oraw_ref[rows, :] += _dot_nt(qd_ref[rows, :], stc_ref[c])

    o = oraw_ref[...]
    ms = _group_stat(o * o, _group_mean_matrix(GLA_V, GLA_DV))
    og = z_ref[:, 512:768]
    o_ref[...] = (o * lax.rsqrt(ms + NORM_EPS) * no_ref[...] * (og * _sigmoid(og))).astype(bf16)


def _gla(z_gla, wg, bg, no, batch, seq):
    tc = SCAN_TILE
    nt = seq // tc
    n_chunks = tc // GLA_CHUNK
    return pl.pallas_call(
        _gla_kernel,
        out_shape=jax.ShapeDtypeStruct((batch * seq, GLA_V), bf16),
        grid=(batch, nt),
        in_specs=[pl.BlockSpec((tc, W_GLA), lambda b, i: (b * nt + i, 0)),
                  _const_spec((128, GLA_QK)),
                  _const_spec((1, GLA_QK)),
                  _const_spec((1, GLA_V))],
        out_specs=pl.BlockSpec((tc, GLA_V), lambda b, i: (b * nt + i, 0)),
        scratch_shapes=[pltpu.VMEM((GLA_V, GLA_QK), f32),
                        pltpu.VMEM((tc, GLA_QK), f32),
                        pltpu.VMEM((tc, GLA_QK), bf16),
                        pltpu.VMEM((n_chunks, GLA_V, GLA_QK), f32),
                        pltpu.VMEM((n_chunks, 8, GLA_QK), f32),
                        pltpu.VMEM((n_chunks, GLA_V, GLA_QK), bf16),
                        pltpu.VMEM((tc, GLA_V), f32)],
        compiler_params=_cparams(2),
        name="gla",
    )(z_gla, wg, bg, no)


DN_W = DN_HEADS * DN_DK
DN_PAIRS = DN_HEADS // 2


def _stack_heads(x):
    lane = _iota(x.shape, 1)
    zero = jnp.zeros_like(x)
    return jnp.concatenate([jnp.where(lane < DN_DK, x, zero), jnp.where(lane >= DN_DK, x, zero)], axis=0)


def _dot_3x(a, b):
    a_hi = a.astype(bf16)
    b_hi = b.astype(bf16)
    a_lo = (a - a_hi.astype(f32)).astype(bf16)
    b_lo = (b - b_hi.astype(f32)).astype(bf16)
    return _dot(a_hi, b_hi) + (_dot(a_hi, b_lo) + _dot(a_lo, b_hi))


def _dn_pre_kernel(y_ref, ab_ref, alog_ref, dtb_ref,
                   a_ref, attn_ref, rhs_ref, qd_ref, kd_ref, egl_ref,
                   q_s, k_s, beta_s, g_s, gam_s, gamr_s):
    tc = y_ref.shape[0]
    c_len = DN_CHUNK

    sum_mat = _group_mean_matrix(DN_W, DN_DK) * float(DN_DK)
    q = y_ref[:, 0:DN_W]
    k = y_ref[:, DN_W:2 * DN_W]
    q_s[...] = q * lax.rsqrt(_group_stat(q * q, sum_mat) + NORM_EPS) * (DN_DK ** -0.5)
    k_s[...] = k * lax.rsqrt(_group_stat(k * k, sum_mat) + NORM_EPS)

    ab = ab_ref[...]
    lane_head = _iota((tc, DN_W), 1) // DN_DK
    a_rep = jnp.zeros((tc, DN_W), f32)
    b_rep = jnp.zeros((tc, DN_W), f32)
    for h in range(DN_HEADS):
        a_rep = jnp.where(lane_head == h, jnp.broadcast_to(ab[:, h:h + 1], (tc, DN_W)), a_rep)
        b_rep = jnp.where(lane_head == h,
                          jnp.broadcast_to(ab[:, DN_HEADS + h:DN_HEADS + h + 1], (tc, DN_W)), b_rep)
    beta_s[...] = _sigmoid(b_rep)
    g_s[...] = -jnp.exp(alog_ref[...]) * _softplus(a_rep + dtb_ref[...])

    r64 = _iota((c_len, c_len), 0)
    c64 = _iota((c_len, c_len), 1)
    tri = (r64 >= c64).astype(bf16)
    ones = jnp.ones((c_len, c_len), bf16)
    row = _iota((c_len, DN_W), 0)
    colj = _iota((c_len, DN_W), 1) % c_len
    incl = row >= colj
    strict = row > colj

    n_chunks = tc // c_len
    g_parts = _split_bf16(g_s[...], 3)
    zero_b = jnp.zeros((c_len, DN_W), bf16)
    total = lambda parts: functools.reduce(lambda a, b: a + b, reversed(parts))
    for c in range(n_chunks):
        rows = slice(c * c_len, (c + 1) * c_len)
        gam_s[rows, :] = total([_dot(tri, part[rows]) for part in g_parts])
    for c in range(n_chunks):
        rows = slice(c * c_len, (c + 1) * c_len)
        gamr_s[rows, :] = total([_dot(ones, jnp.where(row <= colj, part[rows], zero_b)) for part in g_parts])

    def chunk(c):
        rows = slice(c * c_len, (c + 1) * c_len)
        gam = gam_s[rows, :]
        gam_row = gamr_s[rows, :]
        dec_incl = jnp.exp(jnp.where(incl, gam - gam_row, -1e30))
        dec_strict = jnp.where(strict, dec_incl, 0.0)
        eg = jnp.exp(gam)
        g_last = gam[c_len - 1:c_len, :]
        q = q_s[rows, :]
        k = k_s[rows, :]
        beta = beta_s[rows, :]
        bv = beta * y_ref[rows, 2 * DN_W:3 * DN_W]
        bk = beta * eg * k
        kdec = k * jnp.exp(g_last - gam)
        qd_ref[rows, :] = (q * eg).astype(bf16)
        egl_ref[c] = jnp.broadcast_to(jnp.exp(g_last), (8, DN_W))
        qb = q.astype(bf16)
        kb = k.astype(bf16)
        for p in range(DN_PAIRS):
            ps = slice(p * 128, (p + 1) * 128)
            kbp = kb[:, ps]
            k_bd = _stack_heads(kbp)
            a_pair = beta[:, ps] * _dot_nt(kbp, k_bd) * dec_strict[:, ps]
            a_ref[c * DN_HEADS + 2 * p] = a_pair[:, :DN_CHUNK]
            a_ref[c * DN_HEADS + 2 * p + 1] = a_pair[:, DN_CHUNK:]
            attn_ref[c * DN_PAIRS + p] = (_dot_nt(qb[:, ps], k_bd) * dec_incl[:, ps]).astype(bf16)
            kd_ref[c * DN_PAIRS + p] = _stack_heads(kdec[:, ps]).T.astype(bf16)
            rhs_ref[c * DN_PAIRS + p] = jnp.concatenate(
                [_stack_heads(bv[:, ps]), _stack_heads(bk[:, ps])], axis=1)

    for c in range(n_chunks):
        chunk(c)


def _dn_pre(z_dn, alog, dtb, batch, seq):
    tc = SCAN_TILE
    nt = seq // tc
    t = batch * seq
    npt = (tc // DN_CHUNK) * DN_PAIRS
    n_pair = (t // DN_CHUNK) * DN_PAIRS
    scr = lambda: pltpu.VMEM((tc, DN_W), f32)
    tok = lambda: pl.BlockSpec((tc, DN_W), lambda b, i: (b * nt + i, 0))
    mat = lambda r, w: pl.BlockSpec((npt, r, w), lambda b, i: (b * nt + i, 0, 0))
    return pl.pallas_call(
        _dn_pre_kernel,
        out_shape=(jax.ShapeDtypeStruct((2 * n_pair, DN_CHUNK, DN_CHUNK), f32),
                   jax.ShapeDtypeStruct((n_pair, DN_CHUNK, 128), bf16),
                   jax.ShapeDtypeStruct((n_pair, 128, 256), f32),
                   jax.ShapeDtypeStruct((t, DN_W), bf16),
                   jax.ShapeDtypeStruct((n_pair, 128, 128), bf16),
                   jax.ShapeDtypeStruct((t // DN_CHUNK, 8, DN_W), f32)),
        grid=(batch, nt),
        in_specs=[pl.BlockSpec((tc, 3 * DN_W), lambda b, i: (b * nt + i, 0)),
                  pl.BlockSpec((tc, 128), lambda b, i: (b * nt + i, 4 * DN_W // 128)),
                  _const_spec((1, DN_W)),
                  _const_spec((1, DN_W))],
        out_specs=(pl.BlockSpec((2 * npt, DN_CHUNK, DN_CHUNK), lambda b, i: (b * nt + i, 0, 0)),
                   mat(DN_CHUNK, 128), mat(128, 256), tok(), mat(128, 128),
                   pl.BlockSpec((tc // DN_CHUNK, 8, DN_W), lambda b, i: (b * nt + i, 0, 0))),
        scratch_shapes=[scr(), scr(), scr(), scr(), scr(), scr()],
        compiler_params=_cparams(2),
        name="dn_pre",
    )(z_dn, z_dn, alog, dtb)


def _dn_solve_kernel(a_ref, t_ref):
    c_len = DN_CHUNK
    blk = DN_INV_BLOCK
    vshape = a_ref.shape[1:]

    def row_pair(ip, carry):
        i0 = 2 * ip
        i1 = i0 + 1
        for mb in range(c_len // blk):
            m0 = mb * blk

            def col_j(j, accs):
                a0 = a_ref[i0 * c_len + j]
                a1 = a_ref[i1 * c_len + j]
                ts = [t_ref[j * c_len + m0 + m] for m in range(blk)]
                return (tuple(accs[0][m] - a0 * ts[m] for m in range(blk)),
                        tuple(accs[1][m] - a1 * ts[m] for m in range(blk)))

            zeros = tuple(jnp.zeros(vshape, f32) for _ in range(blk))
            acc0, acc1 = lax.fori_loop(m0, i0, col_j, (zeros, zeros))
            a10 = a_ref[i1 * c_len + i0]
            for m in range(blk):
                t0 = acc0[m] + jnp.where(i0 == m0 + m, 1.0, 0.0)
                t_ref[i0 * c_len + m0 + m] = t0
                t_ref[i1 * c_len + m0 + m] = acc1[m] - a10 * t0 + jnp.where(i1 == m0 + m, 1.0, 0.0)
        return carry

    lax.fori_loop(0, c_len // 2, row_pair, 0)


def _dn_solve(a_t):
    n_ent, n_grp, lanes = a_t.shape
    g = min(n_grp, 8)
    spec = lambda: pl.BlockSpec((n_ent, g, lanes), lambda i: (0, i, 0), pipeline_mode=pl.Buffered(1))
    return pl.pallas_call(
        _dn_solve_kernel,
        out_shape=jax.ShapeDtypeStruct(a_t.shape, f32),
        grid=(n_grp // g,),
        in_specs=[spec()],
        out_specs=spec(),
        compiler_params=_cparams(1),
        name="dn_solve",
    )(a_t)


DN_SCAN_SEQS = 2


def _dn_scan_kernel(t_ref, attn_ref, rhs_ref, qd_ref, kd_ref, egl_ref, zg_ref, no_ref,
                    o_ref, st_ref, oraw_ref):
    n_seq, tc = qd_ref.shape[0], qd_ref.shape[1]
    c_len = DN_CHUNK
    seqs = range(n_seq)
    pairs = range(DN_PAIRS)
    psl = [slice(p * 128, (p + 1) * 128) for p in pairs]

    @pl.when(pl.program_id(1) == 0)
    def _():
        st_ref[...] = jnp.zeros_like(st_ref)

    def solve(s, c):
        t_pair = [jnp.concatenate([t_ref[s, c * DN_HEADS + 2 * p], t_ref[s, c * DN_HEADS + 2 * p + 1]], axis=-1)
                  for p in pairs]
        return [_dot_3x(t_pair[p], rhs_ref[s, c * DN_PAIRS + p]) for p in pairs]

    n_chunks = tc // c_len
    sols = [solve(s, 0) for s in seqs]
    for c in range(n_chunks):
        rows = slice(c * c_len, (c + 1) * c_len)
        sts = [[st_ref[s, p] for p in pairs] for s in seqs]
        xq = [[_dot(jnp.concatenate([sols[s][p][:, 128:].astype(bf16), qd_ref[s, rows, psl[p]]], axis=0),
                    sts[s][p].astype(bf16)) for p in pairs] for s in seqs]
        nxt = [solve(s, c + 1) for s in seqs] if c + 1 < n_chunks else None
        u_bd = [[_stack_heads((sols[s][p][:, :128] - xq[s][p][:c_len]).astype(bf16)) for p in pairs]
                for s in seqs]
        au = [[_dot(jnp.concatenate([attn_ref[s, c * DN_PAIRS + p], kd_ref[s, c * DN_PAIRS + p]], axis=0),
                    u_bd[s][p]) for p in pairs] for s in seqs]
        for s in seqs:
            eg_last = egl_ref[s, c][0:1, :]
            for p in pairs:
                st_ref[s, p] = eg_last[:, psl[p]] * sts[s][p] + au[s][p][c_len:]
            oraw_ref[s, rows, :] = jnp.concatenate([xq[s][p][c_len:] + au[s][p][:c_len] for p in pairs], axis=-1)
        sols = nxt

    mean_mat = _group_mean_matrix(DN_W, DN_DV)
    for s in seqs:
        o = oraw_ref[s]
        ms = _group_stat(o * o, mean_mat)
        zg = zg_ref[s]
        o_ref[s] = (o * lax.rsqrt(ms + NORM_EPS) * no_ref[...] * (zg * _sigmoid(zg))).astype(bf16)


def _dn_scan(t_all, attn, rhs, qd, kd, egl, z_dn, no, batch, seq):
    tc = SCAN_TILE
    nt = seq // tc
    ns = DN_SCAN_SEQS if batch % DN_SCAN_SEQS == 0 else 1
    g = batch // ns
    npt = (tc // DN_CHUNK) * DN_PAIRS
    n_chunk_seq = seq // DN_CHUNK

    def grouped(x, per_seq):
        return x.reshape((g, ns, per_seq) + x.shape[1:])

    def spec(per_tile, tail, last_block=0):
        zeros = (0,) * (len(tail) - 1)
        return pl.BlockSpec((None, ns, per_tile) + tail, lambda b, i: (b, 0, i) + zeros + (last_block,))

    out = pl.pallas_call(
        _dn_scan_kernel,
        out_shape=jax.ShapeDtypeStruct((g, ns, seq, DN_W), bf16),
        grid=(g, nt),
        in_specs=[spec(2 * npt, (DN_CHUNK, DN_CHUNK)),
                  spec(npt, (DN_CHUNK, 128)), spec(npt, (128, 256)), spec(tc, (DN_W,)), spec(npt, (128, 128)),
                  spec(tc // DN_CHUNK, (8, DN_W)),
                  spec(tc, (DN_W,), last_block=3),
                  _const_spec((1, DN_W))],
        out_specs=spec(tc, (DN_W,)),
        scratch_shapes=[pltpu.VMEM((ns, DN_PAIRS, 128, 128), f32),
                        pltpu.VMEM((ns, tc, DN_W), f32)],
        compiler_params=_cparams(2),
        name="dn_scan",
    )(t_all.reshape(g, ns, n_chunk_seq * DN_HEADS, DN_CHUNK, DN_CHUNK), grouped(attn, n_chunk_seq * DN_PAIRS),
      grouped(rhs, n_chunk_seq * DN_PAIRS), grouped(qd, seq), grouped(kd, n_chunk_seq * DN_PAIRS),
      grouped(egl, n_chunk_seq), grouped(z_dn, seq), no)
    return out.reshape(batch * seq, DN_W)


def _dn(z_dn, alog, dtb, no, batch, seq):
    a_all, attn, rhs, qd, kd, egl = _dn_pre(z_dn, alog, dtb, batch, seq)
    n_prob = a_all.shape[0]
    n_ent = DN_CHUNK * DN_CHUNK
    a_t = a_all.reshape(n_prob, n_ent).T.reshape(n_ent, n_prob // 128, 128)
    ns = DN_SCAN_SEQS if batch % DN_SCAN_SEQS == 0 else 1
    t_all = _dn_solve(a_t).reshape(n_ent, n_prob).T.reshape(
        batch // ns, ns, n_prob // batch, DN_CHUNK, DN_CHUNK)
    return _dn_scan(t_all, attn, rhs, qd, kd, egl, z_dn, no, batch, seq)


def _merge_kernel(x_ref, g_ref, oa_ref, ob_ref, oc_ref, od_ref, wg_ref, wb_ref, wo_ref, out_ref):
    x = x_ref[...]
    h = _rms(x, g_ref[...]).astype(bf16)
    merged = jnp.zeros(x.shape, f32)
    for n, o_ref in enumerate((oa_ref, ob_ref, oc_ref, od_ref)):
        gate = _sigmoid(_dot(h, wg_ref[:, n * D_MODEL:(n + 1) * D_MODEL]))
        merged = merged + gate * _dot(o_ref[...], wb_ref[n])
    out_ref[...] = x + _dot(merged.astype(bf16), wo_ref[...])


def _merge(x2, g, oa, ob, oc, od, wg, wb, wo):
    t = x2.shape[0]
    tm = TOKEN_TILE
    tok = lambda w: pl.BlockSpec((tm, w), lambda i: (i, 0))
    return pl.pallas_call(
        _merge_kernel,
        out_shape=jax.ShapeDtypeStruct((t, D_MODEL), f32),
        grid=(t // tm,),
        in_specs=[tok(D_MODEL), _const_spec((1, D_MODEL)),
                  tok(BRANCH_WIDTH), tok(BRANCH_WIDTH), tok(BRANCH_WIDTH), tok(BRANCH_WIDTH),
                  _const_spec((D_MODEL, N_BRANCH * D_MODEL)),
                  _const_spec((N_BRANCH, BRANCH_WIDTH, D_MODEL)),
                  _const_spec((D_MODEL, D_MODEL))],
        out_specs=tok(D_MODEL),
        compiler_params=_cparams(1),
        name="merge",
    )(x2, g, oa, ob, oc, od, wg, wb, wo)


def _memkv_kernel(mem_ref, g_ref, wk_ref, wv_ref, k_ref, v_ref):
    mn = _rms(mem_ref[...], g_ref[...]).astype(bf16)
    k_ref[...] = _dot(mn, wk_ref[...]).astype(bf16)
    v_ref[...] = _dot(mn, wv_ref[...]).astype(bf16)


def _memkv(mem2, g, wk, wv, mem_len):
    t = mem2.shape[0]
    blk = lambda: pl.BlockSpec((mem_len, D_MODEL), lambda i: (i, 0))
    return pl.pallas_call(
        _memkv_kernel,
        out_shape=(jax.ShapeDtypeStruct((t, D_MODEL), bf16), jax.ShapeDtypeStruct((t, D_MODEL), bf16)),
        grid=(t // mem_len,),
        in_specs=[blk(), _const_spec((1, D_MODEL)),
                  _const_spec((D_MODEL, D_MODEL)), _const_spec((D_MODEL, D_MODEL))],
        out_specs=(blk(), blk()),
        compiler_params=_cparams(1),
        name="mem_kv",
    )(mem2, g, wk, wv)


def _xattn_kernel(x_ref, g_ref, k_ref, v_ref, wq_ref, wo_ref, out_ref):
    x = x_ref[...]
    h = _rms(x, g_ref[...]).astype(bf16)
    q = (_dot(h, wq_ref[...]) * (XATTN_DIM ** -0.5)).astype(bf16)
    hsl = [slice(hh * XATTN_DIM, (hh + 1) * XATTN_DIM) for hh in range(XATTN_HEADS)]
    scores = [_dot_nt(q[:, sl], k_ref[:, sl]) for sl in hsl]
    probs = []
    for s in scores:
        p = jnp.exp(s - jnp.max(s, axis=-1, keepdims=True))
        probs.append((p / jnp.sum(p, axis=-1, keepdims=True)).astype(bf16))
    o = jnp.concatenate([_dot(p, v_ref[:, sl]).astype(bf16) for p, sl in zip(probs, hsl)], axis=-1)
    out_ref[...] = x + _dot(o, wo_ref[...])


def _xattn(x2, g, kmem, vmem, wq, wo, batch, seq, mem_len):
    tm = TOKEN_TILE
    nt = seq // tm
    tok = pl.BlockSpec((tm, D_MODEL), lambda b, i: (b * nt + i, 0))
    memb = lambda: pl.BlockSpec((mem_len, D_MODEL), lambda b, i: (b, 0))
    return pl.pallas_call(
        _xattn_kernel,
        out_shape=jax.ShapeDtypeStruct((batch * seq, D_MODEL), f32),
        grid=(batch, nt),
        in_specs=[tok, _const_spec((1, D_MODEL)), memb(), memb(),
                  _const_spec((D_MODEL, D_MODEL)), _const_spec((D_MODEL, D_MODEL))],
        out_specs=tok,
        compiler_params=_cparams(2),
        name="xattn",
    )(x2, g, kmem, vmem, wq, wo)


def _mlp_kernel(x_ref, g_ref, wu_ref, wd_ref, gf_ref, out_ref, *, final):
    x = x_ref[...]
    h = _rms(x, g_ref[...]).astype(bf16)
    acc = x
    for c in range(MLP_HIDDEN // MLP_CHUNK):
        sl = slice(c * MLP_CHUNK, (c + 1) * MLP_CHUNK)
        a = jnp.maximum(_dot(h, wu_ref[:, sl]), 0.0)
        acc = acc + _dot((a * a).astype(bf16), wd_ref[sl, :])
    if final:
        acc = _rms(acc, gf_ref[...])
    out_ref[...] = acc


def _mlp(x2, g, wu, wd, gf, final):
    t = x2.shape[0]
    tm = TOKEN_TILE
    tok = pl.BlockSpec((tm, D_MODEL), lambda i: (i, 0))
    return pl.pallas_call(
        functools.partial(_mlp_kernel, final=final),
        out_shape=jax.ShapeDtypeStruct((t, D_MODEL), f32),
        grid=(t // tm,),
        in_specs=[tok, _const_spec((1, D_MODEL)),
                  _const_spec((D_MODEL, MLP_HIDDEN)), _const_spec((MLP_HIDDEN, D_MODEL)),
                  _const_spec((1, D_MODEL))],
        out_specs=tok,
        compiler_params=_cparams(1),
        name="mlp",
    )(x2, g, wu, wd, gf)


def _pad_cols(w, width):
    return jnp.pad(w, ((0, 0), (0, width - w.shape[1])))


def _prep_w_in(w_in):
    col = lambda i: w_in[:, IN_OFFS[i]:IN_OFFS[i + 1]]
    zeros = lambda n: jnp.zeros((D_MODEL, n), w_in.dtype)
    (cq, ckv, kpe, sg_uv, gq, gk, gv, g_lr, g_og, dq, dk, dv, da, db, dz) = (col(i) for i in range(15))
    small = jnp.concatenate([
        cq, ckv, zeros(MLA_NOPE), kpe, zeros(MLA_HEAD_PAD - MLA_NOPE - MLA_ROPE),
        sg_uv,
        gq, gk, gv, g_og, _pad_cols(g_lr, 128),
        dq, dk, dv, dz, _pad_cols(jnp.concatenate([da, db], axis=1), 128),
    ], axis=1)
    assert small.shape[1] == W_SMALL
    return small.astype(bf16), col(15).astype(bf16)


def _rope_inv_freq():
    inv_freq = ROPE_THETA ** (-jnp.arange(0, MLA_ROPE, 2, dtype=f32) / MLA_ROPE)
    return inv_freq[:, None]


def kernel(x, mem, positions, norm_mix, w_in, mla_norm_q, mla_norm_kv, mla_w_uq, mla_w_ukv, sg_ln_g, sg_ln_b, sg_w, sg_b, gla_w_gate, gla_b_gate, gla_norm, dn_conv, dn_a_log, dn_dt_bias, dn_norm, w_branch, w_out, norm_xattn, norm_mem, xattn_wq, xattn_wk, xattn_wv, xattn_wo, norm_mlp, w_up, w_down, norm_final):
    batch, seq, _ = x.shape
    mem_len = mem.shape[1]
    depth = w_in.shape[0]
    assert seq % TOKEN_TILE == 0 and seq % SCAN_TILE == 0 and seq % ATTN_TILE == 0
    x2 = x.reshape(batch * seq, D_MODEL)
    mem2 = mem.reshape(batch * mem_len, D_MODEL)
    pos2 = positions.reshape(batch * seq // TOKEN_TILE, 1, TOKEN_TILE)
    invf = _rope_inv_freq()
    row = lambda v: v[None, :]

    for l in range(depth):
        w_small, w_gates = _prep_w_in(w_in[l])
        z_mla, z_sg, z_gla, z_dn = _inproj(x2, row(norm_mix[l]), w_small, dn_conv[l], seq)

        wuq = jnp.pad(mla_w_uq[l].reshape(MLA_Q_RANK, MLA_HEADS, MLA_NOPE + MLA_ROPE),
                      ((0, 0), (0, 0), (0, MLA_HEAD_PAD - MLA_NOPE - MLA_ROPE)))
        wuq = wuq.reshape(MLA_Q_RANK, MLA_HEADS * MLA_HEAD_PAD).astype(bf16)
        wukv = mla_w_ukv[l].reshape(MLA_KV_RANK, MLA_HEADS, MLA_NOPE + MLA_V)
        wuk = jnp.pad(wukv[:, :, :MLA_NOPE], ((0, 0), (0, 0), (0, MLA_HEAD_PAD - MLA_NOPE)))
        wuk = wuk.reshape(MLA_KV_RANK, MLA_HEADS * MLA_HEAD_PAD).astype(bf16)
        wuv = wukv[:, :, MLA_NOPE:].reshape(MLA_KV_RANK, MLA_HEADS * MLA_V).astype(bf16)
        q_a, k_a, v_a = _mla_pre(z_mla, pos2, invf, row(mla_norm_q[l]), row(mla_norm_kv[l]), wuq, wuk, wuv,
                                 batch, seq)
        o_a = _mla_attn(q_a, k_a, v_a, batch, seq)

        o_b = _sg(z_sg, row(sg_ln_g[l]), row(sg_ln_b[l]), sg_w[l], sg_b[l].T)

        wg = jnp.pad(gla_w_gate[l], ((0, 128 - GLA_GATE_RANK), (0, 0))).astype(bf16)
        o_c = _gla(z_gla, wg, row(gla_b_gate[l]), row(jnp.tile(gla_norm[l], GLA_HEADS)), batch, seq)

        o_d = _dn(z_dn, row(jnp.repeat(dn_a_log[l], DN_DK)), row(jnp.repeat(dn_dt_bias[l], DN_DK)),
                  row(jnp.tile(dn_norm[l], DN_HEADS)), batch, seq)

        x2 = _merge(x2, row(norm_mix[l]), o_a, o_b, o_c, o_d, w_gates,
                    w_branch[l].astype(bf16), w_out[l].astype(bf16))

        k_m, v_m = _memkv(mem2, row(norm_mem[l]), xattn_wk[l].astype(bf16), xattn_wv[l].astype(bf16), mem_len)
        x2 = _xattn(x2, row(norm_xattn[l]), k_m, v_m, xattn_wq[l].astype(bf16), xattn_wo[l].astype(bf16),
                    batch, seq, mem_len)
        x2 = _mlp(x2, row(norm_mlp[l]), w_up[l].astype(bf16), w_down[l].astype(bf16), row(norm_final),
                  final=(l == depth - 1))
    return x2.reshape(batch, seq, D_MODEL)
```

```python
import functools

import jax
import jax.numpy as jnp
import numpy as np
from jax import lax
from jax.experimental import pallas as pl
from jax.experimental.pallas import tpu as pltpu

f32 = jnp.float32
bf16 = jnp.bfloat16
HIGHEST = lax.Precision.HIGHEST

D_MODEL = 1024
NORM_EPS = 1e-6
N_BRANCH = 4
BRANCH_WIDTH = 256

MLA_HEADS = 4
MLA_Q_RANK = 256
MLA_KV_RANK = 128
MLA_NOPE = 64
MLA_ROPE = 32
MLA_V = 64
ROPE_THETA = 10000.0
MLA_HEAD_PAD = 128
MLA_VT_ROWS = 80

SG_GROUPS = 4
SG_WIDTH = 256
SG_CHUNK = 128

GLA_HEADS = 4
GLA_DK = 32
GLA_DV = 64
GLA_GATE_RANK = 16
GLA_GATE_TEMP = 16.0
GLA_CHUNK = 64

DN_HEADS = 4
DN_DK = 64
DN_DV = 64
DN_CONV = 4
DN_CHUNK = 64
DN_INV_BLOCK = 16

XATTN_HEADS = 4
XATTN_DIM = D_MODEL // XATTN_HEADS
MLP_HIDDEN = 4 * D_MODEL

IN_SIZES = (
    MLA_Q_RANK, MLA_KV_RANK, MLA_ROPE,
    2 * SG_WIDTH,
    GLA_HEADS * GLA_DK, GLA_HEADS * GLA_DK, GLA_HEADS * GLA_DV,
    GLA_GATE_RANK, GLA_HEADS * GLA_DV,
    DN_HEADS * DN_DK, DN_HEADS * DN_DK, DN_HEADS * DN_DV,
    DN_HEADS, DN_HEADS, DN_HEADS * DN_DV,
    N_BRANCH * D_MODEL,
)
IN_OFFS = tuple(int(v) for v in np.cumsum((0,) + IN_SIZES))

W_MLA = 512
W_SG = 512
W_GLA = 896
W_DN = 1152
W_SMALL = W_MLA + W_SG + W_GLA + W_DN

TOKEN_TILE = 512
ATTN_TILE = 256
ATTN_KEY_TILE = 128
SCAN_TILE = 512
MLP_CHUNK = 1024
VMEM_LIMIT = 56 * 1024 * 1024


def _cparams(n_axes):
    return pltpu.CompilerParams(dimension_semantics=("arbitrary",) * n_axes,
                                vmem_limit_bytes=VMEM_LIMIT)


def _const_spec(shape):
    nd = len(shape)
    return pl.BlockSpec(shape, lambda *_: (0,) * nd, pipeline_mode=pl.Buffered(1))


def _rms(x, g):
    return x * lax.rsqrt(jnp.mean(x * x, axis=-1, keepdims=True) + NORM_EPS) * g


def _sigmoid(x):
    return 1.0 / (1.0 + jnp.exp(-x))


def _softplus(x):
    return jnp.maximum(x, 0.0) + jnp.log1p(jnp.exp(-jnp.abs(x)))


def _dot(a, b):
    return jnp.dot(a, b, preferred_element_type=f32)


def _split_bf16(x, terms):
    parts = []
    for _ in range(terms):
        p = x.astype(bf16)
        parts.append(p)
        x = x - p.astype(f32)
    return parts


def _dot_sel(sel, x, terms=3):
    sel = sel.astype(bf16)
    parts = [_dot(sel, p) for p in _split_bf16(x, terms)]
    return functools.reduce(lambda a, b: a + b, reversed(parts))


def _group_stat(x, mat, terms=2):
    mat = mat.astype(bf16)
    parts = [_dot(p, mat) for p in _split_bf16(x, terms)]
    return functools.reduce(lambda a, b: a + b, reversed(parts))


def _dot_nt(a, b):
    return lax.dot_general(a, b, (((1,), (1,)), ((), ())), preferred_element_type=f32)


def _dot_tn(a, b):
    return lax.dot_general(a, b, (((0,), (0,)), ((), ())), preferred_element_type=f32)


def _iota(shape, dim):
    return lax.broadcasted_iota(jnp.int32, shape, dim)


def _group_mean_matrix(width, group):
    r = _iota((width, width), 0) // group
    c = _iota((width, width), 1) // group
    return jnp.where(r == c, 1.0 / group, 0.0).astype(f32)


DN_CONV_TAPS = 4
DN_CONV_COLS = 768
DN_CONV_PAD = 8


def _inproj_kernel(x_ref, g_ref, w_ref, cw_ref, mla_ref, sg_ref, gla_ref, dn_ref, xbuf_ref, *,
                   tiles_per_seq):
    @pl.when(pl.program_id(0) % tiles_per_seq == 0)
    def _():
        xbuf_ref[0:DN_CONV_PAD, :] = jnp.zeros((DN_CONV_PAD, DN_CONV_COLS), f32)

    h = _rms(x_ref[...], g_ref[...]).astype(bf16)
    tm = h.shape[0]
    z_dn = _dot(h, w_ref[:, W_SMALL - W_DN:])
    dn_ref[:, DN_CONV_COLS:] = z_dn[:, DN_CONV_COLS:]
    xbuf_ref[DN_CONV_PAD:DN_CONV_PAD + tm, :] = z_dn[:, :DN_CONV_COLS]
    y = jnp.zeros((tm, DN_CONV_COLS), f32)
    for kk in range(DN_CONV_TAPS):
        y = y + cw_ref[kk:kk + 1, :] * xbuf_ref[pl.ds(DN_CONV_PAD - (DN_CONV_TAPS - 1) + kk, tm), :]
    xbuf_ref[0:DN_CONV_PAD, :] = xbuf_ref[tm:tm + DN_CONV_PAD, :]
    dn_ref[:, :DN_CONV_COLS] = y * _sigmoid(y)
    lo = 0
    for ref, width in ((mla_ref, W_MLA), (sg_ref, W_SG), (gla_ref, W_GLA)):
        ref[...] = _dot(h, w_ref[:, lo:lo + width])
        lo += width


def _inproj(x2, g, w_small, conv_w, seq):
    t = x2.shape[0]
    tm = TOKEN_TILE
    outs = tuple(jax.ShapeDtypeStruct((t, w), f32) for w in (W_MLA, W_SG, W_GLA, W_DN))
    return pl.pallas_call(
        functools.partial(_inproj_kernel, tiles_per_seq=seq // tm),
        out_shape=outs,
        grid=(t // tm,),
        in_specs=[pl.BlockSpec((tm, D_MODEL), lambda i: (i, 0)),
                  _const_spec((1, D_MODEL)),
                  _const_spec((D_MODEL, W_SMALL)),
                  _const_spec((DN_CONV_TAPS, DN_CONV_COLS))],
        out_specs=tuple(pl.BlockSpec((tm, w), lambda i: (i, 0)) for w in (W_MLA, W_SG, W_GLA, W_DN)),
        scratch_shapes=[pltpu.VMEM((tm + 2 * DN_CONV_PAD, DN_CONV_COLS), f32)],
        compiler_params=_cparams(1),
        name="inproj",
    )(x2, g, w_small, conv_w)


def _mla_pre_kernel(z_ref, pos_ref, invf_ref, nq_ref, nkv_ref, wuq_ref, wuk_ref, wuv_ref,
                    qt_ref, k_ref, vt_ref):
    z = z_ref[...]
    tm = z.shape[0]
    cq = z[:, :MLA_Q_RANK]
    ckv = z[:, MLA_Q_RANK:MLA_Q_RANK + MLA_KV_RANK]
    kpe = z[:, MLA_Q_RANK + MLA_KV_RANK:]

    half = MLA_ROPE // 2
    ang_t = invf_ref[...] * pos_ref[...].astype(f32)
    f_row = _iota((half, MLA_HEAD_PAD), 0)
    f_lane = _iota((half, MLA_HEAD_PAD), 1)
    place = ((f_lane == MLA_NOPE + f_row) | (f_lane == MLA_NOPE + half + f_row)).astype(bf16)

    def to_lanes(x_t):
        parts = [_dot_tn(p, place) for p in _split_bf16(x_t, 3)]
        return functools.reduce(lambda a, b: a + b, reversed(parts))

    cos = to_lanes(jnp.cos(ang_t))
    sin = to_lanes(jnp.sin(ang_t))
    lane = _iota((tm, MLA_HEAD_PAD), 1)
    in_x1 = (lane >= MLA_NOPE) & (lane < MLA_NOPE + half)
    in_x2 = (lane >= MLA_NOPE + half) & (lane < MLA_NOPE + MLA_ROPE)
    c_tab = jnp.where(in_x1 | in_x2, cos, 1.0)
    s1_tab = jnp.where(in_x1, -sin, 0.0)
    s2_tab = jnp.where(in_x2, sin, 0.0)

    def rope(t):
        return (t * c_tab + pltpu.roll(t, MLA_HEAD_PAD - half, 1) * s1_tab
                + pltpu.roll(t, half, 1) * s2_tab)

    scale = (MLA_NOPE + MLA_ROPE) ** -0.5
    q = _dot(_rms(cq, nq_ref[...]).astype(bf16), wuq_ref[...])
    kvn = _rms(ckv, nkv_ref[...]).astype(bf16)
    kn = _dot(kvn, wuk_ref[...])
    v_t = _dot(kvn, wuv_ref[...]).T
    kpe_rot = rope(kpe)
    q_rot = []
    for h in range(MLA_HEADS):
        sl = slice(h * MLA_HEAD_PAD, (h + 1) * MLA_HEAD_PAD)
        q_rot.append(rope(q[:, sl]) * scale)
        k_ref[:, sl] = (kn[:, sl] + kpe_rot).astype(bf16)
    qt_ref[...] = jnp.concatenate(q_rot, axis=-1).T.astype(bf16)
    pad = jnp.concatenate([jnp.ones((1, tm), f32), jnp.zeros((MLA_VT_ROWS - MLA_V - 1, tm), f32)], axis=0)
    rows = []
    for h in range(MLA_HEADS):
        rows += [v_t[h * MLA_V:(h + 1) * MLA_V], pad]
    vt_ref[...] = jnp.concatenate(rows, axis=0).astype(bf16)


def _mla_pre(z_mla, pos2, invf, nq, nkv, wuq, wuk, wuv, batch, seq):
    t = z_mla.shape[0]
    tm = TOKEN_TILE
    nt = seq // tm
    hp = MLA_HEADS * MLA_HEAD_PAD
    hv = MLA_HEADS * MLA_V
    hvt = MLA_HEADS * MLA_VT_ROWS
    return pl.pallas_call(
        _mla_pre_kernel,
        out_shape=(jax.ShapeDtypeStruct((batch, hp, seq), bf16),
                   jax.ShapeDtypeStruct((t, hp), bf16),
                   jax.ShapeDtypeStruct((batch, hvt, seq), bf16)),
        grid=(t // tm,),
        in_specs=[pl.BlockSpec((tm, W_MLA), lambda i: (i, 0)),
                  pl.BlockSpec((None, 1, tm), lambda i: (i, 0, 0)),
                  _const_spec((MLA_ROPE // 2, 1)),
                  _const_spec((1, MLA_Q_RANK)),
                  _const_spec((1, MLA_KV_RANK)),
                  _const_spec((MLA_Q_RANK, hp)),
                  _const_spec((MLA_KV_RANK, hp)),
                  _const_spec((MLA_KV_RANK, hv))],
        out_specs=(pl.BlockSpec((None, hp, tm), lambda i: (i // nt, 0, i % nt)),
                   pl.BlockSpec((tm, hp), lambda i: (i, 0)),
                   pl.BlockSpec((None, hvt, tm), lambda i: (i // nt, 0, i % nt))),
        compiler_params=_cparams(1),
        name="mla_pre",
    )(z_mla, pos2, invf, nq, nkv, wuq, wuk, wuv)


def _mla_attn_kernel(qt_ref, k_ref, vt_ref, o_ref, sa_ref, sb_ref):
    i = pl.program_id(1)
    tq = qt_ref.shape[1]
    tk = sa_ref.shape[1]
    per = tq // tk
    key = _iota((tk, tq), 0)
    qry = _iota((tk, tq), 1)

    def issue_scores(b, buf):
        start = pl.multiple_of(b * tk, tk)
        cms = []
        for h in range(MLA_HEADS):
            hsl = slice(h * MLA_HEAD_PAD, (h + 1) * MLA_HEAD_PAD)
            s = _dot(k_ref[pl.ds(start, tk), hsl], qt_ref[hsl, :])
            buf[h] = s
            cms.append(jnp.max(s, axis=0, keepdims=True))
        return tuple(cms)

    def consume(b, buf, cms, state, diag_offset=None):
        start = pl.multiple_of(b * tk, tk)
        probs = []
        for h in range(MLA_HEADS):
            s = buf[h]
            if diag_offset is None:
                cm = cms[h]
            else:
                s = jnp.where(key + diag_offset <= qry, s, -jnp.inf)
                cm = jnp.max(s, axis=0, keepdims=True)
            m_new = jnp.maximum(state[h][0], cm)
            probs.append((m_new, jnp.exp(s - m_new).astype(bf16)))
        new = []
        for h in range(MLA_HEADS):
            vsl = slice(h * MLA_VT_ROWS, (h + 1) * MLA_VT_ROWS)
            m, acc = state[h]
            m_new, p = probs[h]
            acc = jnp.exp(m - m_new) * acc + _dot(vt_ref[vsl, pl.ds(start, tk)], p)
            new.append((m_new, acc))
        return tuple(new)

    def body(t, carry):
        state, cm_a = carry
        cm_b = issue_scores(per * t + 1, sb_ref)
        state = consume(per * t, sa_ref, cm_a, state)
        cm_a = issue_scores(per * t + 2, sa_ref)
        state = consume(per * t + 1, sb_ref, cm_b, state)
        return state, cm_a

    assert per == 2
    init = tuple((jnp.full((1, tq), -jnp.inf, f32), jnp.zeros((MLA_VT_ROWS, tq), f32))
                 for _ in range(MLA_HEADS))
    carry = (init, issue_scores(0, sa_ref))
    carry = lax.fori_loop(0, i // 4, lambda u, c: body(4 * u + 3, body(4 * u + 2, body(4 * u + 1, body(4 * u, c)))),
                          carry)
    t4 = 4 * (i // 4)
    carry = lax.fori_loop(0, (i % 4) // 2, lambda _, c: body(t4 + 1, body(t4, c)), carry)
    state, _ = lax.fori_loop(0, i % 2, lambda _, c: body(i - 1, c), carry)
    issue_scores(per * i + 1, sb_ref)
    state = consume(per * i, sa_ref, None, state, diag_offset=0)
    state = consume(per * i + 1, sb_ref, None, state, diag_offset=tk)
    o_t = jnp.concatenate([acc[:MLA_V] / acc[MLA_V:MLA_V + 1] for _, acc in state], axis=0)
    o_ref[...] = o_t.T.astype(bf16)


def _mla_attn(q_t, k, v_t, batch, seq):
    tq = ATTN_TILE
    nq = seq // tq
    hp = MLA_HEADS * MLA_HEAD_PAD
    hv = MLA_HEADS * MLA_V
    hvt = MLA_HEADS * MLA_VT_ROWS
    return pl.pallas_call(
        _mla_attn_kernel,
        out_shape=jax.ShapeDtypeStruct((batch * seq, hv), bf16),
        grid=(batch, nq),
        in_specs=[pl.BlockSpec((None, hp, tq), lambda b, i: (b, 0, i)),
                  pl.BlockSpec((seq, hp), lambda b, i: (b, 0)),
                  pl.BlockSpec((None, hvt, seq), lambda b, i: (b, 0, 0))],
        out_specs=pl.BlockSpec((tq, hv), lambda b, i: (b * nq + i, 0)),
        scratch_shapes=[pltpu.VMEM((MLA_HEADS, ATTN_KEY_TILE, tq), f32),
                        pltpu.VMEM((MLA_HEADS, ATTN_KEY_TILE, tq), f32)],
        compiler_params=_cparams(2),
        name="mla_attn",
    )(q_t, k, v_t)


def _sg_kernel(z_ref, lng_ref, lnb_ref, ws_ref, bst_ref, o_ref):
    z = z_ref[...]
    tm = z.shape[0]
    uv = 0.5 * z * (1.0 + jnp.tanh(np.sqrt(2.0 / np.pi) * (z + 0.044715 * (z * z * z))))
    u = uv[:, :SG_WIDTH]
    v = uv[:, SG_WIDTH:]
    mu = jnp.mean(v, axis=-1, keepdims=True)
    vc = v - mu
    vn = vc * lax.rsqrt(jnp.mean(vc * vc, axis=-1, keepdims=True) + NORM_EPS)
    vn = (vn * lng_ref[...] + lnb_ref[...]).astype(bf16)

    gw = SG_WIDTH // SG_GROUPS
    lane_group = _iota((SG_CHUNK, SG_WIDTH), 1) // gw
    tril = _iota((SG_CHUNK, SG_CHUNK), 0) >= _iota((SG_CHUNK, SG_CHUNK), 1)
    bias = jnp.zeros((SG_CHUNK, SG_WIDTH), f32)
    wm = []
    for g in range(SG_GROUPS):
        wm.append(jnp.where(tril, ws_ref[g], 0.0).astype(bf16))
        bias = jnp.where(lane_group == g,
                         jnp.broadcast_to(bst_ref[:, g:g + 1], (SG_CHUNK, SG_WIDTH)), bias)
    w_cat = jnp.concatenate(wm, axis=1)
    zero_b = jnp.zeros((SG_CHUNK, SG_WIDTH), bf16)
    for c in range(tm // SG_CHUNK):
        rows = slice(c * SG_CHUNK, (c + 1) * SG_CHUNK)
        vch = vn[rows]
        v_bd = jnp.concatenate([jnp.where(lane_group == g, vch, zero_b) for g in range(SG_GROUPS)], axis=0)
        o_ref[rows, :] = (u[rows] * (bias + _dot(w_cat, v_bd))).astype(bf16)


def _sg(z_sg, lng, lnb, ws, bst):
    t = z_sg.shape[0]
    tm = TOKEN_TILE
    return pl.pallas_call(
        _sg_kernel,
        out_shape=jax.ShapeDtypeStruct((t, SG_WIDTH), bf16),
        grid=(t // tm,),
        in_specs=[pl.BlockSpec((tm, W_SG), lambda i: (i, 0)),
                  _const_spec((1, SG_WIDTH)),
                  _const_spec((1, SG_WIDTH)),
                  _const_spec((SG_GROUPS, SG_CHUNK, SG_CHUNK)),
                  _const_spec((SG_CHUNK, SG_GROUPS))],
        out_specs=pl.BlockSpec((tm, SG_WIDTH), lambda i: (i, 0)),
        compiler_params=_cparams(1),
        name="spatial_gating",
    )(z_sg, lng, lnb, ws, bst)


GLA_QK = GLA_HEADS * GLA_DK
GLA_V = GLA_HEADS * GLA_DV
GLA_SUB = 16


def _gla_kernel(z_ref, wg_ref, bg_ref, no_ref, o_ref,
                st_ref, cum_ref, qd_ref, upd_ref, dl_ref, stc_ref, oraw_ref):
    tc = z_ref.shape[0]
    c_len = GLA_CHUNK
    n_chunks = tc // c_len

    @pl.when(pl.program_id(1) == 0)
    def _():
        st_ref[...] = jnp.zeros_like(st_ref)

    zg = _dot(z_ref[:, 768:896].astype(bf16), wg_ref[...]) + bg_ref[...]
    log_a = (jnp.minimum(zg, 0.0) - jnp.log1p(jnp.exp(-jnp.abs(zg)))) * (1.0 / GLA_GATE_TEMP)

    tri = (_iota((c_len, c_len), 0) >= _iota((c_len, c_len), 1)).astype(f32)
    for c in range(n_chunks):
        cum_ref[c * c_len:(c + 1) * c_len, :] = _dot_sel(tri, log_a[c * c_len:(c + 1) * c_len])

    expand = (_iota((GLA_QK, GLA_V), 0) // GLA_DK == _iota((GLA_QK, GLA_V), 1) // GLA_DV).astype(bf16)
    st_mask = (_iota((GLA_V, GLA_QK), 0) // GLA_DV == _iota((GLA_V, GLA_QK), 1) // GLA_DK).astype(f32)
    sub = GLA_SUB
    row = _iota((sub, GLA_QK), 0)

    def intra(c):
        base = c * c_len
        rows = slice(base, base + c_len)
        cum = cum_ref[rows, :]
        q = z_ref[rows, 0:128] * (GLA_DK ** -0.5)
        k = z_ref[rows, 128:256]
        v = z_ref[rows, 256:512]
        v_t = v.T.astype(bf16)
        qd_ref[rows, :] = (q * jnp.exp(cum)).astype(bf16)
        last = cum[c_len - 1:c_len, :]
        dl_ref[c] = jnp.broadcast_to(jnp.exp(last), (8, GLA_QK))
        upd_ref[c] = _dot(v_t, (k * jnp.exp(last - cum)).astype(bf16)) * st_mask

        n_sub = c_len // sub
        parts = [None]
        for sb in range(1, n_sub):
            r0 = sb * sub
            kx = k[0:r0] * jnp.exp(cum[r0 - 1:r0, :] - cum[0:r0])
            parts.append((_dot(v_t[:, 0:r0], kx.astype(bf16)) * st_mask).astype(bf16))
        rs = []
        for sb in range(n_sub):
            r0 = sb * sub
            q_i = q[r0:r0 + sub]
            cum_i = cum[r0:r0 + sub]
            ts = []
            for jj in range(sub):
                kj = z_ref[base + r0 + jj:base + r0 + jj + 1, 128:256]
                cj = cum_ref[base + r0 + jj:base + r0 + jj + 1, :]
                dec = jnp.exp(jnp.where(row >= jj, cum_i - cj, -1e30))
                ts.append((q_i * kj * dec).astype(bf16))
            rs.append(_dot(jnp.concatenate(ts, axis=0), expand))
        blocks = []
        for sb in range(n_sub):
            r0 = sb * sub
            o_sb = jnp.zeros((sub, GLA_V), f32)
            for jj in range(sub):
                o_sb = o_sb + rs[sb][jj * sub:(jj + 1) * sub] * z_ref[base + r0 + jj:base + r0 + jj + 1, 256:512]
            if sb > 0:
                qx = q[r0:r0 + sub] * jnp.exp(cum[r0:r0 + sub] - cum[r0 - 1:r0, :])
                o_sb = o_sb + _dot_nt(qx.astype(bf16), parts[sb])
            blocks.append(o_sb)
        oraw_ref[rows, :] = jnp.concatenate(blocks, axis=0)

    for c in range(n_chunks):
        intra(c)

    def recur(c, st):
        stc_ref[c] = st.astype(bf16)
        return dl_ref[c][0:1, :] * st + upd_ref[c]

    st_ref[...] = lax.fori_loop(0, n_chunks, recur, st_ref[...])
    for c in range(n_chunks):
        rows = slice(c * c_len, (c + 1) * c_len)
        oraw_ref[rows, :] += _dot_nt(qd_ref[rows, :], stc_ref[c])

    o = oraw_ref[...]
    ms = _group_stat(o * o, _group_mean_matrix(GLA_V, GLA_DV))
    og = z_ref[:, 512:768]
    o_ref[...] = (o * lax.rsqrt(ms + NORM_EPS) * no_ref[...] * (og * _sigmoid(og))).astype(bf16)


def _gla(z_gla, wg, bg, no, batch, seq):
    tc = SCAN_TILE
    nt = seq // tc
    n_chunks = tc // GLA_CHUNK
    return pl.pallas_call(
        _gla_kernel,
        out_shape=jax.ShapeDtypeStruct((batch * seq, GLA_V), bf16),
        grid=(batch, nt),
        in_specs=[pl.BlockSpec((tc, W_GLA), lambda b, i: (b * nt + i, 0)),
                  _const_spec((128, GLA_QK)),
                  _const_spec((1, GLA_QK)),
                  _const_spec((1, GLA_V))],
        out_specs=pl.BlockSpec((tc, GLA_V), lambda b, i: (b * nt + i, 0)),
        scratch_shapes=[pltpu.VMEM((GLA_V, GLA_QK), f32),
                        pltpu.VMEM((tc, GLA_QK), f32),
                        pltpu.VMEM((tc, GLA_QK), bf16),
                        pltpu.VMEM((n_chunks, GLA_V, GLA_QK), f32),
                        pltpu.VMEM((n_chunks, 8, GLA_QK), f32),
                        pltpu.VMEM((n_chunks, GLA_V, GLA_QK), bf16),
                        pltpu.VMEM((tc, GLA_V), f32)],
        compiler_params=_cparams(2),
        name="gla",
    )(z_gla, wg, bg, no)


DN_W = DN_HEADS * DN_DK
DN_PAIRS = DN_HEADS // 2


def _stack_heads(x):
    lane = _iota(x.shape, 1)
    zero = jnp.zeros_like(x)
    return jnp.concatenate([jnp.where(lane < DN_DK, x, zero), jnp.where(lane >= DN_DK, x, zero)], axis=0)


def _dot_3x(a, b):
    a_hi = a.astype(bf16)
    b_hi = b.astype(bf16)
    a_lo = (a - a_hi.astype(f32)).astype(bf16)
    b_lo = (b - b_hi.astype(f32)).astype(bf16)
    return _dot(a_hi, b_hi) + (_dot(a_hi, b_lo) + _dot(a_lo, b_hi))


def _dn_pre_kernel(y_ref, ab_ref, alog_ref, dtb_ref,
                   a_ref, attn_ref, rhs_ref, qd_ref, kd_ref, egl_ref,
                   q_s, k_s, beta_s, g_s, gam_s, gamr_s):
    tc = y_ref.shape[0]
    c_len = DN_CHUNK

    sum_mat = _group_mean_matrix(DN_W, DN_DK) * float(DN_DK)
    q = y_ref[:, 0:DN_W]
    k = y_ref[:, DN_W:2 * DN_W]
    q_s[...] = q * lax.rsqrt(_group_stat(q * q, sum_mat) + NORM_EPS) * (DN_DK ** -0.5)
    k_s[...] = k * lax.rsqrt(_group_stat(k * k, sum_mat) + NORM_EPS)

    ab = ab_ref[...]
    lane_head = _iota((tc, DN_W), 1) // DN_DK
    a_rep = jnp.zeros((tc, DN_W), f32)
    b_rep = jnp.zeros((tc, DN_W), f32)
    for h in range(DN_HEADS):
        a_rep = jnp.where(lane_head == h, jnp.broadcast_to(ab[:, h:h + 1], (tc, DN_W)), a_rep)
        b_rep = jnp.where(lane_head == h,
                          jnp.broadcast_to(ab[:, DN_HEADS + h:DN_HEADS + h + 1], (tc, DN_W)), b_rep)
    beta_s[...] = _sigmoid(b_rep)
    g_s[...] = -jnp.exp(alog_ref[...]) * _softplus(a_rep + dtb_ref[...])

    r64 = _iota((c_len, c_len), 0)
    c64 = _iota((c_len, c_len), 1)
    tri = (r64 >= c64).astype(bf16)
    ones = jnp.ones((c_len, c_len), bf16)
    row = _iota((c_len, DN_W), 0)
    colj = _iota((c_len, DN_W), 1) % c_len
    incl = row >= colj
    strict = row > colj

    n_chunks = tc // c_len
    g_parts = _split_bf16(g_s[...], 3)
    zero_b = jnp.zeros((c_len, DN_W), bf16)
    total = lambda parts: functools.reduce(lambda a, b: a + b, reversed(parts))
    for c in range(n_chunks):
        rows = slice(c * c_len, (c + 1) * c_len)
        gam_s[rows, :] = total([_dot(tri, part[rows]) for part in g_parts])
    for c in range(n_chunks):
        rows = slice(c * c_len, (c + 1) * c_len)
        gamr_s[rows, :] = total([_dot(ones, jnp.where(row <= colj, part[rows], zero_b)) for part in g_parts])

    def chunk(c):
        rows = slice(c * c_len, (c + 1) * c_len)
        gam = gam_s[rows, :]
        gam_row = gamr_s[rows, :]
        dec_incl = jnp.exp(jnp.where(incl, gam - gam_row, -1e30))
        dec_strict = jnp.where(strict, dec_incl, 0.0)
        eg = jnp.exp(gam)
        g_last = gam[c_len - 1:c_len, :]
        q = q_s[rows, :]
        k = k_s[rows, :]
        beta = beta_s[rows, :]
        bv = beta * y_ref[rows, 2 * DN_W:3 * DN_W]
        bk = beta * eg * k
        kdec = k * jnp.exp(g_last - gam)
        qd_ref[rows, :] = (q * eg).astype(bf16)
        egl_ref[c] = jnp.broadcast_to(jnp.exp(g_last), (8, DN_W))
        qb = q.astype(bf16)
        kb = k.astype(bf16)
        for p in range(DN_PAIRS):
            ps = slice(p * 128, (p + 1) * 128)
            kbp = kb[:, ps]
            k_bd = _stack_heads(kbp)
            a_pair = beta[:, ps] * _dot_nt(kbp, k_bd) * dec_strict[:, ps]
            a_ref[c * DN_HEADS + 2 * p] = a_pair[:, :DN_CHUNK]
            a_ref[c * DN_HEADS + 2 * p + 1] = a_pair[:, DN_CHUNK:]
            attn_ref[c * DN_PAIRS + p] = (_dot_nt(qb[:, ps], k_bd) * dec_incl[:, ps]).astype(bf16)
            kd_ref[c * DN_PAIRS + p] = _stack_heads(kdec[:, ps]).T.astype(bf16)
            rhs_ref[c * DN_PAIRS + p] = jnp.concatenate(
                [_stack_heads(bv[:, ps]), _stack_heads(bk[:, ps])], axis=1)

    for c in range(n_chunks):
        chunk(c)


def _dn_pre(z_dn, alog, dtb, batch, seq):
    tc = SCAN_TILE
    nt = seq // tc
    t = batch * seq
    npt = (tc // DN_CHUNK) * DN_PAIRS
    n_pair = (t // DN_CHUNK) * DN_PAIRS
    scr = lambda: pltpu.VMEM((tc, DN_W), f32)
    tok = lambda: pl.BlockSpec((tc, DN_W), lambda b, i: (b * nt + i, 0))
    mat = lambda r, w: pl.BlockSpec((npt, r, w), lambda b, i: (b * nt + i, 0, 0))
    return pl.pallas_call(
        _dn_pre_kernel,
        out_shape=(jax.ShapeDtypeStruct((2 * n_pair, DN_CHUNK, DN_CHUNK), f32),
                   jax.ShapeDtypeStruct((n_pair, DN_CHUNK, 128), bf16),
                   jax.ShapeDtypeStruct((n_pair, 128, 256), f32),
                   jax.ShapeDtypeStruct((t, DN_W), bf16),
                   jax.ShapeDtypeStruct((n_pair, 128, 128), bf16),
                   jax.ShapeDtypeStruct((t // DN_CHUNK, 8, DN_W), f32)),
        grid=(batch, nt),
        in_specs=[pl.BlockSpec((tc, 3 * DN_W), lambda b, i: (b * nt + i, 0)),
                  pl.BlockSpec((tc, 128), lambda b, i: (b * nt + i, 4 * DN_W // 128)),
                  _const_spec((1, DN_W)),
                  _const_spec((1, DN_W))],
        out_specs=(pl.BlockSpec((2 * npt, DN_CHUNK, DN_CHUNK), lambda b, i: (b * nt + i, 0, 0)),
                   mat(DN_CHUNK, 128), mat(128, 256), tok(), mat(128, 128),
                   pl.BlockSpec((tc // DN_CHUNK, 8, DN_W), lambda b, i: (b * nt + i, 0, 0))),
        scratch_shapes=[scr(), scr(), scr(), scr(), scr(), scr()],
        compiler_params=_cparams(2),
        name="dn_pre",
    )(z_dn, z_dn, alog, dtb)


def _dn_solve_kernel(a_ref, t_ref):
    c_len = DN_CHUNK
    blk = DN_INV_BLOCK
    vshape = a_ref.shape[1:]

    def row_pair(ip, carry):
        i0 = 2 * ip
        i1 = i0 + 1
        for mb in range(c_len // blk):
            m0 = mb * blk

            def col_j(j, accs):
                a0 = a_ref[i0 * c_len + j]
                a1 = a_ref[i1 * c_len + j]
                ts = [t_ref[j * c_len + m0 + m] for m in range(blk)]
                return (tuple(accs[0][m] - a0 * ts[m] for m in range(blk)),
                        tuple(accs[1][m] - a1 * ts[m] for m in range(blk)))

            zeros = tuple(jnp.zeros(vshape, f32) for _ in range(blk))
            acc0, acc1 = lax.fori_loop(m0, i0, col_j, (zeros, zeros))
            a10 = a_ref[i1 * c_len + i0]
            for m in range(blk):
                t0 = acc0[m] + jnp.where(i0 == m0 + m, 1.0, 0.0)
                t_ref[i0 * c_len + m0 + m] = t0
                t_ref[i1 * c_len + m0 + m] = acc1[m] - a10 * t0 + jnp.where(i1 == m0 + m, 1.0, 0.0)
        return carry

    lax.fori_loop(0, c_len // 2, row_pair, 0)


def _dn_solve(a_t):
    n_ent, n_grp, lanes = a_t.shape
    g = min(n_grp, 8)
    spec = lambda: pl.BlockSpec((n_ent, g, lanes), lambda i: (0, i, 0), pipeline_mode=pl.Buffered(1))
    return pl.pallas_call(
        _dn_solve_kernel,
        out_shape=jax.ShapeDtypeStruct(a_t.shape, f32),
        grid=(n_grp // g,),
        in_specs=[spec()],
        out_specs=spec(),
        compiler_params=_cparams(1),
        name="dn_solve",
    )(a_t)


DN_SCAN_SEQS = 2


def _dn_scan_kernel(*refs):
    (attn_ref, rhs_ref, qd_ref, kd_ref, egl_ref, zg_ref, no_ref, o_ref, st_ref, oraw_ref) = refs[-10:]
    t_refs = refs[:-10]
    n_seq, tc = qd_ref.shape[0], qd_ref.shape[1]
    assert len(t_refs) == n_seq
    c_len = DN_CHUNK
    seqs = range(n_seq)
    pairs = range(DN_PAIRS)
    psl = [slice(p * 128, (p + 1) * 128) for p in pairs]

    @pl.when(pl.program_id(1) == 0)
    def _():
        st_ref[...] = jnp.zeros_like(st_ref)

    def solve(s, c):
        t_pair = [jnp.concatenate([t_refs[s][c * DN_HEADS + 2 * p], t_refs[s][c * DN_HEADS + 2 * p + 1]], axis=-1)
                  for p in pairs]
        return [_dot_3x(t_pair[p], rhs_ref[s, c * DN_PAIRS + p]) for p in pairs]

    n_chunks = tc // c_len
    sols = [solve(s, 0) for s in seqs]
    for c in range(n_chunks):
        rows = slice(c * c_len, (c + 1) * c_len)
        sts = [[st_ref[s, p] for p in pairs] for s in seqs]
        xq = [[_dot(jnp.concatenate([sols[s][p][:, 128:].astype(bf16), qd_ref[s, rows, psl[p]]], axis=0),
                    sts[s][p].astype(bf16)) for p in pairs] for s in seqs]
        nxt = [solve(s, c + 1) for s in seqs] if c + 1 < n_chunks else None
        u_bd = [[_stack_heads((sols[s][p][:, :128] - xq[s][p][:c_len]).astype(bf16)) for p in pairs]
                for s in seqs]
        au = [[_dot(jnp.concatenate([attn_ref[s, c * DN_PAIRS + p], kd_ref[s, c * DN_PAIRS + p]], axis=0),
                    u_bd[s][p]) for p in pairs] for s in seqs]
        for s in seqs:
            eg_last = egl_ref[s, c][0:1, :]
            for p in pairs:
                st_ref[s, p] = eg_last[:, psl[p]] * sts[s][p] + au[s][p][c_len:]
            oraw_ref[s, rows, :] = jnp.concatenate([xq[s][p][c_len:] + au[s][p][:c_len] for p in pairs], axis=-1)
        sols = nxt

    mean_mat = _group_mean_matrix(DN_W, DN_DV)
    for s in seqs:
        o = oraw_ref[s]
        ms = _group_stat(o * o, mean_mat)
        zg = zg_ref[s]
        o_ref[s] = (o * lax.rsqrt(ms + NORM_EPS) * no_ref[...] * (zg * _sigmoid(zg))).astype(bf16)


def _dn_scan(t_all, attn, rhs, qd, kd, egl, z_dn, no, batch, seq):
    tc = SCAN_TILE
    nt = seq // tc
    ns = DN_SCAN_SEQS if batch % DN_SCAN_SEQS == 0 else 1
    g = batch // ns
    npt = (tc // DN_CHUNK) * DN_PAIRS
    n_chunk_seq = seq // DN_CHUNK

    def grouped(x, per_seq):
        return x.reshape((g, ns, per_seq) + x.shape[1:])

    def spec(per_tile, tail, last_block=0):
        zeros = (0,) * (len(tail) - 1)
        return pl.BlockSpec((None, ns, per_tile) + tail, lambda b, i: (b, 0, i) + zeros + (last_block,))

    out = pl.pallas_call(
        _dn_scan_kernel,
        out_shape=jax.ShapeDtypeStruct((g, ns, seq, DN_W), bf16),
        grid=(g, nt),
        in_specs=[pl.BlockSpec((2 * npt, DN_CHUNK, DN_CHUNK), lambda b, i, s=s: ((b * ns + s) * nt + i, 0, 0))
                  for s in range(ns)] + [
                  spec(npt, (DN_CHUNK, 128)), spec(npt, (128, 256)), spec(tc, (DN_W,)), spec(npt, (128, 128)),
                  spec(tc // DN_CHUNK, (8, DN_W)),
                  spec(tc, (DN_W,), last_block=3),
                  _const_spec((1, DN_W))],
        out_specs=spec(tc, (DN_W,)),
        scratch_shapes=[pltpu.VMEM((ns, DN_PAIRS, 128, 128), f32),
                        pltpu.VMEM((ns, tc, DN_W), f32)],
        compiler_params=_cparams(2),
        name="dn_scan",
    )(*([t_all] * ns), grouped(attn, n_chunk_seq * DN_PAIRS),
      grouped(rhs, n_chunk_seq * DN_PAIRS), grouped(qd, seq), grouped(kd, n_chunk_seq * DN_PAIRS),
      grouped(egl, n_chunk_seq), grouped(z_dn, seq), no)
    return out.reshape(batch * seq, DN_W)


def _dn(z_dn, alog, dtb, no, batch, seq):
    a_all, attn, rhs, qd, kd, egl = _dn_pre(z_dn, alog, dtb, batch, seq)
    n_prob = a_all.shape[0]
    n_ent = DN_CHUNK * DN_CHUNK
    a_t = a_all.reshape(n_prob, n_ent).T.reshape(n_ent, n_prob // 128, 128)
    t_all = _dn_solve(a_t).reshape(n_ent, n_prob).T.reshape(n_prob, DN_CHUNK, DN_CHUNK)
    return _dn_scan(t_all, attn, rhs, qd, kd, egl, z_dn, no, batch, seq)


def _merge_kernel(x_ref, g_ref, oa_ref, ob_ref, oc_ref, od_ref, wg_ref, wb_ref, wo_ref, out_ref):
    x = x_ref[...]
    h = _rms(x, g_ref[...]).astype(bf16)
    merged = jnp.zeros(x.shape, f32)
    for n, o_ref in enumerate((oa_ref, ob_ref, oc_ref, od_ref)):
        gate = _sigmoid(_dot(h, wg_ref[:, n * D_MODEL:(n + 1) * D_MODEL]))
        merged = merged + gate * _dot(o_ref[...], wb_ref[n])
    out_ref[...] = x + _dot(merged.astype(bf16), wo_ref[...])


def _merge(x2, g, oa, ob, oc, od, wg, wb, wo):
    t = x2.shape[0]
    tm = TOKEN_TILE
    tok = lambda w: pl.BlockSpec((tm, w), lambda i: (i, 0))
    return pl.pallas_call(
        _merge_kernel,
        out_shape=jax.ShapeDtypeStruct((t, D_MODEL), f32),
        grid=(t // tm,),
        in_specs=[tok(D_MODEL), _const_spec((1, D_MODEL)),
                  tok(BRANCH_WIDTH), tok(BRANCH_WIDTH), tok(BRANCH_WIDTH), tok(BRANCH_WIDTH),
                  _const_spec((D_MODEL, N_BRANCH * D_MODEL)),
                  _const_spec((N_BRANCH, BRANCH_WIDTH, D_MODEL)),
                  _const_spec((D_MODEL, D_MODEL))],
        out_specs=tok(D_MODEL),
        compiler_params=_cparams(1),
        name="merge",
    )(x2, g, oa, ob, oc, od, wg, wb, wo)


def _memkv_kernel(mem_ref, g_ref, wk_ref, wv_ref, k_ref, v_ref):
    mn = _rms(mem_ref[...], g_ref[...]).astype(bf16)
    k_ref[...] = _dot(mn, wk_ref[...]).astype(bf16)
    v_ref[...] = _dot(mn, wv_ref[...]).astype(bf16)


def _memkv(mem2, g, wk, wv, mem_len):
    t = mem2.shape[0]
    blk = lambda: pl.BlockSpec((mem_len, D_MODEL), lambda i: (i, 0))
    return pl.pallas_call(
        _memkv_kernel,
        out_shape=(jax.ShapeDtypeStruct((t, D_MODEL), bf16), jax.ShapeDtypeStruct((t, D_MODEL), bf16)),
        grid=(t // mem_len,),
        in_specs=[blk(), _const_spec((1, D_MODEL)),
                  _const_spec((D_MODEL, D_MODEL)), _const_spec((D_MODEL, D_MODEL))],
        out_specs=(blk(), blk()),
        compiler_params=_cparams(1),
        name="mem_kv",
    )(mem2, g, wk, wv)


def _xattn_kernel(x_ref, g_ref, k_ref, v_ref, wq_ref, wo_ref, out_ref):
    x = x_ref[...]
    h = _rms(x, g_ref[...]).astype(bf16)
    q = (_dot(h, wq_ref[...]) * (XATTN_DIM ** -0.5)).astype(bf16)
    hsl = [slice(hh * XATTN_DIM, (hh + 1) * XATTN_DIM) for hh in range(XATTN_HEADS)]
    scores = [_dot_nt(q[:, sl], k_ref[:, sl]) for sl in hsl]
    probs = []
    for s in scores:
        p = jnp.exp(s - jnp.max(s, axis=-1, keepdims=True))
        probs.append((p / jnp.sum(p, axis=-1, keepdims=True)).astype(bf16))
    o = jnp.concatenate([_dot(p, v_ref[:, sl]).astype(bf16) for p, sl in zip(probs, hsl)], axis=-1)
    out_ref[...] = x + _dot(o, wo_ref[...])


def _xattn(x2, g, kmem, vmem, wq, wo, batch, seq, mem_len):
    tm = TOKEN_TILE
    nt = seq // tm
    tok = pl.BlockSpec((tm, D_MODEL), lambda b, i: (b * nt + i, 0))
    memb = lambda: pl.BlockSpec((mem_len, D_MODEL), lambda b, i: (b, 0))
    return pl.pallas_call(
        _xattn_kernel,
        out_shape=jax.ShapeDtypeStruct((batch * seq, D_MODEL), f32),
        grid=(batch, nt),
        in_specs=[tok, _const_spec((1, D_MODEL)), memb(), memb(),
                  _const_spec((D_MODEL, D_MODEL)), _const_spec((D_MODEL, D_MODEL))],
        out_specs=tok,
        compiler_params=_cparams(2),
        name="xattn",
    )(x2, g, kmem, vmem, wq, wo)


def _mlp_kernel(x_ref, g_ref, wu_ref, wd_ref, gf_ref, out_ref, *, final):
    x = x_ref[...]
    h = _rms(x, g_ref[...]).astype(bf16)
    acc = x
    for c in range(MLP_HIDDEN // MLP_CHUNK):
        sl = slice(c * MLP_CHUNK, (c + 1) * MLP_CHUNK)
        a = jnp.maximum(_dot(h, wu_ref[:, sl]), 0.0)
        acc = acc + _dot((a * a).astype(bf16), wd_ref[sl, :])
    if final:
        acc = _rms(acc, gf_ref[...])
    out_ref[...] = acc


def _mlp(x2, g, wu, wd, gf, final):
    t = x2.shape[0]
    tm = TOKEN_TILE
    tok = pl.BlockSpec((tm, D_MODEL), lambda i: (i, 0))
    return pl.pallas_call(
        functools.partial(_mlp_kernel, final=final),
        out_shape=jax.ShapeDtypeStruct((t, D_MODEL), f32),
        grid=(t // tm,),
        in_specs=[tok, _const_spec((1, D_MODEL)),
                  _const_spec((D_MODEL, MLP_HIDDEN)), _const_spec((MLP_HIDDEN, D_MODEL)),
                  _const_spec((1, D_MODEL))],
        out_specs=tok,
        compiler_params=_cparams(1),
        name="mlp",
    )(x2, g, wu, wd, gf)


def _pad_cols(w, width):
    return jnp.pad(w, ((0, 0), (0, width - w.shape[1])))


def _prep_w_in(w_in):
    col = lambda i: w_in[:, IN_OFFS[i]:IN_OFFS[i + 1]]
    zeros = lambda n: jnp.zeros((D_MODEL, n), w_in.dtype)
    (cq, ckv, kpe, sg_uv, gq, gk, gv, g_lr, g_og, dq, dk, dv, da, db, dz) = (col(i) for i in range(15))
    small = jnp.concatenate([
        cq, ckv, zeros(MLA_NOPE), kpe, zeros(MLA_HEAD_PAD - MLA_NOPE - MLA_ROPE),
        sg_uv,
        gq, gk, gv, g_og, _pad_cols(g_lr, 128),
        dq, dk, dv, dz, _pad_cols(jnp.concatenate([da, db], axis=1), 128),
    ], axis=1)
    assert small.shape[1] == W_SMALL
    return small.astype(bf16), col(15).astype(bf16)


def _rope_inv_freq():
    inv_freq = ROPE_THETA ** (-jnp.arange(0, MLA_ROPE, 2, dtype=f32) / MLA_ROPE)
    return inv_freq[:, None]


def kernel(x, mem, positions, norm_mix, w_in, mla_norm_q, mla_norm_kv, mla_w_uq, mla_w_ukv, sg_ln_g, sg_ln_b, sg_w, sg_b, gla_w_gate, gla_b_gate, gla_norm, dn_conv, dn_a_log, dn_dt_bias, dn_norm, w_branch, w_out, norm_xattn, norm_mem, xattn_wq, xattn_wk, xattn_wv, xattn_wo, norm_mlp, w_up, w_down, norm_final):
    batch, seq, _ = x.shape
    mem_len = mem.shape[1]
    depth = w_in.shape[0]
    assert seq % TOKEN_TILE == 0 and seq % SCAN_TILE == 0 and seq % ATTN_TILE == 0
    x2 = x.reshape(batch * seq, D_MODEL)
    mem2 = mem.reshape(batch * mem_len, D_MODEL)
    pos2 = positions.reshape(batch * seq // TOKEN_TILE, 1, TOKEN_TILE)
    invf = _rope_inv_freq()
    row = lambda v: v[None, :]

    for l in range(depth):
        w_small, w_gates = _prep_w_in(w_in[l])
        z_mla, z_sg, z_gla, z_dn = _inproj(x2, row(norm_mix[l]), w_small, dn_conv[l], seq)

        wuq = jnp.pad(mla_w_uq[l].reshape(MLA_Q_RANK, MLA_HEADS, MLA_NOPE + MLA_ROPE),
                      ((0, 0), (0, 0), (0, MLA_HEAD_PAD - MLA_NOPE - MLA_ROPE)))
        wuq = wuq.reshape(MLA_Q_RANK, MLA_HEADS * MLA_HEAD_PAD).astype(bf16)
        wukv = mla_w_ukv[l].reshape(MLA_KV_RANK, MLA_HEADS, MLA_NOPE + MLA_V)
        wuk = jnp.pad(wukv[:, :, :MLA_NOPE], ((0, 0), (0, 0), (0, MLA_HEAD_PAD - MLA_NOPE)))
        wuk = wuk.reshape(MLA_KV_RANK, MLA_HEADS * MLA_HEAD_PAD).astype(bf16)
        wuv = wukv[:, :, MLA_NOPE:].reshape(MLA_KV_RANK, MLA_HEADS * MLA_V).astype(bf16)
        q_a, k_a, v_a = _mla_pre(z_mla, pos2, invf, row(mla_norm_q[l]), row(mla_norm_kv[l]), wuq, wuk, wuv,
                                 batch, seq)
        o_a = _mla_attn(q_a, k_a, v_a, batch, seq)

        o_b = _sg(z_sg, row(sg_ln_g[l]), row(sg_ln_b[l]), sg_w[l], sg_b[l].T)

        wg = jnp.pad(gla_w_gate[l], ((0, 128 - GLA_GATE_RANK), (0, 0))).astype(bf16)
        o_c = _gla(z_gla, wg, row(gla_b_gate[l]), row(jnp.tile(gla_norm[l], GLA_HEADS)), batch, seq)

        o_d = _dn(z_dn, row(jnp.repeat(dn_a_log[l], DN_DK)), row(jnp.repeat(dn_dt_bias[l], DN_DK)),
                  row(jnp.tile(dn_norm[l], DN_HEADS)), batch, seq)

        x2 = _merge(x2, row(norm_mix[l]), o_a, o_b, o_c, o_d, w_gates,
                    w_branch[l].astype(bf16), w_out[l].astype(bf16))

        k_m, v_m = _memkv(mem2, row(norm_mem[l]), xattn_wk[l].astype(bf16), xattn_wv[l].astype(bf16), mem_len)
        x2 = _xattn(x2, row(norm_xattn[l]), k_m, v_m, xattn_wq[l].astype(bf16), xattn_wo[l].astype(bf16),
                    batch, seq, mem_len)
        x2 = _mlp(x2, row(norm_mlp[l]), w_up[l].astype(bf16), w_down[l].astype(bf16), row(norm_final),
                  final=(l == depth - 1))
    return x2.reshape(batch, seq, D_MODEL)
```
